```python
import jax, jax.numpy as jnp
from jax import lax
import numpy as np

D_MODEL = 1024
BATCH = 8
SEQ = 2048
DEPTH = 1
DEC_BATCH = 128
DEC_SEQ = 8
PAST_LEN = 16384
PAGE_SIZE = 128

N_META = 16
D_RNN = D_MODEL
LRU_BLOCKS = 8
LRU_BLOCK = D_RNN // LRU_BLOCKS
CONV_W = 4
LRU_C = 8.0
HG_HEADS = 8
HG_DK = 128
HG_DV = D_MODEL // HG_HEADS
D_HG_K = HG_HEADS * HG_DK
D_HG_V = HG_HEADS * HG_DV
HG_CHUNK = 32
D_FF = 4 * D_MODEL
EPS = 1e-6
COL_WIDTHS = (D_RNN, D_RNN, D_HG_K, D_HG_K, D_HG_V, D_HG_V, D_MODEL, D_MODEL)
D_IN = D_RNN * 2 + D_HG_K * 2 + D_HG_V * 2 + D_MODEL * 2

kernel_name = 'hybrid_rglru_hgrn2_meta_decode_step'


def rms_norm(x, g):
    xf = x.astype(jnp.float32)
    y = xf * lax.rsqrt(jnp.mean(xf * xf, axis=-1, keepdims=True) + EPS)
    return (y * g.astype(jnp.float32)).astype(x.dtype)


def causal_conv(u, buf, w, b):
    L = u.shape[1]
    up = jnp.concatenate([buf.astype(u.dtype), u], axis=1)
    out = b + w[0] * up[:, 0:L]
    for k in range(1, CONV_W):
        out = out + w[k] * up[:, k:k + L]
    return out, up[:, L:]


def block_diag(u, w, b):
    ub = u.reshape(u.shape[:-1] + (LRU_BLOCKS, LRU_BLOCK))
    return jnp.einsum('blhi,hij->blhj', ub, w.astype(jnp.float32)).reshape(u.shape) + b.astype(jnp.float32)


def rg_lru(u, h0, w_r, b_r, w_i, b_i, lam, reset_first):
    uf = u.astype(jnp.float32)
    r = jax.nn.sigmoid(block_diag(uf, w_r, b_r))
    ig = jax.nn.sigmoid(block_diag(uf, w_i, b_i))
    log_a = -LRU_C * r * jax.nn.softplus(-lam.astype(jnp.float32))
    a = jnp.exp(log_a)
    mult = jnp.sqrt(-jnp.expm1(2.0 * log_a))
    if reset_first:
        mult = mult.at[:, 0].set(1.0)
    bterm = mult * ig * uf

    def combine(c1, c2):
        a1, b1 = c1
        a2, b2 = c2
        return a1 * a2, a2 * b1 + b2

    a_cum, b_cum = lax.associative_scan(combine, (a, bterm), axis=1)
    h = a_cum * h0.astype(jnp.float32)[:, None] + b_cum
    return h, h[:, -1]


def hgrn2_chunk(S, q, k, v, logf):
    C = q.shape[1]
    b = jnp.cumsum(logf, axis=1)
    o_inter = jnp.einsum('bthk,bhkv->bthv', q * jnp.exp(b), S)
    causal = jnp.tril(jnp.ones((C, C), dtype=bool))
    diff = b[:, :, None] - b[:, None, :]
    decay = jnp.where(causal[None, :, :, None, None], jnp.exp(jnp.minimum(diff, 0.0)), 0.0)
    A = jnp.einsum('bthk,btshk,bshk->bhts', q, decay, k)
    o_intra = jnp.einsum('bhts,bshv->bthv', A, v)
    b_last = b[:, -1]
    S_new = jnp.exp(b_last)[..., None] * S + jnp.einsum('bshk,bshv->bhkv', k * jnp.exp(b_last[:, None] - b), v)
    return S_new, o_inter + o_intra


def hgrn2(q, f, i, S0, lb, meta_lead):
    Bn, L, _ = q.shape
    qf = jax.nn.silu(q.astype(jnp.float32)).reshape(Bn, L, HG_HEADS, HG_DK)
    fg = lb + (1.0 - lb) * jax.nn.sigmoid(f.astype(jnp.float32))
    logf = jnp.log(fg).reshape(Bn, L, HG_HEADS, HG_DK)
    k = (1.0 - fg).reshape(Bn, L, HG_HEADS, HG_DK)
    v = i.astype(jnp.float32).reshape(Bn, L, HG_HEADS, HG_DV)
    S0 = S0.astype(jnp.float32)
    if meta_lead:
        S, o_meta = hgrn2_chunk(S0, qf[:, :N_META], k[:, :N_META], v[:, :N_META], logf[:, :N_META])
        rest = L - N_META
        n_chunks = rest // HG_CHUNK

        def to_chunks(t):
            return t[:, N_META:].reshape((Bn, n_chunks, HG_CHUNK) + t.shape[2:]).swapaxes(0, 1)

        def step(S_c, xs):
            return hgrn2_chunk(S_c, *xs)

        S, o_rest = lax.scan(step, S, (to_chunks(qf), to_chunks(k), to_chunks(v), to_chunks(logf)))
        o_rest = o_rest.swapaxes(0, 1).reshape(Bn, rest, HG_HEADS, HG_DV)
        o = jnp.concatenate([o_meta, o_rest], axis=1)
    else:
        S, o = hgrn2_chunk(S0, qf, k, v, logf)
    return o, S


def layer(x, conv_buf, h0, S0, is_prompt, lb, gains, w_in, conv_w, conv_b, rg_w, rg_b, ig_w, ig_b,
          lru_lambda, hg_gnorm, w_branch_a, w_branch_b, w_out, w_up, w_down):
    Bn, L, _ = x.shape
    xn = rms_norm(x, gains[0])
    proj = xn @ w_in
    split_idx = np.cumsum(np.array(COL_WIDTHS))[:-1].tolist()
    u, gate_a, q, f, i, og, m_a, m_b = jnp.split(proj, split_idx, axis=-1)
    uc, conv_new = causal_conv(u, conv_buf, conv_w, conv_b)
    h, h_last = rg_lru(uc, h0, rg_w, rg_b, ig_w, ig_b, lru_lambda, is_prompt)
    y_a = h.astype(x.dtype) * jax.nn.gelu(gate_a)
    o, S_new = hgrn2(q, f, i, S0, lb, is_prompt)
    o = rms_norm(o, hg_gnorm.reshape(HG_HEADS, HG_DV)).reshape(Bn, L, D_HG_V)
    y_b = o.astype(x.dtype) * jax.nn.silu(og)
    mixed = jax.nn.sigmoid(m_a) * (y_a @ w_branch_a) + jax.nn.sigmoid(m_b) * (y_b @ w_branch_b)
    x = x + rms_norm(mixed @ w_out, gains[1])
    hn = rms_norm(x, gains[2])
    ff = jnp.square(jax.nn.relu(hn @ w_up)) @ w_down
    x = x + rms_norm(ff, gains[3])
    return x, conv_new, h_last, S_new


def setup_inputs(seed: int = 0) -> dict:
    key = jax.random.key(seed)
    ks = jax.random.split(key, 24)
    nrm = jax.random.normal
    f32 = jnp.float32
    u = jax.random.uniform(ks[13], (DEPTH, D_RNN), f32, 0.9, 0.999) ** (1.0 / LRU_C)
    return {
        'x_prompt': nrm(ks[0], (BATCH, SEQ, D_MODEL), f32),
        'x_sample': nrm(ks[1], (DEC_BATCH, DEC_SEQ, D_MODEL), f32),
        'state_conv': nrm(ks[2], (DEPTH, DEC_BATCH, CONV_W - 1, D_RNN), f32),
        'state_rglru': nrm(ks[3], (DEPTH, DEC_BATCH, D_RNN), f32),
        'state_hgrn': 0.5 * nrm(ks[4], (DEPTH, DEC_BATCH, HG_HEADS, HG_DK, HG_DV), f32),
        'meta_tokens': nrm(ks[5], (N_META, D_MODEL), f32),
        'norm_gains': 1.0 + 0.05 * nrm(ks[6], (DEPTH, 4, D_MODEL), f32),
        'w_in': nrm(ks[7], (DEPTH, D_MODEL, D_IN), f32) * D_MODEL ** -0.5,
        'conv_w': nrm(ks[8], (DEPTH, CONV_W, D_RNN), f32) * CONV_W ** -0.5,
        'conv_b': 0.02 * nrm(ks[9], (DEPTH, D_RNN), f32),
        'rg_w': nrm(ks[10], (DEPTH, LRU_BLOCKS, LRU_BLOCK, LRU_BLOCK), f32) * LRU_BLOCK ** -0.5,
        'rg_b': 0.02 * nrm(ks[11], (DEPTH, D_RNN), f32),
        'ig_w': nrm(ks[12], (DEPTH, LRU_BLOCKS, LRU_BLOCK, LRU_BLOCK), f32) * LRU_BLOCK ** -0.5,
        'ig_b': 0.02 * nrm(ks[14], (DEPTH, D_RNN), f32),
        'lru_lambda': jnp.log(u) - jnp.log1p(-u),
        'hgrn_lb': 0.5 * nrm(ks[15], (DEPTH + 1, D_HG_K), f32),
        'hgrn_gnorm': 1.0 + 0.05 * nrm(ks[16], (DEPTH, D_HG_V), f32),
        'w_branch_a': nrm(ks[17], (DEPTH, D_RNN, D_MODEL), f32) * D_RNN ** -0.5,
        'w_branch_b': nrm(ks[18], (DEPTH, D_HG_V, D_MODEL), f32) * D_HG_V ** -0.5,
        'w_out': nrm(ks[19], (DEPTH, D_MODEL, D_MODEL), f32) * D_MODEL ** -0.5,
        'w_up': nrm(ks[20], (DEPTH, D_MODEL, D_FF), f32) * D_MODEL ** -0.5,
        'w_down': nrm(ks[21], (DEPTH, D_FF, D_MODEL), f32) * D_FF ** -0.5,
    }


def reference(x_prompt, x_sample, state_conv, state_rglru, state_hgrn, meta_tokens, norm_gains, w_in,
              conv_w, conv_b, rg_w, rg_b, ig_w, ig_b, lru_lambda, hgrn_lb, hgrn_gnorm,
              w_branch_a, w_branch_b, w_out, w_up, w_down):
    bp = x_prompt.shape[0]
    meta = jnp.broadcast_to(meta_tokens.astype(x_prompt.dtype)[None], (bp, N_META, D_MODEL))
    xp = jnp.concatenate([meta, x_prompt], axis=1)
    xs = x_sample
    lb_all = jnp.cumsum(jax.nn.softmax(hgrn_lb.astype(jnp.float32), axis=0), axis=0)
    conv_p, h_p, S_p, conv_s, h_s, S_s = [], [], [], [], [], []
    for l in range(DEPTH):
        params = (norm_gains[l], w_in[l], conv_w[l], conv_b[l], rg_w[l], rg_b[l], ig_w[l], ig_b[l],
                  lru_lambda[l], hgrn_gnorm[l], w_branch_a[l], w_branch_b[l], w_out[l], w_up[l], w_down[l])
        zc = jnp.zeros((bp, CONV_W - 1, D_RNN), xp.dtype)
        zh = jnp.zeros((bp, D_RNN), jnp.float32)
        zS = jnp.zeros((bp, HG_HEADS, HG_DK, HG_DV), jnp.float32)
        xp, c1, h1, s1 = layer(xp, zc, zh, zS, True, lb_all[l], *params)
        xs, c2, h2, s2 = layer(xs, state_conv[l], state_rglru[l], state_hgrn[l], False, lb_all[l], *params)
        conv_p.append(c1); h_p.append(h1); S_p.append(s1)
        conv_s.append(c2); h_s.append(h2); S_s.append(s2)
    y_prompt = xp[:, N_META:]
    y_sample = xs
    return (y_prompt, y_sample, jnp.stack(conv_p), jnp.stack(h_p), jnp.stack(S_p),
            jnp.stack(conv_s), jnp.stack(h_s), jnp.stack(S_s))
```

```python
import functools

import jax
import jax.numpy as jnp
from jax import lax
from jax.experimental import pallas as pl
from jax.experimental.pallas import tpu as pltpu

D = 1024
N_HEADS = 8
HEAD = 128
N_META = 16
LRU_C = 8.0
EPS = 1e-6
D_FF = 4096
SUBLANES = 8
F32 = jnp.float32
BF16 = jnp.bfloat16

P_GAIN, P_CONV_W, P_CONV_B, P_RG_B, P_IG_B, P_LAMBDA, P_LB, P_GNORM, P_ROWS = 0, 4, 8, 9, 10, 11, 12, 14, 16

C_U, C_GATE, C_Q, C_F, C_I, C_OG, C_MA, C_MB = range(8)

VMEM_LIMIT_BYTES = 56 * 1024 * 1024


def _rms(x, g):
    ms = jnp.mean(x * x, axis=-1, keepdims=True)
    return x * lax.rsqrt(ms + EPS) * g


def _sigmoid(x):
    return 1.0 / (1.0 + jnp.exp(-x))


def _gelu_tanh(x):
    return 0.5 * x * (1.0 + jnp.tanh(0.7978845608028654 * (x + 0.044715 * (x * x * x))))


def _mm(a, b):
    return jnp.dot(a, b, preferred_element_type=F32)


def _mm_nt(a, b):
    return lax.dot_general(a, b, (((1,), (1,)), ((), ())), preferred_element_type=F32)


def _mm_tn(a, b):
    return lax.dot_general(a, b, (((0,), (0,)), ((), ())), preferred_element_type=F32)


def _split3(x):
    hi = x.astype(BF16)
    r1 = x - hi.astype(F32)
    mid = r1.astype(BF16)
    lo = (r1 - mid.astype(F32)).astype(BF16)
    return hi, mid, lo


def _mask_matmul(mask_bf, x):
    hi, mid, lo = _split3(x)
    return _mm(mask_bf, hi) + _mm(mask_bf, mid) + _mm(mask_bf, lo)


def _mixer_kernel(x_ref, cext_ref, h0_ref, s0_ref, pvec_ref, win_ref, wri_ref, wa_ref, wb_ref, wout_ref,
                  x1_ref, cout_ref, hout_ref, sout_ref,
                  ubuf, hbuf, qi_s, ks_s, v_s, ld_s, oin_s,
                  *, seqs, tokens, n_tiles, reset_first, sub):
    j = pl.program_id(1)
    rows = seqs * tokens
    per_seq = seqs > 1

    @pl.when(j == 0)
    def _init():
        ubuf[:, 0:SUBLANES, :] = cext_ref[...]
        hout_ref[...] = h0_ref[...]
        if not per_seq:
            sout_ref[...] = s0_ref[...]

    pv = pvec_ref[...]

    def prow(r):
        return pv[r:r + 1, :]

    def wcol(g):
        return win_ref[:, g * D:(g + 1) * D]

    x = x_ref[...].reshape(rows, D)
    xn = _rms(x, prow(P_GAIN + 0)).astype(BF16)

    row_id = lax.broadcasted_iota(jnp.int32, (rows, 1), 0)
    row_in_group = row_id & (SUBLANES - 1)

    u = _mm(xn, wcol(C_U))
    ubuf[:, SUBLANES:SUBLANES + tokens, :] = u.reshape(seqs, tokens, D)
    uc = prow(P_CONV_B) + prow(P_CONV_W + 3) * u
    for k in range(1, 4):
        shifted = ubuf[:, SUBLANES - k:SUBLANES - k + tokens, :].reshape(rows, D)
        uc = uc + prow(P_CONV_W + 3 - k) * shifted
    tail = ubuf[:, tokens:tokens + SUBLANES, :]
    ubuf[:, 0:SUBLANES, :] = tail

    @pl.when(j == n_tiles - 1)
    def _conv_out():
        cout_ref[...] = tail

    ucb = uc.astype(BF16)
    r_parts, i_parts = [], []
    for b in range(N_HEADS):
        ri = _mm(ucb[:, b * HEAD:(b + 1) * HEAD], wri_ref[b])
        r_parts.append(ri[:, :HEAD])
        i_parts.append(ri[:, HEAD:])
    r_gate = _sigmoid(jnp.concatenate(r_parts, axis=1) + prow(P_RG_B))
    i_gate = _sigmoid(jnp.concatenate(i_parts, axis=1) + prow(P_IG_B))
    lam = prow(P_LAMBDA)
    softplus_neg_lam = jnp.maximum(-lam, 0.0) + jnp.log1p(jnp.exp(-jnp.abs(lam)))
    log_a = (-LRU_C) * r_gate * softplus_neg_lam
    a_cum = jnp.exp(log_a)
    th = jnp.tanh(log_a)
    mult = jnp.sqrt(-2.0 * th / (1.0 - th))
    if reset_first:
        mult = jnp.where(jnp.logical_and(row_id == 0, j == 0), 1.0, mult)
    b_cum = mult * i_gate * uc

    for s in (1, 2, 4):
        keep = row_in_group >= s
        a_prev = jnp.where(keep, pltpu.roll(a_cum, s, 0), 1.0)
        b_prev = jnp.where(keep, pltpu.roll(b_cum, s, 0), 0.0)
        b_cum = a_cum * b_prev + b_cum
        a_cum = a_cum * a_prev

    if per_seq:
        h_in = jnp.broadcast_to(hout_ref[...], (seqs, SUBLANES, D)).reshape(rows, D)
        h_all = a_cum * h_in + b_cum
        hout_ref[...] = h_all.reshape(seqs, SUBLANES, D)[:, SUBLANES - 1:SUBLANES, :]
    else:
        h = hout_ref[0]
        for g in range(rows // SUBLANES):
            sl = slice(g * SUBLANES, (g + 1) * SUBLANES)
            hg = a_cum[sl] * h + b_cum[sl]
            hbuf[sl, :] = hg
            h = hg[SUBLANES - 1:SUBLANES, :]
        hout_ref[0] = h
        h_all = hbuf[...]

    ya = (h_all * _gelu_tanh(_mm(xn, wcol(C_GATE)))).astype(BF16)

    q = _mm(xn, wcol(C_Q))
    qf = q * _sigmoid(q)
    lb_raw = pv[P_LB:P_LB + 2, :]
    lb_exp = jnp.exp(lb_raw - jnp.max(lb_raw, axis=0, keepdims=True))
    lb = lb_exp[0:1, :] / jnp.sum(lb_exp, axis=0, keepdims=True)
    fg = lb + (1.0 - lb) * _sigmoid(_mm(xn, wcol(C_F)))
    logf = jnp.log(fg)
    kk = 1.0 - fg
    v = _mm(xn, wcol(C_I))
    vb = v.astype(BF16)

    ri2 = lax.broadcasted_iota(jnp.int32, (rows, rows), 0)
    ci2 = lax.broadcasted_iota(jnp.int32, (rows, rows), 1)
    causal = ci2 <= ri2
    if per_seq:
        same_seq = (ri2 // tokens) == (ci2 // tokens)
        causal = jnp.logical_and(causal, same_seq)
    cum = _mask_matmul(causal.astype(BF16), logf)
    ones8 = jnp.ones((SUBLANES, HEAD), BF16)
    gnorm = prow(P_GNORM)
    o_heads = []

    if per_seq:
        total = _mask_matmul(same_seq.astype(BF16), logf)
        qi = qf * jnp.exp(cum)
        kd = (kk * jnp.exp(-cum)).astype(BF16)
        d1, d2, d3 = _split3(jnp.exp(total))
        ld = jnp.where(row_in_group == 0, d1.astype(F32),
                       jnp.where(row_in_group == 1, d2.astype(F32),
                                 jnp.where(row_in_group == 2, d3.astype(F32), 0.0)))
        qi_s[...] = qi
        ks_s[...] = kk * jnp.exp(total - cum)
        v_s[...] = v
        ld_s[...] = ld
        qib = qi.astype(BF16)

        def seq_body(g, carry):
            rsl = pl.ds(pl.multiple_of(g * SUBLANES, SUBLANES), SUBLANES)
            for hd in range(N_HEADS):
                hs = slice(hd * HEAD, (hd + 1) * HEAD)
                s_old = s0_ref[g, hd]
                oin_s[rsl, hs] = _mm(qi_s[rsl, hs].astype(BF16), s_old.astype(BF16))
                upd = _mm_tn(ks_s[rsl, hs].astype(BF16), v_s[rsl, hs].astype(BF16))
                decay = _mm_tn(ld_s[rsl, hs].astype(BF16), ones8)
                sout_ref[g, hd] = decay * s_old + upd
            return carry

        lax.fori_loop(0, seqs, seq_body, 0)
        o_inter = oin_s[...]
        for hd in range(N_HEADS):
            hs = slice(hd * HEAD, (hd + 1) * HEAD)
            att = jnp.where(causal, _mm_nt(qib[:, hs], kd[:, hs]), 0.0).astype(BF16)
            o_h = o_inter[:, hs] + _mm(att, vb[:, hs])
            ms = jnp.mean(o_h * o_h, axis=-1, keepdims=True)
            o_heads.append(o_h * lax.rsqrt(ms + EPS))
    else:
        n_sub = rows // sub
        last = cum[rows - 1:rows, :]
        qib = (qf * jnp.exp(cum)).astype(BF16)
        ksb = (kk * jnp.exp(last - cum)).astype(BF16)
        d1, d2, d3 = _split3(jnp.exp(last))
        r8 = lax.broadcasted_iota(jnp.int32, (SUBLANES, 1), 0)
        ld = jnp.where(r8 == 0, d1.astype(F32),
                       jnp.where(r8 == 1, d2.astype(F32),
                                 jnp.where(r8 == 2, d3.astype(F32), 0.0))).astype(BF16)
        for hd in range(N_HEADS):
            hs = slice(hd * HEAD, (hd + 1) * HEAD)
            s_old = sout_ref[0, hd]
            o_inter = _mm(qib[:, hs], s_old.astype(BF16))
            parts = []
            for i in range(n_sub):
                r0 = i * sub
                r1 = r0 + sub
                ref_row = cum[r0 - 1:r0, hs] if i > 0 else jnp.zeros((1, HEAD), F32)
                qd = (qf[r0:r1, hs] * jnp.exp(cum[r0:r1, hs] - ref_row)).astype(BF16)
                kdi = (kk[0:r1, hs] * jnp.exp(ref_row - cum[0:r1, hs])).astype(BF16)
                att = _mm_nt(qd, kdi)
                rr = lax.broadcasted_iota(jnp.int32, (sub, r1), 0) + r0
                cc = lax.broadcasted_iota(jnp.int32, (sub, r1), 1)
                att = jnp.where(cc <= rr, att, 0.0).astype(BF16)
                parts.append(_mm(att, vb[0:r1, hs]))
            o_h = o_inter + (jnp.concatenate(parts, axis=0) if n_sub > 1 else parts[0])
            upd = _mm_tn(ksb[:, hs], vb[:, hs])
            decay = _mm_tn(ld[:, hs], ones8)
            sout_ref[0, hd] = decay * s_old + upd
            ms = jnp.mean(o_h * o_h, axis=-1, keepdims=True)
            o_heads.append(o_h * lax.rsqrt(ms + EPS))

    o_n = jnp.concatenate(o_heads, axis=1) * gnorm
    og = _mm(xn, wcol(C_OG))
    yb = (o_n * (og * _sigmoid(og))).astype(BF16)

    mixed = (_sigmoid(_mm(xn, wcol(C_MA))) * _mm(ya, wa_ref[...])
             + _sigmoid(_mm(xn, wcol(C_MB))) * _mm(yb, wb_ref[...]))
    z = _mm(mixed.astype(BF16), wout_ref[...])
    x1 = x + _rms(z, prow(P_GAIN + 1))
    x1_ref[...] = x1.reshape(seqs, tokens, D)


def _resident(shape):
    return pl.BlockSpec(shape, lambda i, j: (0,) * len(shape), pipeline_mode=pl.Buffered(1))


def _mixer(x, cext, h0, s0, pvec, w_in, w_ri, w_a, w_b, w_out, *, seqs, tokens, reset_first, sub, name):
    n_seq, length, _ = x.shape
    n_tiles = length // tokens
    assert n_seq % seqs == 0 and length % tokens == 0 and tokens % SUBLANES == 0
    assert seqs == 1 or (n_tiles == 1 and tokens == SUBLANES)
    assert (seqs * tokens) % sub == 0
    rows = seqs * tokens
    grid = (n_seq // seqs, n_tiles)
    kern = functools.partial(_mixer_kernel, seqs=seqs, tokens=tokens, n_tiles=n_tiles,
                             reset_first=reset_first, sub=sub)
    per_seq_rows = rows if seqs > 1 else SUBLANES
    return pl.pallas_call(
        kern,
        grid=grid,
        in_specs=[
            pl.BlockSpec((seqs, tokens, D), lambda i, j: (i, j, 0)),
            pl.BlockSpec((seqs, SUBLANES, D), lambda i, j: (i, 0, 0)),
            pl.BlockSpec((seqs, 1, D), lambda i, j: (i, 0, 0)),
            pl.BlockSpec((seqs, N_HEADS, HEAD, HEAD), lambda i, j: (i, 0, 0, 0)),
            _resident((P_ROWS, D)),
            _resident(w_in.shape),
            _resident(w_ri.shape),
            _resident(w_a.shape),
            _resident(w_b.shape),
            _resident(w_out.shape),
        ],
        out_specs=[
            pl.BlockSpec((seqs, tokens, D), lambda i, j: (i, j, 0)),
            pl.BlockSpec((seqs, SUBLANES, D), lambda i, j: (i, 0, 0)),
            pl.BlockSpec((seqs, 1, D), lambda i, j: (i, 0, 0)),
            pl.BlockSpec((seqs, N_HEADS, HEAD, HEAD), lambda i, j: (i, 0, 0, 0)),
        ],
        out_shape=[
            jax.ShapeDtypeStruct((n_seq, length, D), F32),
            jax.ShapeDtypeStruct((n_seq, SUBLANES, D), F32),
            jax.ShapeDtypeStruct((n_seq, 1, D), F32),
            jax.ShapeDtypeStruct((n_seq, N_HEADS, HEAD, HEAD), F32),
        ],
        scratch_shapes=[
            pltpu.VMEM((seqs, SUBLANES + tokens, D), F32),
            pltpu.VMEM((rows, D), F32),
            pltpu.VMEM((per_seq_rows, D), F32),
            pltpu.VMEM((per_seq_rows, D), F32),
            pltpu.VMEM((per_seq_rows, D), F32),
            pltpu.VMEM((per_seq_rows, D), F32),
            pltpu.VMEM((per_seq_rows, D), F32),
        ],
        compiler_params=pltpu.CompilerParams(
            dimension_semantics=("arbitrary", "arbitrary"),
            vmem_limit_bytes=VMEM_LIMIT_BYTES),
        name=name,
    )(x, cext, h0, s0, pvec, w_in, w_ri, w_a, w_b, w_out)


def _mlp_kernel(x_ref, g_ref, wup_ref, wdn_ref, o_ref):
    x = x_ref[...]
    hn = _rms(x, g_ref[0:1, :]).astype(BF16)
    acc = jnp.zeros(x.shape, F32)
    for c in range(D_FF // D):
        t = _mm(hn, wup_ref[:, c * D:(c + 1) * D])
        t = jnp.square(jnp.maximum(t, 0.0)).astype(BF16)
        acc = acc + _mm(t, wdn_ref[c * D:(c + 1) * D, :])
    o_ref[...] = x + _rms(acc, g_ref[1:2, :])


def _mlp(x2d, gains, w_up, w_down, *, tile, name):
    n = x2d.shape[0]
    assert n % tile == 0
    return pl.pallas_call(
        _mlp_kernel,
        grid=(n // tile,),
        in_specs=[
            pl.BlockSpec((tile, D), lambda i: (i, 0)),
            pl.BlockSpec((2, D), lambda i: (0, 0), pipeline_mode=pl.Buffered(1)),
            pl.BlockSpec(w_up.shape, lambda i: (0, 0), pipeline_mode=pl.Buffered(1)),
            pl.BlockSpec(w_down.shape, lambda i: (0, 0), pipeline_mode=pl.Buffered(1)),
        ],
        out_specs=pl.BlockSpec((tile, D), lambda i: (i, 0)),
        out_shape=jax.ShapeDtypeStruct((n, D), F32),
        compiler_params=pltpu.CompilerParams(
            dimension_semantics=("arbitrary",),
            vmem_limit_bytes=VMEM_LIMIT_BYTES),
        name=name,
    )(x2d, gains, w_up, w_down)


def kernel(x_prompt, x_sample, state_conv, state_rglru, state_hgrn, meta_tokens, norm_gains, w_in,
           conv_w, conv_b, rg_w, rg_b, ig_w, ig_b, lru_lambda, hgrn_lb, hgrn_gnorm,
           w_branch_a, w_branch_b, w_out, w_up, w_down):
    bp, seq, _ = x_prompt.shape
    bs, dec_seq, _ = x_sample.shape
    layer = 0
    pvec = jnp.concatenate([
        norm_gains[layer], conv_w[layer], conv_b[layer][None], rg_b[layer][None], ig_b[layer][None],
        lru_lambda[layer][None], hgrn_lb[layer:layer + 2], hgrn_gnorm[layer][None],
        jnp.zeros((1, D), F32)], axis=0).astype(F32)
    w_in_b = w_in[layer].astype(BF16)
    w_ri = jnp.concatenate([rg_w[layer], ig_w[layer]], axis=-1).astype(BF16)
    w_a = w_branch_a[layer].astype(BF16)
    w_b = w_branch_b[layer].astype(BF16)
    w_o = w_out[layer].astype(BF16)
    w_u = w_up[layer].astype(BF16)
    w_d = w_down[layer].astype(BF16)
    weights = (pvec, w_in_b, w_ri, w_a, w_b, w_o)

    zeros_c = jnp.zeros((1, SUBLANES, D), F32)
    zeros_h = jnp.zeros((1, 1, D), F32)
    zeros_s = jnp.zeros((1, N_HEADS, HEAD, HEAD), F32)
    _, c_m, h_m, s_m = _mixer(meta_tokens[None].astype(F32), zeros_c, zeros_h, zeros_s, *weights,
                              seqs=1, tokens=N_META, reset_first=True, sub=N_META, name="mixer_meta")

    x1_p, c_p, h_p, s_p = _mixer(
        x_prompt,
        jnp.broadcast_to(c_m, (bp, SUBLANES, D)),
        jnp.broadcast_to(h_m, (bp, 1, D)),
        jnp.broadcast_to(s_m, (bp, N_HEADS, HEAD, HEAD)),
        *weights, seqs=1, tokens=256, reset_first=False, sub=32, name="mixer_prompt")

    cext_s = jnp.pad(state_conv[layer], ((0, 0), (SUBLANES - 3, 0), (0, 0)))
    x1_s, c_s, h_s, s_s = _mixer(
        x_sample, cext_s, state_rglru[layer][:, None, :], state_hgrn[layer],
        *weights, seqs=8, tokens=dec_seq, reset_first=False, sub=8, name="mixer_sample")

    gains_mlp = norm_gains[layer, 2:4]
    y_p = _mlp(x1_p.reshape(bp * seq, D), gains_mlp, w_u, w_d, tile=512, name="mlp_prompt")
    y_s = _mlp(x1_s.reshape(bs * dec_seq, D), gains_mlp, w_u, w_d, tile=512, name="mlp_sample")

    return (y_p.reshape(bp, seq, D), y_s.reshape(bs, dec_seq, D),
            c_p[:, SUBLANES - 3:, :][None], h_p[:, 0, :][None], s_p[None],
            c_s[:, SUBLANES - 3:, :][None], h_s[:, 0, :][None], s_s[None])
```

```python
import functools

import jax
import jax.numpy as jnp
from jax import lax
from jax.experimental import pallas as pl
from jax.experimental.pallas import tpu as pltpu

D = 1024
N_HEADS = 8
HEAD = 128
N_META = 16
LRU_C = 8.0
EPS = 1e-6
D_FF = 4096
SUBLANES = 8
F32 = jnp.float32
BF16 = jnp.bfloat16

P_GAIN, P_CONV_W, P_CONV_B, P_RG_B, P_IG_B, P_LAMBDA, P_LB, P_GNORM, P_ROWS = 0, 4, 8, 9, 10, 11, 12, 14, 16

C_U, C_GATE, C_Q, C_F, C_I, C_OG, C_MA, C_MB = range(8)

VMEM_LIMIT_BYTES = 56 * 1024 * 1024


def _rms(x, g):
    ms = jnp.mean(x * x, axis=-1, keepdims=True)
    return x * lax.rsqrt(ms + EPS) * g


def _sigmoid(x):
    return 1.0 / (1.0 + jnp.exp(-x))


def _gelu_tanh(x):
    return 0.5 * x * (1.0 + jnp.tanh(0.7978845608028654 * (x + 0.044715 * (x * x * x))))


def _mm(a, b):
    return jnp.dot(a, b, preferred_element_type=F32)


def _mm_nt(a, b):
    return lax.dot_general(a, b, (((1,), (1,)), ((), ())), preferred_element_type=F32)


def _mm_tn(a, b):
    return lax.dot_general(a, b, (((0,), (0,)), ((), ())), preferred_element_type=F32)


def _split3(x):
    hi = x.astype(BF16)
    r1 = x - hi.astype(F32)
    mid = r1.astype(BF16)
    lo = (r1 - mid.astype(F32)).astype(BF16)
    return hi, mid, lo


def _mask_matmul(mask_bf, x):
    hi, mid, lo = _split3(x)
    return _mm(mask_bf, hi) + _mm(mask_bf, mid) + _mm(mask_bf, lo)


def _mixer_kernel(x_ref, cext_ref, h0_ref, s0_ref, pvec_ref, win_ref, wri_ref, wa_ref, wb_ref, wout_ref,
                  x1_ref, cout_ref, hout_ref, sout_ref,
                  ubuf, hbuf, qi_s, ks_s, v_s, ld_s, oin_s,
                  *, seqs, tokens, n_tiles, reset_first, sub):
    j = pl.program_id(1)
    rows = seqs * tokens
    per_seq = seqs > 1

    @pl.when(j == 0)
    def _init():
        ubuf[:, 0:SUBLANES, :] = cext_ref[...]
        hout_ref[...] = h0_ref[...]
        if not per_seq:
            sout_ref[...] = s0_ref[...]

    pv = pvec_ref[...]

    def prow(r):
        return pv[r:r + 1, :]

    def wcol(g):
        return win_ref[:, g * D:(g + 1) * D]

    x = x_ref[...].reshape(rows, D)
    xn = _rms(x, prow(P_GAIN + 0)).astype(BF16)

    row_id = lax.broadcasted_iota(jnp.int32, (rows, 1), 0)
    row_in_group = row_id & (SUBLANES - 1)

    u = _mm(xn, wcol(C_U))
    ubuf[:, SUBLANES:SUBLANES + tokens, :] = u.reshape(seqs, tokens, D)
    uc = prow(P_CONV_B) + prow(P_CONV_W + 3) * u
    for k in range(1, 4):
        shifted = ubuf[:, SUBLANES - k:SUBLANES - k + tokens, :].reshape(rows, D)
        uc = uc + prow(P_CONV_W + 3 - k) * shifted
    tail = ubuf[:, tokens:tokens + SUBLANES, :]
    ubuf[:, 0:SUBLANES, :] = tail

    @pl.when(j == n_tiles - 1)
    def _conv_out():
        cout_ref[...] = tail

    ucb = uc.astype(BF16)
    r_parts, i_parts = [], []
    for b in range(N_HEADS):
        ri = _mm(ucb[:, b * HEAD:(b + 1) * HEAD], wri_ref[b])
        r_parts.append(ri[:, :HEAD])
        i_parts.append(ri[:, HEAD:])
    r_gate = _sigmoid(jnp.concatenate(r_parts, axis=1) + prow(P_RG_B))
    i_gate = _sigmoid(jnp.concatenate(i_parts, axis=1) + prow(P_IG_B))
    lam = prow(P_LAMBDA)
    softplus_neg_lam = jnp.maximum(-lam, 0.0) + jnp.log1p(jnp.exp(-jnp.abs(lam)))
    log_a = (-LRU_C) * r_gate * softplus_neg_lam
    a_cum = jnp.exp(log_a)
    th = jnp.tanh(log_a)
    mult = jnp.sqrt(-2.0 * th / (1.0 - th))
    if reset_first:
        mult = jnp.where(jnp.logical_and(row_id == 0, j == 0), 1.0, mult)
    b_cum = mult * i_gate * uc

    for s in (1, 2, 4):
        keep = row_in_group >= s
        a_prev = jnp.where(keep, pltpu.roll(a_cum, s, 0), 1.0)
        b_prev = jnp.where(keep, pltpu.roll(b_cum, s, 0), 0.0)
        b_cum = a_cum * b_prev + b_cum
        a_cum = a_cum * a_prev

    if per_seq:
        h_in = jnp.broadcast_to(hout_ref[...], (seqs, SUBLANES, D)).reshape(rows, D)
        h_all = a_cum * h_in + b_cum
        hout_ref[...] = h_all.reshape(seqs, SUBLANES, D)[:, SUBLANES - 1:SUBLANES, :]
    else:
        h = hout_ref[0]
        for g in range(rows // SUBLANES):
            sl = slice(g * SUBLANES, (g + 1) * SUBLANES)
            hg = a_cum[sl] * h + b_cum[sl]
            hbuf[sl, :] = hg
            h = hg[SUBLANES - 1:SUBLANES, :]
        hout_ref[0] = h
        h_all = hbuf[...]

    ya = (h_all * _gelu_tanh(_mm(xn, wcol(C_GATE)))).astype(BF16)

    q = _mm(xn, wcol(C_Q))
    qf = q * _sigmoid(q)
    lb_raw = pv[P_LB:P_LB + 2, :]
    lb_exp = jnp.exp(lb_raw - jnp.max(lb_raw, axis=0, keepdims=True))
    lb = lb_exp[0:1, :] / jnp.sum(lb_exp, axis=0, keepdims=True)
    fg = lb + (1.0 - lb) * _sigmoid(_mm(xn, wcol(C_F)))
    logf = jnp.log(fg)
    kk = 1.0 - fg
    v = _mm(xn, wcol(C_I))
    vb = v.astype(BF16)

    ri2 = lax.broadcasted_iota(jnp.int32, (rows, rows), 0)
    ci2 = lax.broadcasted_iota(jnp.int32, (rows, rows), 1)
    causal = ci2 <= ri2
    if per_seq:
        same_seq = (ri2 // tokens) == (ci2 // tokens)
        causal = jnp.logical_and(causal, same_seq)
    cum = _mask_matmul(causal.astype(BF16), logf)
    ones8 = jnp.ones((SUBLANES, HEAD), BF16)
    gnorm = prow(P_GNORM)
    o_heads = []

    if per_seq:
        total = _mask_matmul(same_seq.astype(BF16), logf)
        qi = qf * jnp.exp(cum)
        kd = (kk * jnp.exp(-cum)).astype(BF16)
        d1, d2, d3 = _split3(jnp.exp(total))
        ld = jnp.where(row_in_group == 0, d1.astype(F32),
                       jnp.where(row_in_group == 1, d2.astype(F32),
                                 jnp.where(row_in_group == 2, d3.astype(F32), 0.0)))
        qi_s[...] = qi
        ks_s[...] = kk * jnp.exp(total - cum)
        v_s[...] = v
        ld_s[...] = ld
        qib = qi.astype(BF16)

        def seq_body(g, carry):
            rsl = pl.ds(pl.multiple_of(g * SUBLANES, SUBLANES), SUBLANES)
            for hd in range(N_HEADS):
                hs = slice(hd * HEAD, (hd + 1) * HEAD)
                s_old = s0_ref[g, hd]
                oin_s[rsl, hs] = _mm(qi_s[rsl, hs].astype(BF16), s_old.astype(BF16))
                upd = _mm_tn(ks_s[rsl, hs].astype(BF16), v_s[rsl, hs].astype(BF16))
                decay = _mm_tn(ld_s[rsl, hs].astype(BF16), ones8)
                sout_ref[g, hd] = decay * s_old + upd
            return carry

        lax.fori_loop(0, seqs, seq_body, 0)
        o_inter = oin_s[...]
        for hd in range(N_HEADS):
            hs = slice(hd * HEAD, (hd + 1) * HEAD)
            att = jnp.where(causal, _mm_nt(qib[:, hs], kd[:, hs]), 0.0).astype(BF16)
            o_h = o_inter[:, hs] + _mm(att, vb[:, hs])
            ms = jnp.mean(o_h * o_h, axis=-1, keepdims=True)
            o_heads.append(o_h * lax.rsqrt(ms + EPS))
    else:
        n_sub = rows // sub
        last = cum[rows - 1:rows, :]
        qib = (qf * jnp.exp(cum)).astype(BF16)
        ksb = (kk * jnp.exp(last - cum)).astype(BF16)
        d1, d2, d3 = _split3(jnp.exp(last))
        r8 = lax.broadcasted_iota(jnp.int32, (SUBLANES, 1), 0)
        ld = jnp.where(r8 == 0, d1.astype(F32),
                       jnp.where(r8 == 1, d2.astype(F32),
                                 jnp.where(r8 == 2, d3.astype(F32), 0.0))).astype(BF16)
        for hd in range(N_HEADS):
            hs = slice(hd * HEAD, (hd + 1) * HEAD)
            s_old = sout_ref[0, hd]
            parts = []
            for i in range(n_sub):
                r0 = i * sub
                r1 = r0 + sub
                width = min(rows, -(-r1 // HEAD) * HEAD)
                ref_row = cum[r0 - 1:r0, hs] if i > 0 else jnp.zeros((1, HEAD), F32)
                qd = (qf[r0:r1, hs] * jnp.exp(cum[r0:r1, hs] - ref_row)).astype(BF16)
                arg = ref_row - cum[0:width, hs]
                if width > r1:
                    arg = jnp.where(lax.broadcasted_iota(jnp.int32, (width, 1), 0) < r1, arg, 0.0)
                kdi = (kk[0:width, hs] * jnp.exp(arg)).astype(BF16)
                att = _mm_nt(qd, kdi)
                rr = lax.broadcasted_iota(jnp.int32, (sub, width), 0) + r0
                cc = lax.broadcasted_iota(jnp.int32, (sub, width), 1)
                att = jnp.where(cc <= rr, att, 0.0).astype(BF16)
                if width < rows:
                    att = jnp.concatenate([att, jnp.zeros((sub, rows - width), BF16)], axis=1)
                parts.append(att)
            att_full = jnp.concatenate(parts, axis=0) if n_sub > 1 else parts[0]
            if rows % HEAD == 0:
                o_h = _mm(jnp.concatenate([att_full, qib[:, hs]], axis=1),
                          jnp.concatenate([vb[:, hs], s_old.astype(BF16)], axis=0))
            else:
                o_h = _mm(att_full, vb[:, hs]) + _mm(qib[:, hs], s_old.astype(BF16))
            upd = _mm_tn(ksb[:, hs], vb[:, hs])
            decay = _mm_tn(ld[:, hs], ones8)
            sout_ref[0, hd] = decay * s_old + upd
            ms = jnp.mean(o_h * o_h, axis=-1, keepdims=True)
            o_heads.append(o_h * lax.rsqrt(ms + EPS))

    o_n = jnp.concatenate(o_heads, axis=1) * gnorm
    og = _mm(xn, wcol(C_OG))
    yb = (o_n * (og * _sigmoid(og))).astype(BF16)

    mixed = (_sigmoid(_mm(xn, wcol(C_MA))) * _mm(ya, wa_ref[...])
             + _sigmoid(_mm(xn, wcol(C_MB))) * _mm(yb, wb_ref[...]))
    z = _mm(mixed.astype(BF16), wout_ref[...])
    x1 = x + _rms(z, prow(P_GAIN + 1))
    x1_ref[...] = x1.reshape(seqs, tokens, D)


def _resident(shape):
    return pl.BlockSpec(shape, lambda i, j: (0,) * len(shape), pipeline_mode=pl.Buffered(1))


def _mixer(x, cext, h0, s0, pvec, w_in, w_ri, w_a, w_b, w_out, *, seqs, tokens, reset_first, sub, name):
    n_seq, length, _ = x.shape
    n_tiles = length // tokens
    assert n_seq % seqs == 0 and length % tokens == 0 and tokens % SUBLANES == 0
    assert seqs == 1 or (n_tiles == 1 and tokens == SUBLANES)
    assert (seqs * tokens) % sub == 0
    rows = seqs * tokens
    grid = (n_seq // seqs, n_tiles)
    kern = functools.partial(_mixer_kernel, seqs=seqs, tokens=tokens, n_tiles=n_tiles,
                             reset_first=reset_first, sub=sub)
    per_seq_rows = rows if seqs > 1 else SUBLANES
    return pl.pallas_call(
        kern,
        grid=grid,
        in_specs=[
            pl.BlockSpec((seqs, tokens, D), lambda i, j: (i, j, 0)),
            pl.BlockSpec((seqs, SUBLANES, D), lambda i, j: (i, 0, 0)),
            pl.BlockSpec((seqs, 1, D), lambda i, j: (i, 0, 0)),
            pl.BlockSpec((seqs, N_HEADS, HEAD, HEAD), lambda i, j: (i, 0, 0, 0)),
            _resident((P_ROWS, D)),
            _resident(w_in.shape),
            _resident(w_ri.shape),
            _resident(w_a.shape),
            _resident(w_b.shape),
            _resident(w_out.shape),
        ],
        out_specs=[
            pl.BlockSpec((seqs, tokens, D), lambda i, j: (i, j, 0)),
            pl.BlockSpec((seqs, SUBLANES, D), lambda i, j: (i, 0, 0)),
            pl.BlockSpec((seqs, 1, D), lambda i, j: (i, 0, 0)),
            pl.BlockSpec((seqs, N_HEADS, HEAD, HEAD), lambda i, j: (i, 0, 0, 0)),
        ],
        out_shape=[
            jax.ShapeDtypeStruct((n_seq, length, D), F32),
            jax.ShapeDtypeStruct((n_seq, SUBLANES, D), F32),
            jax.ShapeDtypeStruct((n_seq, 1, D), F32),
            jax.ShapeDtypeStruct((n_seq, N_HEADS, HEAD, HEAD), F32),
        ],
        scratch_shapes=[
            pltpu.VMEM((seqs, SUBLANES + tokens, D), F32),
            pltpu.VMEM((rows, D), F32),
            pltpu.VMEM((per_seq_rows, D), F32),
            pltpu.VMEM((per_seq_rows, D), F32),
            pltpu.VMEM((per_seq_rows, D), F32),
            pltpu.VMEM((per_seq_rows, D), F32),
            pltpu.VMEM((per_seq_rows, D), F32),
        ],
        compiler_params=pltpu.CompilerParams(
            dimension_semantics=("arbitrary", "arbitrary"),
            vmem_limit_bytes=VMEM_LIMIT_BYTES),
        name=name,
    )(x, cext, h0, s0, pvec, w_in, w_ri, w_a, w_b, w_out)


def _mlp_kernel(x_ref, g_ref, wup_ref, wdn_ref, o_ref):
    x = x_ref[...]
    hn = _rms(x, g_ref[0:1, :]).astype(BF16)
    acc = jnp.zeros(x.shape, F32)
    for c in range(D_FF // D):
        t = _mm(hn, wup_ref[:, c * D:(c + 1) * D])
        t = jnp.square(jnp.maximum(t, 0.0)).astype(BF16)
        acc = acc + _mm(t, wdn_ref[c * D:(c + 1) * D, :])
    o_ref[...] = x + _rms(acc, g_ref[1:2, :])


def _mlp(x2d, gains, w_up, w_down, *, tile, name):
    n = x2d.shape[0]
    assert n % tile == 0
    return pl.pallas_call(
        _mlp_kernel,
        grid=(n // tile,),
        in_specs=[
            pl.BlockSpec((tile, D), lambda i: (i, 0)),
            pl.BlockSpec((2, D), lambda i: (0, 0), pipeline_mode=pl.Buffered(1)),
            pl.BlockSpec(w_up.shape, lambda i: (0, 0), pipeline_mode=pl.Buffered(1)),
            pl.BlockSpec(w_down.shape, lambda i: (0, 0), pipeline_mode=pl.Buffered(1)),
        ],
        out_specs=pl.BlockSpec((tile, D), lambda i: (i, 0)),
        out_shape=jax.ShapeDtypeStruct((n, D), F32),
        compiler_params=pltpu.CompilerParams(
            dimension_semantics=("arbitrary",),
            vmem_limit_bytes=VMEM_LIMIT_BYTES),
        name=name,
    )(x2d, gains, w_up, w_down)


def kernel(x_prompt, x_sample, state_conv, state_rglru, state_hgrn, meta_tokens, norm_gains, w_in,
           conv_w, conv_b, rg_w, rg_b, ig_w, ig_b, lru_lambda, hgrn_lb, hgrn_gnorm,
           w_branch_a, w_branch_b, w_out, w_up, w_down):
    bp, seq, _ = x_prompt.shape
    bs, dec_seq, _ = x_sample.shape
    layer = 0
    pvec = jnp.concatenate([
        norm_gains[layer], conv_w[layer], conv_b[layer][None], rg_b[layer][None], ig_b[layer][None],
        lru_lambda[layer][None], hgrn_lb[layer:layer + 2], hgrn_gnorm[layer][None],
        jnp.zeros((1, D), F32)], axis=0).astype(F32)
    w_in_b = w_in[layer].astype(BF16)
    w_ri = jnp.concatenate([rg_w[layer], ig_w[layer]], axis=-1).astype(BF16)
    w_a = w_branch_a[layer].astype(BF16)
    w_b = w_branch_b[layer].astype(BF16)
    w_o = w_out[layer].astype(BF16)
    w_u = w_up[layer].astype(BF16)
    w_d = w_down[layer].astype(BF16)
    weights = (pvec, w_in_b, w_ri, w_a, w_b, w_o)

    zeros_c = jnp.zeros((1, SUBLANES, D), F32)
    zeros_h = jnp.zeros((1, 1, D), F32)
    zeros_s = jnp.zeros((1, N_HEADS, HEAD, HEAD), F32)
    _, c_m, h_m, s_m = _mixer(meta_tokens[None].astype(F32), zeros_c, zeros_h, zeros_s, *weights,
                              seqs=1, tokens=N_META, reset_first=True, sub=N_META, name="mixer_meta")

    x1_p, c_p, h_p, s_p = _mixer(
        x_prompt,
        jnp.broadcast_to(c_m, (bp, SUBLANES, D)),
        jnp.broadcast_to(h_m, (bp, 1, D)),
        jnp.broadcast_to(s_m, (bp, N_HEADS, HEAD, HEAD)),
        *weights, seqs=1, tokens=256, reset_first=False, sub=64, name="mixer_prompt")

    cext_s = jnp.pad(state_conv[layer], ((0, 0), (SUBLANES - 3, 0), (0, 0)))
    x1_s, c_s, h_s, s_s = _mixer(
        x_sample, cext_s, state_rglru[layer][:, None, :], state_hgrn[layer],
        *weights, seqs=8, tokens=dec_seq, reset_first=False, sub=8, name="mixer_sample")

    gains_mlp = norm_gains[layer, 2:4]
    y_p = _mlp(x1_p.reshape(bp * seq, D), gains_mlp, w_u, w_d, tile=512, name="mlp_prompt")
    y_s = _mlp(x1_s.reshape(bs * dec_seq, D), gains_mlp, w_u, w_d, tile=512, name="mlp_sample")

    return (y_p.reshape(bp, seq, D), y_s.reshape(bs, dec_seq, D),
            c_p[:, SUBLANES - 3:, :][None], h_p[:, 0, :][None], s_p[None],
            c_s[:, SUBLANES - 3:, :][None], h_s[:, 0, :][None], s_s[None])
```

```python
import functools

import jax
import jax.numpy as jnp
from jax import lax
from jax.experimental import pallas as pl
from jax.experimental.pallas import tpu as pltpu

D = 1024
N_HEADS = 8
HEAD = 128
N_META = 16
LRU_C = 8.0
EPS = 1e-6
D_FF = 4096
SUBLANES = 8
F32 = jnp.float32
BF16 = jnp.bfloat16

P_GAIN, P_CONV_W, P_CONV_B, P_RG_B, P_IG_B, P_LAMBDA, P_LB, P_GNORM, P_ROWS = 0, 4, 8, 9, 10, 11, 12, 14, 16

C_U, C_GATE, C_Q, C_F, C_I, C_OG, C_MA, C_MB = range(8)

VMEM_LIMIT_BYTES = 56 * 1024 * 1024


def _rms(x, g):
    ms = jnp.mean(x * x, axis=-1, keepdims=True)
    return x * lax.rsqrt(ms + EPS) * g


def _sigmoid(x):
    return 0.5 * jnp.tanh(0.5 * x) + 0.5


_GELU_C1 = 0.7978845608028654
_GELU_C2 = 0.7978845608028654 * 0.044715


def _gelu_tanh(x):
    half = 0.5 * x
    return half + half * jnp.tanh(x * (_GELU_C1 + _GELU_C2 * (x * x)))


def _mm(a, b):
    return jnp.dot(a, b, preferred_element_type=F32)


def _mm_nt(a, b):
    return lax.dot_general(a, b, (((1,), (1,)), ((), ())), preferred_element_type=F32)


def _mm_tn(a, b):
    return lax.dot_general(a, b, (((0,), (0,)), ((), ())), preferred_element_type=F32)


def _split3(x):
    hi = x.astype(BF16)
    r1 = x - hi.astype(F32)
    mid = r1.astype(BF16)
    lo = (r1 - mid.astype(F32)).astype(BF16)
    return hi, mid, lo


def _mask_matmul(mask_bf, x):
    hi, mid, lo = _split3(x)
    return _mm(mask_bf, hi) + _mm(mask_bf, mid) + _mm(mask_bf, lo)


def _pack_rows(w):
    wb = w.astype(BF16)
    pairs = wb.reshape(wb.shape[:-2] + (wb.shape[-2] // 2, 2, wb.shape[-1]))
    return lax.bitcast_convert_type(jnp.swapaxes(pairs, -1, -2), jnp.int32)


def _unpack_rows(words):
    return pltpu.bitcast(words, BF16)


def _column_of(row):
    return jnp.transpose(jnp.broadcast_to(row, (HEAD, HEAD)))


def _mixer_kernel(x_ref, cext_ref, h0_ref, s0_ref, pvec_ref, win_ref, wri_ref, wa_ref, wb_ref, wout_ref,
                  x1_ref, cout_ref, hout_ref, sout_ref,
                  ubuf, hbuf, ga_s, sgb_s, sog_s, qf_s, kk_s, cum_s, v_s, tot_s, oin_s,
                  *, seqs, tokens, n_tiles, reset_first, sub):
    j = pl.program_id(1)
    rows = seqs * tokens
    per_seq = seqs > 1

    @pl.when(j == 0)
    def _init():
        ubuf[:, 0:SUBLANES, :] = cext_ref[...]
        hout_ref[...] = h0_ref[...]
        if not per_seq:
            sout_ref[...] = s0_ref[...]

    pv = pvec_ref[...]

    def prow(r):
        return pv[r:r + 1, :]

    def wcol(g):
        return _unpack_rows(win_ref[:, g * D:(g + 1) * D])

    x = x_ref[...].reshape(rows, D)
    xn = _rms(x, prow(P_GAIN + 0)).astype(BF16)

    row_id = lax.broadcasted_iota(jnp.int32, (rows, 1), 0)
    row_in_group = row_id & (SUBLANES - 1)

    u = _mm(xn, wcol(C_U))
    ubuf[:, SUBLANES:SUBLANES + tokens, :] = u.reshape(seqs, tokens, D)
    uc = prow(P_CONV_B) + prow(P_CONV_W + 3) * u
    for k in range(1, 4):
        shifted = ubuf[:, SUBLANES - k:SUBLANES - k + tokens, :].reshape(rows, D)
        uc = uc + prow(P_CONV_W + 3 - k) * shifted
    tail = ubuf[:, tokens:tokens + SUBLANES, :]
    ubuf[:, 0:SUBLANES, :] = tail

    @pl.when(j == n_tiles - 1)
    def _conv_out():
        cout_ref[...] = tail

    ucb = uc.astype(BF16)
    r_parts, i_parts = [], []
    for b in range(N_HEADS):
        ri = _mm(ucb[:, b * HEAD:(b + 1) * HEAD], _unpack_rows(wri_ref[b]))
        r_parts.append(ri[:, :HEAD])
        i_parts.append(ri[:, HEAD:])
    r_gate = _sigmoid(jnp.concatenate(r_parts, axis=1) + prow(P_RG_B))
    i_gate = _sigmoid(jnp.concatenate(i_parts, axis=1) + prow(P_IG_B))
    lam = prow(P_LAMBDA)
    softplus_neg_lam = jnp.maximum(-lam, 0.0) + jnp.log1p(jnp.exp(-jnp.abs(lam)))
    log_a = (-LRU_C) * r_gate * softplus_neg_lam
    a_cum = jnp.exp(log_a)
    th = jnp.tanh(log_a)
    mult = jnp.sqrt(-2.0 * th / (1.0 - th))
    if reset_first:
        mult = jnp.where(jnp.logical_and(row_id == 0, j == 0), 1.0, mult)
    b_cum = mult * i_gate * uc

    a_cum = a_cum.reshape(rows // SUBLANES, SUBLANES, D)
    b_cum = b_cum.reshape(rows // SUBLANES, SUBLANES, D)
    sublane = lax.broadcasted_iota(jnp.int32, (1, SUBLANES, 1), 1)
    for s in (1, 2, 4):
        keep = sublane >= s
        a_prev = jnp.where(keep, pltpu.roll(a_cum, s, 1), 1.0)
        b_prev = jnp.where(keep, pltpu.roll(b_cum, s, 1), 0.0)
        b_cum = a_cum * b_prev + b_cum
        a_cum = a_cum * a_prev
    a_cum = a_cum.reshape(rows, D)
    b_cum = b_cum.reshape(rows, D)

    if per_seq:
        h_in = jnp.broadcast_to(hout_ref[...], (seqs, SUBLANES, D)).reshape(rows, D)
        h_all = a_cum * h_in + b_cum
        hout_ref[...] = h_all.reshape(seqs, SUBLANES, D)[:, SUBLANES - 1:SUBLANES, :]
    else:
        h = hout_ref[0]
        for g in range(rows // SUBLANES):
            sl = slice(g * SUBLANES, (g + 1) * SUBLANES)
            hg = a_cum[sl] * h + b_cum[sl]
            hbuf[sl, :] = hg
            h = hg[SUBLANES - 1:SUBLANES, :]
        hout_ref[0] = h
        h_all = hbuf[...]

    ya = (h_all * _gelu_tanh(_mm(xn, wcol(C_GATE)))).astype(BF16)
    ga_s[...] = _sigmoid(_mm(xn, wcol(C_MA))) * _mm(ya, _unpack_rows(wa_ref[...]))
    sgb_s[...] = _sigmoid(_mm(xn, wcol(C_MB)))
    og = _mm(xn, wcol(C_OG))
    sog_s[...] = og * _sigmoid(og)

    q = _mm(xn, wcol(C_Q))
    qf_s[...] = q * _sigmoid(q)
    lb_raw = pv[P_LB:P_LB + 2, :]
    lb_exp = jnp.exp(lb_raw - jnp.max(lb_raw, axis=0, keepdims=True))
    lb = lb_exp[0:1, :] / jnp.sum(lb_exp, axis=0, keepdims=True)
    fg = lb + (1.0 - lb) * _sigmoid(_mm(xn, wcol(C_F)))
    logf = jnp.log(fg)
    kk_s[...] = 1.0 - fg
    v_s[...] = _mm(xn, wcol(C_I))

    ri2 = lax.broadcasted_iota(jnp.int32, (rows, rows), 0)
    ci2 = lax.broadcasted_iota(jnp.int32, (rows, rows), 1)
    if per_seq:
        same_seq = (ri2 // tokens) == (ci2 // tokens)
        tri = jnp.logical_and(ci2 <= ri2, same_seq)
        tot_s[...] = _mask_matmul(same_seq.astype(BF16), logf)
    else:
        tri = ci2 <= ri2
    cum_s[...] = _mask_matmul(tri.astype(BF16), logf)

    @pl.when(j >= 0)
    def _second_region():
        qf = qf_s[...]
        kk = kk_s[...]
        cum = cum_s[...]
        v = v_s[...]
        vb = v.astype(BF16)
        o_heads = []

        if per_seq:
            total = tot_s[...]
            qi = qf * jnp.exp(cum)
            kd = (kk * jnp.exp(-cum)).astype(BF16)
            qf_s[...] = qi
            kk_s[...] = kk * jnp.exp(total - cum)
            qib = qi.astype(BF16)
            ri = lax.broadcasted_iota(jnp.int32, (rows, rows), 0)
            ci = lax.broadcasted_iota(jnp.int32, (rows, rows), 1)
            causal = jnp.logical_and(ci <= ri, (ri // tokens) == (ci // tokens))

            def seq_body(g, carry):
                rsl = pl.ds(pl.multiple_of(g * SUBLANES, SUBLANES), SUBLANES)
                for hd in range(N_HEADS):
                    hs = slice(hd * HEAD, (hd + 1) * HEAD)
                    s_old = s0_ref[g, hd]
                    oin_s[rsl, hs] = _mm(qf_s[rsl, hs].astype(BF16), s_old.astype(BF16))
                    upd = _mm_tn(kk_s[rsl, hs].astype(BF16), v_s[rsl, hs].astype(BF16))
                    decay = _column_of(jnp.exp(tot_s[pl.ds(g * SUBLANES, 1), hs]))
                    sout_ref[g, hd] = decay * s_old + upd
                return carry

            lax.fori_loop(0, seqs, seq_body, 0)
            o_inter = oin_s[...]
            for hd in range(N_HEADS):
                hs = slice(hd * HEAD, (hd + 1) * HEAD)
                att = jnp.where(causal, _mm_nt(qib[:, hs], kd[:, hs]), 0.0).astype(BF16)
                o_h = o_inter[:, hs] + _mm(att, vb[:, hs])
                ms = jnp.mean(o_h * o_h, axis=-1, keepdims=True)
                o_heads.append(o_h * lax.rsqrt(ms + EPS))
        else:
            n_sub = rows // sub
            last = cum[rows - 1:rows, :]
            qib = (qf * jnp.exp(cum)).astype(BF16)
            ksb = (kk * jnp.exp(last - cum)).astype(BF16)
            decay_row = jnp.exp(last)
            for hd in range(N_HEADS):
                hs = slice(hd * HEAD, (hd + 1) * HEAD)
                s_old = sout_ref[0, hd]
                parts = []
                for i in range(n_sub):
                    r0 = i * sub
                    r1 = r0 + sub
                    width = min(rows, -(-r1 // HEAD) * HEAD)
                    ref_row = cum[r0 - 1:r0, hs] if i > 0 else jnp.zeros((1, HEAD), F32)
                    qd = (qf[r0:r1, hs] * jnp.exp(cum[r0:r1, hs] - ref_row)).astype(BF16)
                    arg = ref_row - cum[0:width, hs]
                    if width > r1:
                        arg = jnp.where(lax.broadcasted_iota(jnp.int32, (width, 1), 0) < r1, arg, 0.0)
                    kdi = (kk[0:width, hs] * jnp.exp(arg)).astype(BF16)
                    att = _mm_nt(qd, kdi)
                    rr = lax.broadcasted_iota(jnp.int32, (sub, width), 0) + r0
                    cc = lax.broadcasted_iota(jnp.int32, (sub, width), 1)
                    att = jnp.where(cc <= rr, att, 0.0).astype(BF16)
                    if width < rows:
                        att = jnp.concatenate([att, jnp.zeros((sub, rows - width), BF16)], axis=1)
                    parts.append(att)
                att_full = jnp.concatenate(parts, axis=0) if n_sub > 1 else parts[0]
                if rows % HEAD == 0:
                    o_h = _mm(jnp.concatenate([att_full, qib[:, hs]], axis=1),
                              jnp.concatenate([vb[:, hs], s_old.astype(BF16)], axis=0))
                else:
                    o_h = _mm(att_full, vb[:, hs]) + _mm(qib[:, hs], s_old.astype(BF16))
                upd = _mm_tn(ksb[:, hs], vb[:, hs])
                sout_ref[0, hd] = _column_of(decay_row[:, hs]) * s_old + upd
                ms = jnp.mean(o_h * o_h, axis=-1, keepdims=True)
                o_heads.append(o_h * lax.rsqrt(ms + EPS))

        o_n = jnp.concatenate(o_heads, axis=1) * prow(P_GNORM)
        yb = (o_n * sog_s[...]).astype(BF16)

        mixed = ga_s[...] + sgb_s[...] * _mm(yb, _unpack_rows(wb_ref[...]))
        z = _mm(mixed.astype(BF16), _unpack_rows(wout_ref[...]))
        x1 = x_ref[...].reshape(rows, D) + _rms(z, prow(P_GAIN + 1))
        x1_ref[...] = x1.reshape(seqs, tokens, D)


def _resident(shape):
    return pl.BlockSpec(shape, lambda i, j: (0,) * len(shape), pipeline_mode=pl.Buffered(1))


def _mixer(x, cext, h0, s0, pvec, w_in, w_ri, w_a, w_b, w_out, *, seqs, tokens, reset_first, sub, name):
    n_seq, length, _ = x.shape
    n_tiles = length // tokens
    assert n_seq % seqs == 0 and length % tokens == 0 and tokens % SUBLANES == 0
    assert seqs == 1 or (n_tiles == 1 and tokens == SUBLANES)
    assert (seqs * tokens) % sub == 0
    rows = seqs * tokens
    grid = (n_seq // seqs, n_tiles)
    kern = functools.partial(_mixer_kernel, seqs=seqs, tokens=tokens, n_tiles=n_tiles,
                             reset_first=reset_first, sub=sub)
    per_seq_rows = rows if seqs > 1 else SUBLANES
    tile_f32 = pltpu.VMEM((rows, D), F32)
    return pl.pallas_call(
        kern,
        grid=grid,
        in_specs=[
            pl.BlockSpec((seqs, tokens, D), lambda i, j: (i, j, 0)),
            pl.BlockSpec((seqs, SUBLANES, D), lambda i, j: (i, 0, 0)),
            pl.BlockSpec((seqs, 1, D), lambda i, j: (i, 0, 0)),
            pl.BlockSpec((seqs, N_HEADS, HEAD, HEAD), lambda i, j: (i, 0, 0, 0)),
            _resident((P_ROWS, D)),
            _resident(w_in.shape),
            _resident(w_ri.shape),
            _resident(w_a.shape),
            _resident(w_b.shape),
            _resident(w_out.shape),
        ],
        out_specs=[
            pl.BlockSpec((seqs, tokens, D), lambda i, j: (i, j, 0)),
            pl.BlockSpec((seqs, SUBLANES, D), lambda i, j: (i, 0, 0)),
            pl.BlockSpec((seqs, 1, D), lambda i, j: (i, 0, 0)),
            pl.BlockSpec((seqs, N_HEADS, HEAD, HEAD), lambda i, j: (i, 0, 0, 0)),
        ],
        out_shape=[
            jax.ShapeDtypeStruct((n_seq, length, D), F32),
            jax.ShapeDtypeStruct((n_seq, SUBLANES, D), F32),
            jax.ShapeDtypeStruct((n_seq, 1, D), F32),
            jax.ShapeDtypeStruct((n_seq, N_HEADS, HEAD, HEAD), F32),
        ],
        scratch_shapes=[
            pltpu.VMEM((seqs, SUBLANES + tokens, D), F32),
            tile_f32,
            tile_f32, tile_f32, tile_f32,
            tile_f32, tile_f32, tile_f32, tile_f32,
            pltpu.VMEM((per_seq_rows, D), F32),
            pltpu.VMEM((per_seq_rows, D), F32),
        ],
        compiler_params=pltpu.CompilerParams(
            dimension_semantics=("arbitrary", "arbitrary"),
            vmem_limit_bytes=VMEM_LIMIT_BYTES),
        name=name,
    )(x, cext, h0, s0, pvec, w_in, w_ri, w_a, w_b, w_out)


def _mlp_kernel(x_ref, g_ref, wup_ref, wdn_ref, o_ref):
    x = x_ref[...]
    hn = _rms(x, g_ref[0:1, :]).astype(BF16)
    acc = jnp.zeros(x.shape, F32)
    for c in range(D_FF // D):
        t = _mm(hn, _unpack_rows(wup_ref[:, c * D:(c + 1) * D]))
        t = jnp.square(jnp.maximum(t, 0.0)).astype(BF16)
        acc = acc + _mm(t, _unpack_rows(wdn_ref[c * (D // 2):(c + 1) * (D // 2), :]))
    o_ref[...] = x + _rms(acc, g_ref[1:2, :])


def _mlp(x2d, gains, w_up, w_down, *, tile, name):
    n = x2d.shape[0]
    assert n % tile == 0
    return pl.pallas_call(
        _mlp_kernel,
        grid=(n // tile,),
        in_specs=[
            pl.BlockSpec((tile, D), lambda i: (i, 0)),
            pl.BlockSpec((2, D), lambda i: (0, 0), pipeline_mode=pl.Buffered(1)),
            pl.BlockSpec(w_up.shape, lambda i: (0, 0), pipeline_mode=pl.Buffered(1)),
            pl.BlockSpec(w_down.shape, lambda i: (0, 0), pipeline_mode=pl.Buffered(1)),
        ],
        out_specs=pl.BlockSpec((tile, D), lambda i: (i, 0)),
        out_shape=jax.ShapeDtypeStruct((n, D), F32),
        compiler_params=pltpu.CompilerParams(
            dimension_semantics=("arbitrary",),
            vmem_limit_bytes=VMEM_LIMIT_BYTES),
        name=name,
    )(x2d, gains, w_up, w_down)


def kernel(x_prompt, x_sample, state_conv, state_rglru, state_hgrn, meta_tokens, norm_gains, w_in,
           conv_w, conv_b, rg_w, rg_b, ig_w, ig_b, lru_lambda, hgrn_lb, hgrn_gnorm,
           w_branch_a, w_branch_b, w_out, w_up, w_down):
    bp, seq, _ = x_prompt.shape
    bs, dec_seq, _ = x_sample.shape
    layer = 0
    pvec = jnp.concatenate([
        norm_gains[layer], conv_w[layer], conv_b[layer][None], rg_b[layer][None], ig_b[layer][None],
        lru_lambda[layer][None], hgrn_lb[layer:layer + 2], hgrn_gnorm[layer][None],
        jnp.zeros((1, D), F32)], axis=0).astype(F32)
    w_in_b = _pack_rows(w_in[layer])
    w_ri = _pack_rows(jnp.concatenate([rg_w[layer], ig_w[layer]], axis=-1))
    w_a = _pack_rows(w_branch_a[layer])
    w_b = _pack_rows(w_branch_b[layer])
    w_o = _pack_rows(w_out[layer])
    w_u = _pack_rows(w_up[layer])
    w_d = _pack_rows(w_down[layer])
    weights = (pvec, w_in_b, w_ri, w_a, w_b, w_o)

    zeros_c = jnp.zeros((1, SUBLANES, D), F32)
    zeros_h = jnp.zeros((1, 1, D), F32)
    zeros_s = jnp.zeros((1, N_HEADS, HEAD, HEAD), F32)
    _, c_m, h_m, s_m = _mixer(meta_tokens[None].astype(F32), zeros_c, zeros_h, zeros_s, *weights,
                              seqs=1, tokens=N_META, reset_first=True, sub=N_META, name="mixer_meta")

    x1_p, c_p, h_p, s_p = _mixer(
        x_prompt,
        jnp.broadcast_to(c_m, (bp, SUBLANES, D)),
        jnp.broadcast_to(h_m, (bp, 1, D)),
        jnp.broadcast_to(s_m, (bp, N_HEADS, HEAD, HEAD)),
        *weights, seqs=1, tokens=256, reset_first=False, sub=64, name="mixer_prompt")

    cext_s = jnp.pad(state_conv[layer], ((0, 0), (SUBLANES - 3, 0), (0, 0)))
    x1_s, c_s, h_s, s_s = _mixer(
        x_sample, cext_s, state_rglru[layer][:, None, :], state_hgrn[layer],
        *weights, seqs=8, tokens=dec_seq, reset_first=False, sub=8, name="mixer_sample")

    gains_mlp = norm_gains[layer, 2:4]
    y_p = _mlp(x1_p.reshape(bp * seq, D), gains_mlp, w_u, w_d, tile=512, name="mlp_prompt")
    y_s = _mlp(x1_s.reshape(bs * dec_seq, D), gains_mlp, w_u, w_d, tile=512, name="mlp_sample")

    return (y_p.reshape(bp, seq, D), y_s.reshape(bs, dec_seq, D),
            c_p[:, SUBLANES - 3:, :][None], h_p[:, 0, :][None], s_p[None],
            c_s[:, SUBLANES - 3:, :][None], h_s[:, 0, :][None], s_s[None])
```

```python
import functools

import jax
import jax.numpy as jnp
from jax import lax
from jax.experimental import pallas as pl
from jax.experimental.pallas import tpu as pltpu

D = 1024
N_HEADS = 8
HEAD = 128
N_META = 16
LRU_C = 8.0
EPS = 1e-6
D_FF = 4096
SUBLANES = 8
F32 = jnp.float32
BF16 = jnp.bfloat16

P_GAIN, P_CONV_W, P_CONV_B, P_RG_B, P_IG_B, P_LAMBDA, P_LB, P_GNORM, P_ROWS = 0, 4, 8, 9, 10, 11, 12, 14, 16

C_U, C_GATE, C_Q, C_F, C_I, C_OG, C_MA, C_MB = range(8)

VMEM_LIMIT_BYTES = 56 * 1024 * 1024


def _rms(x, g):
    ms = jnp.mean(x * x, axis=-1, keepdims=True)
    return x * lax.rsqrt(ms + EPS) * g


def _sigmoid(x):
    return 0.5 * jnp.tanh(0.5 * x) + 0.5


_GELU_C1 = 0.7978845608028654
_GELU_C2 = 0.7978845608028654 * 0.044715


def _gelu_tanh(x):
    half = 0.5 * x
    return half + half * jnp.tanh(x * (_GELU_C1 + _GELU_C2 * (x * x)))


def _mm(a, b):
    return jnp.dot(a, b, preferred_element_type=F32)


def _mm_nt(a, b):
    return lax.dot_general(a, b, (((1,), (1,)), ((), ())), preferred_element_type=F32)


def _mm_tn(a, b):
    return lax.dot_general(a, b, (((0,), (0,)), ((), ())), preferred_element_type=F32)


def _split3(x):
    hi = x.astype(BF16)
    r1 = x - hi.astype(F32)
    mid = r1.astype(BF16)
    lo = (r1 - mid.astype(F32)).astype(BF16)
    return hi, mid, lo


def _mask_matmul(mask_bf, x):
    hi, mid, lo = _split3(x)
    return _mm(mask_bf, hi) + _mm(mask_bf, mid) + _mm(mask_bf, lo)


def _pack_rows(w):
    bits = lax.bitcast_convert_type(w.astype(BF16), jnp.uint16).astype(jnp.uint32)
    words = bits[..., 0::2, :] | (bits[..., 1::2, :] << 16)
    return lax.bitcast_convert_type(words, jnp.int32)


def _unpack_rows(words):
    return pltpu.bitcast(words, BF16)


def _column_of(row):
    return jnp.transpose(jnp.broadcast_to(row, (HEAD, HEAD)))


def _mixer_kernel(x_ref, cext_ref, h0_ref, s0_ref, pvec_ref, win_ref, wri_ref, wa_ref, wb_ref, wout_ref,
                  x1_ref, cout_ref, hout_ref, sout_ref,
                  ubuf, hbuf, ga_s, sgb_s, sog_s, qf_s, kk_s, cum_s, v_s, tot_s, oin_s,
                  *, seqs, tokens, n_tiles, reset_first, sub):
    j = pl.program_id(1)
    rows = seqs * tokens
    per_seq = seqs > 1

    @pl.when(j == 0)
    def _init():
        ubuf[:, 0:SUBLANES, :] = cext_ref[...]
        hout_ref[...] = h0_ref[...]
        if not per_seq:
            sout_ref[...] = s0_ref[...]

    pv = pvec_ref[...]

    def prow(r):
        return pv[r:r + 1, :]

    def wcol(g):
        return _unpack_rows(win_ref[:, g * D:(g + 1) * D])

    x = x_ref[...].reshape(rows, D)
    xn = _rms(x, prow(P_GAIN + 0)).astype(BF16)

    row_id = lax.broadcasted_iota(jnp.int32, (rows, 1), 0)
    row_in_group = row_id & (SUBLANES - 1)

    u = _mm(xn, wcol(C_U))
    ubuf[:, SUBLANES:SUBLANES + tokens, :] = u.reshape(seqs, tokens, D)
    uc = prow(P_CONV_B) + prow(P_CONV_W + 3) * u
    for k in range(1, 4):
        shifted = ubuf[:, SUBLANES - k:SUBLANES - k + tokens, :].reshape(rows, D)
        uc = uc + prow(P_CONV_W + 3 - k) * shifted
    tail = ubuf[:, tokens:tokens + SUBLANES, :]
    ubuf[:, 0:SUBLANES, :] = tail

    @pl.when(j == n_tiles - 1)
    def _conv_out():
        cout_ref[...] = tail

    ucb = uc.astype(BF16)
    r_parts, i_parts = [], []
    for b in range(N_HEADS):
        ri = _mm(ucb[:, b * HEAD:(b + 1) * HEAD], _unpack_rows(wri_ref[b]))
        r_parts.append(ri[:, :HEAD])
        i_parts.append(ri[:, HEAD:])
    r_gate = _sigmoid(jnp.concatenate(r_parts, axis=1) + prow(P_RG_B))
    i_gate = _sigmoid(jnp.concatenate(i_parts, axis=1) + prow(P_IG_B))
    lam = prow(P_LAMBDA)
    softplus_neg_lam = jnp.maximum(-lam, 0.0) + jnp.log1p(jnp.exp(-jnp.abs(lam)))
    log_a = (-LRU_C) * r_gate * softplus_neg_lam
    a_cum = jnp.exp(log_a)
    th = jnp.tanh(log_a)
    mult = jnp.sqrt(-2.0 * th / (1.0 - th))
    if reset_first:
        mult = jnp.where(jnp.logical_and(row_id == 0, j == 0), 1.0, mult)
    b_cum = mult * i_gate * uc

    a_cum = a_cum.reshape(rows // SUBLANES, SUBLANES, D)
    b_cum = b_cum.reshape(rows // SUBLANES, SUBLANES, D)
    sublane = lax.broadcasted_iota(jnp.int32, (1, SUBLANES, 1), 1)
    for s in (1, 2, 4):
        keep = sublane >= s
        a_prev = jnp.where(keep, pltpu.roll(a_cum, s, 1), 1.0)
        b_prev = jnp.where(keep, pltpu.roll(b_cum, s, 1), 0.0)
        b_cum = a_cum * b_prev + b_cum
        a_cum = a_cum * a_prev
    a_cum = a_cum.reshape(rows, D)
    b_cum = b_cum.reshape(rows, D)

    if per_seq:
        h_in = jnp.broadcast_to(hout_ref[...], (seqs, SUBLANES, D)).reshape(rows, D)
        h_all = a_cum * h_in + b_cum
        hout_ref[...] = h_all.reshape(seqs, SUBLANES, D)[:, SUBLANES - 1:SUBLANES, :]
    else:
        h = hout_ref[0]
        for g in range(rows // SUBLANES):
            sl = slice(g * SUBLANES, (g + 1) * SUBLANES)
            hg = a_cum[sl] * h + b_cum[sl]
            hbuf[sl, :] = hg
            h = hg[SUBLANES - 1:SUBLANES, :]
        hout_ref[0] = h
        h_all = hbuf[...]

    ya = (h_all * _gelu_tanh(_mm(xn, wcol(C_GATE)))).astype(BF16)
    ga_s[...] = _sigmoid(_mm(xn, wcol(C_MA))) * _mm(ya, _unpack_rows(wa_ref[...]))
    sgb_s[...] = _sigmoid(_mm(xn, wcol(C_MB)))
    og = _mm(xn, wcol(C_OG))
    sog_s[...] = og * _sigmoid(og)

    q = _mm(xn, wcol(C_Q))
    qf_s[...] = q * _sigmoid(q)
    lb_raw = pv[P_LB:P_LB + 2, :]
    lb_exp = jnp.exp(lb_raw - jnp.max(lb_raw, axis=0, keepdims=True))
    lb = lb_exp[0:1, :] / jnp.sum(lb_exp, axis=0, keepdims=True)
    fg = lb + (1.0 - lb) * _sigmoid(_mm(xn, wcol(C_F)))
    logf = jnp.log(fg)
    kk_s[...] = 1.0 - fg
    v_s[...] = _mm(xn, wcol(C_I))

    ri2 = lax.broadcasted_iota(jnp.int32, (rows, rows), 0)
    ci2 = lax.broadcasted_iota(jnp.int32, (rows, rows), 1)
    if per_seq:
        same_seq = (ri2 // tokens) == (ci2 // tokens)
        tri = jnp.logical_and(ci2 <= ri2, same_seq)
        tot_s[...] = _mask_matmul(same_seq.astype(BF16), logf)
    else:
        tri = ci2 <= ri2
    cum_s[...] = _mask_matmul(tri.astype(BF16), logf)

    @pl.when(j >= 0)
    def _second_region():
        qf = qf_s[...]
        kk = kk_s[...]
        cum = cum_s[...]
        v = v_s[...]
        vb = v.astype(BF16)
        o_heads = []

        if per_seq:
            total = tot_s[...]
            qi = qf * jnp.exp(cum)
            kd = (kk * jnp.exp(-cum)).astype(BF16)
            qf_s[...] = qi
            kk_s[...] = kk * jnp.exp(total - cum)
            qib = qi.astype(BF16)
            ri = lax.broadcasted_iota(jnp.int32, (rows, rows), 0)
            ci = lax.broadcasted_iota(jnp.int32, (rows, rows), 1)
            causal = jnp.logical_and(ci <= ri, (ri // tokens) == (ci // tokens))

            def seq_body(g, carry):
                rsl = pl.ds(pl.multiple_of(g * SUBLANES, SUBLANES), SUBLANES)
                for hd in range(N_HEADS):
                    hs = slice(hd * HEAD, (hd + 1) * HEAD)
                    s_old = s0_ref[g, hd]
                    oin_s[rsl, hs] = _mm(qf_s[rsl, hs].astype(BF16), s_old.astype(BF16))
                    upd = _mm_tn(kk_s[rsl, hs].astype(BF16), v_s[rsl, hs].astype(BF16))
                    decay = _column_of(jnp.exp(tot_s[pl.ds(g * SUBLANES, 1), hs]))
                    sout_ref[g, hd] = decay * s_old + upd
                return carry

            lax.fori_loop(0, seqs, seq_body, 0)
            o_inter = oin_s[...]
            for hd in range(N_HEADS):
                hs = slice(hd * HEAD, (hd + 1) * HEAD)
                att = jnp.where(causal, _mm_nt(qib[:, hs], kd[:, hs]), 0.0).astype(BF16)
                o_h = o_inter[:, hs] + _mm(att, vb[:, hs])
                ms = jnp.mean(o_h * o_h, axis=-1, keepdims=True)
                o_heads.append(o_h * lax.rsqrt(ms + EPS))
        else:
            n_sub = rows // sub
            last = cum[rows - 1:rows, :]
            qib = (qf * jnp.exp(cum)).astype(BF16)
            ksb = (kk * jnp.exp(last - cum)).astype(BF16)
            decay_row = jnp.exp(last)
            for hd in range(N_HEADS):
                hs = slice(hd * HEAD, (hd + 1) * HEAD)
                s_old = sout_ref[0, hd]
                parts = []
                for i in range(n_sub):
                    r0 = i * sub
                    r1 = r0 + sub
                    width = min(rows, -(-r1 // HEAD) * HEAD)
                    ref_row = cum[r0 - 1:r0, hs] if i > 0 else jnp.zeros((1, HEAD), F32)
                    qd = (qf[r0:r1, hs] * jnp.exp(cum[r0:r1, hs] - ref_row)).astype(BF16)
                    arg = ref_row - cum[0:width, hs]
                    if width > r1:
                        arg = jnp.where(lax.broadcasted_iota(jnp.int32, (width, 1), 0) < r1, arg, 0.0)
                    kdi = (kk[0:width, hs] * jnp.exp(arg)).astype(BF16)
                    att = _mm_nt(qd, kdi)
                    rr = lax.broadcasted_iota(jnp.int32, (sub, width), 0) + r0
                    cc = lax.broadcasted_iota(jnp.int32, (sub, width), 1)
                    att = jnp.where(cc <= rr, att, 0.0).astype(BF16)
                    if width < rows:
                        att = jnp.concatenate([att, jnp.zeros((sub, rows - width), BF16)], axis=1)
                    parts.append(att)
                att_full = jnp.concatenate(parts, axis=0) if n_sub > 1 else parts[0]
                if rows % HEAD == 0:
                    o_h = _mm(jnp.concatenate([att_full, qib[:, hs]], axis=1),
                              jnp.concatenate([vb[:, hs], s_old.astype(BF16)], axis=0))
                else:
                    o_h = _mm(att_full, vb[:, hs]) + _mm(qib[:, hs], s_old.astype(BF16))
                upd = _mm_tn(ksb[:, hs], vb[:, hs])
                sout_ref[0, hd] = _column_of(decay_row[:, hs]) * s_old + upd
                ms = jnp.mean(o_h * o_h, axis=-1, keepdims=True)
                o_heads.append(o_h * lax.rsqrt(ms + EPS))

        o_n = jnp.concatenate(o_heads, axis=1) * prow(P_GNORM)
        yb = (o_n * sog_s[...]).astype(BF16)

        mixed = ga_s[...] + sgb_s[...] * _mm(yb, _unpack_rows(wb_ref[...]))
        z = _mm(mixed.astype(BF16), _unpack_rows(wout_ref[...]))
        x1 = x_ref[...].reshape(rows, D) + _rms(z, prow(P_GAIN + 1))
        x1_ref[...] = x1.reshape(seqs, tokens, D)


def _resident(shape):
    return pl.BlockSpec(shape, lambda i, j: (0,) * len(shape), pipeline_mode=pl.Buffered(1))


def _mixer(x, cext, h0, s0, pvec, w_in, w_ri, w_a, w_b, w_out, *, seqs, tokens, reset_first, sub, name):
    n_seq, length, _ = x.shape
    n_tiles = length // tokens
    assert n_seq % seqs == 0 and length % tokens == 0 and tokens % SUBLANES == 0
    assert seqs == 1 or (n_tiles == 1 and tokens == SUBLANES)
    assert (seqs * tokens) % sub == 0
    rows = seqs * tokens
    grid = (n_seq // seqs, n_tiles)
    kern = functools.partial(_mixer_kernel, seqs=seqs, tokens=tokens, n_tiles=n_tiles,
                             reset_first=reset_first, sub=sub)
    per_seq_rows = rows if seqs > 1 else SUBLANES
    tile_f32 = pltpu.VMEM((rows, D), F32)
    return pl.pallas_call(
        kern,
        grid=grid,
        in_specs=[
            pl.BlockSpec((seqs, tokens, D), lambda i, j: (i, j, 0)),
            pl.BlockSpec((seqs, SUBLANES, D), lambda i, j: (i, 0, 0)),
            pl.BlockSpec((seqs, 1, D), lambda i, j: (i, 0, 0)),
            pl.BlockSpec((seqs, N_HEADS, HEAD, HEAD), lambda i, j: (i, 0, 0, 0)),
            _resident((P_ROWS, D)),
            _resident(w_in.shape),
            _resident(w_ri.shape),
            _resident(w_a.shape),
            _resident(w_b.shape),
            _resident(w_out.shape),
        ],
        out_specs=[
            pl.BlockSpec((seqs, tokens, D), lambda i, j: (i, j, 0)),
            pl.BlockSpec((seqs, SUBLANES, D), lambda i, j: (i, 0, 0)),
            pl.BlockSpec((seqs, 1, D), lambda i, j: (i, 0, 0)),
            pl.BlockSpec((seqs, N_HEADS, HEAD, HEAD), lambda i, j: (i, 0, 0, 0)),
        ],
        out_shape=[
            jax.ShapeDtypeStruct((n_seq, length, D), F32),
            jax.ShapeDtypeStruct((n_seq, SUBLANES, D), F32),
            jax.ShapeDtypeStruct((n_seq, 1, D), F32),
            jax.ShapeDtypeStruct((n_seq, N_HEADS, HEAD, HEAD), F32),
        ],
        scratch_shapes=[
            pltpu.VMEM((seqs, SUBLANES + tokens, D), F32),
            tile_f32,
            tile_f32, tile_f32, tile_f32,
            tile_f32, tile_f32, tile_f32, tile_f32,
            pltpu.VMEM((per_seq_rows, D), F32),
            pltpu.VMEM((per_seq_rows, D), F32),
        ],
        compiler_params=pltpu.CompilerParams(
            dimension_semantics=("arbitrary", "arbitrary"),
            vmem_limit_bytes=VMEM_LIMIT_BYTES),
        name=name,
    )(x, cext, h0, s0, pvec, w_in, w_ri, w_a, w_b, w_out)


def _mlp_kernel(x_ref, g_ref, wup_ref, wdn_ref, o_ref):
    x = x_ref[...]
    hn = _rms(x, g_ref[0:1, :]).astype(BF16)
    acc = jnp.zeros(x.shape, F32)
    for c in range(D_FF // D):
        t = _mm(hn, _unpack_rows(wup_ref[:, c * D:(c + 1) * D]))
        t = jnp.square(jnp.maximum(t, 0.0)).astype(BF16)
        acc = acc + _mm(t, _unpack_rows(wdn_ref[c * (D // 2):(c + 1) * (D // 2), :]))
    o_ref[...] = x + _rms(acc, g_ref[1:2, :])


def _mlp(x2d, gains, w_up, w_down, *, tile, name):
    n = x2d.shape[0]
    assert n % tile == 0
    return pl.pallas_call(
        _mlp_kernel,
        grid=(n // tile,),
        in_specs=[
            pl.BlockSpec((tile, D), lambda i: (i, 0)),
            pl.BlockSpec((2, D), lambda i: (0, 0), pipeline_mode=pl.Buffered(1)),
            pl.BlockSpec(w_up.shape, lambda i: (0, 0), pipeline_mode=pl.Buffered(1)),
            pl.BlockSpec(w_down.shape, lambda i: (0, 0), pipeline_mode=pl.Buffered(1)),
        ],
        out_specs=pl.BlockSpec((tile, D), lambda i: (i, 0)),
        out_shape=jax.ShapeDtypeStruct((n, D), F32),
        compiler_params=pltpu.CompilerParams(
            dimension_semantics=("arbitrary",),
            vmem_limit_bytes=VMEM_LIMIT_BYTES),
        name=name,
    )(x2d, gains, w_up, w_down)


def kernel(x_prompt, x_sample, state_conv, state_rglru, state_hgrn, meta_tokens, norm_gains, w_in,
           conv_w, conv_b, rg_w, rg_b, ig_w, ig_b, lru_lambda, hgrn_lb, hgrn_gnorm,
           w_branch_a, w_branch_b, w_out, w_up, w_down):
    bp, seq, _ = x_prompt.shape
    bs, dec_seq, _ = x_sample.shape
    layer = 0
    pvec = jnp.concatenate([
        norm_gains[layer], conv_w[layer], conv_b[layer][None], rg_b[layer][None], ig_b[layer][None],
        lru_lambda[layer][None], hgrn_lb[layer:layer + 2], hgrn_gnorm[layer][None],
        jnp.zeros((1, D), F32)], axis=0).astype(F32)
    w_in_b = _pack_rows(w_in[layer])
    w_ri = _pack_rows(jnp.concatenate([rg_w[layer], ig_w[layer]], axis=-1))
    w_a = _pack_rows(w_branch_a[layer])
    w_b = _pack_rows(w_branch_b[layer])
    w_o = _pack_rows(w_out[layer])
    w_u = _pack_rows(w_up[layer])
    w_d = _pack_rows(w_down[layer])
    weights = (pvec, w_in_b, w_ri, w_a, w_b, w_o)

    zeros_c = jnp.zeros((1, SUBLANES, D), F32)
    zeros_h = jnp.zeros((1, 1, D), F32)
    zeros_s = jnp.zeros((1, N_HEADS, HEAD, HEAD), F32)
    _, c_m, h_m, s_m = _mixer(meta_tokens[None].astype(F32), zeros_c, zeros_h, zeros_s, *weights,
                              seqs=1, tokens=N_META, reset_first=True, sub=N_META, name="mixer_meta")

    x1_p, c_p, h_p, s_p = _mixer(
        x_prompt,
        jnp.broadcast_to(c_m, (bp, SUBLANES, D)),
        jnp.broadcast_to(h_m, (bp, 1, D)),
        jnp.broadcast_to(s_m, (bp, N_HEADS, HEAD, HEAD)),
        *weights, seqs=1, tokens=256, reset_first=False, sub=64, name="mixer_prompt")

    cext_s = jnp.pad(state_conv[layer], ((0, 0), (SUBLANES - 3, 0), (0, 0)))
    x1_s, c_s, h_s, s_s = _mixer(
        x_sample, cext_s, state_rglru[layer][:, None, :], state_hgrn[layer],
        *weights, seqs=8, tokens=dec_seq, reset_first=False, sub=8, name="mixer_sample")

    gains_mlp = norm_gains[layer, 2:4]
    y_p = _mlp(x1_p.reshape(bp * seq, D), gains_mlp, w_u, w_d, tile=512, name="mlp_prompt")
    y_s = _mlp(x1_s.reshape(bs * dec_seq, D), gains_mlp, w_u, w_d, tile=512, name="mlp_sample")

    return (y_p.reshape(bp, seq, D), y_s.reshape(bs, dec_seq, D),
            c_p[:, SUBLANES - 3:, :][None], h_p[:, 0, :][None], s_p[None],
            c_s[:, SUBLANES - 3:, :][None], h_s[:, 0, :][None], s_s[None])
```

```python
import functools

import jax
import jax.numpy as jnp
from jax import lax
from jax.experimental import pallas as pl
from jax.experimental.pallas import tpu as pltpu

D = 1024
N_HEADS = 8
HEAD = 128
N_META = 16
LRU_C = 8.0
EPS = 1e-6
D_FF = 4096
SUBLANES = 8
F32 = jnp.float32
BF16 = jnp.bfloat16

P_GAIN, P_CONV_W, P_CONV_B, P_RG_B, P_IG_B, P_LAMBDA, P_LB, P_GNORM, P_ROWS = 0, 4, 8, 9, 10, 11, 12, 14, 16

C_U, C_GATE, C_Q, C_F, C_I, C_OG, C_MA, C_MB = range(8)

VMEM_LIMIT_BYTES = 56 * 1024 * 1024
PACK_BLOCK_ELEMS = 1 << 20


def _rms(x, g):
    ms = jnp.mean(x * x, axis=-1, keepdims=True)
    return x * lax.rsqrt(ms + EPS) * g


def _sigmoid(x):
    return 0.5 * jnp.tanh(0.5 * x) + 0.5


_GELU_C1 = 0.7978845608028654
_GELU_C2 = 0.7978845608028654 * 0.044715


def _gelu_tanh(x):
    half = 0.5 * x
    return half + half * jnp.tanh(x * (_GELU_C1 + _GELU_C2 * (x * x)))


def _mm(a, b):
    return jnp.dot(a, b, preferred_element_type=F32)


def _mm_nt(a, b):
    return lax.dot_general(a, b, (((1,), (1,)), ((), ())), preferred_element_type=F32)


def _mm_tn(a, b):
    return lax.dot_general(a, b, (((0,), (0,)), ((), ())), preferred_element_type=F32)


def _split3(x):
    hi = x.astype(BF16)
    r1 = x - hi.astype(F32)
    mid = r1.astype(BF16)
    lo = (r1 - mid.astype(F32)).astype(BF16)
    return hi, mid, lo


def _mask_matmul(mask_bf, x):
    hi, mid, lo = _split3(x)
    return _mm(mask_bf, hi) + _mm(mask_bf, mid) + _mm(mask_bf, lo)


def _pack_kernel(w_ref, o_ref):
    o_ref[...] = pltpu.bitcast(w_ref[...].astype(BF16), jnp.int32)


def _pack_rows(w, *, name):
    k, n = w.shape
    tile = min(k, PACK_BLOCK_ELEMS // n)
    assert k % tile == 0 and tile % (2 * SUBLANES) == 0
    return pl.pallas_call(
        _pack_kernel,
        grid=(k // tile,),
        in_specs=[pl.BlockSpec((tile, n), lambda i: (i, 0))],
        out_specs=pl.BlockSpec((tile // 2, n), lambda i: (i, 0)),
        out_shape=jax.ShapeDtypeStruct((k // 2, n), jnp.int32),
        compiler_params=pltpu.CompilerParams(dimension_semantics=("arbitrary",)),
        name=name,
    )(w)


def _unpack_rows(words):
    return pltpu.bitcast(words, BF16)


def _column_of(row):
    return jnp.transpose(jnp.broadcast_to(row, (HEAD, HEAD)))


def _mixer_kernel(x_ref, cext_ref, h0_ref, s0_ref, pvec_ref, win_ref, wri_ref, wa_ref, wb_ref, wout_ref,
                  x1_ref, cout_ref, hout_ref, sout_ref,
                  ubuf, hbuf, ga_s, sgb_s, sog_s, qf_s, kk_s, cum_s, v_s, tot_s, oin_s,
                  *, seqs, tokens, n_tiles, reset_first, sub):
    j = pl.program_id(1)
    rows = seqs * tokens
    per_seq = seqs > 1

    @pl.when(j == 0)
    def _init():
        ubuf[:, 0:SUBLANES, :] = cext_ref[...]
        hout_ref[...] = h0_ref[...]
        if not per_seq:
            sout_ref[...] = s0_ref[...]

    pv = pvec_ref[...]

    def prow(r):
        return pv[r:r + 1, :]

    def wcol(g):
        return _unpack_rows(win_ref[:, g * D:(g + 1) * D])

    x = x_ref[...].reshape(rows, D)
    xn = _rms(x, prow(P_GAIN + 0)).astype(BF16)

    row_id = lax.broadcasted_iota(jnp.int32, (rows, 1), 0)
    row_in_group = row_id & (SUBLANES - 1)

    u = _mm(xn, wcol(C_U))
    ubuf[:, SUBLANES:SUBLANES + tokens, :] = u.reshape(seqs, tokens, D)
    uc = prow(P_CONV_B) + prow(P_CONV_W + 3) * u
    for k in range(1, 4):
        shifted = ubuf[:, SUBLANES - k:SUBLANES - k + tokens, :].reshape(rows, D)
        uc = uc + prow(P_CONV_W + 3 - k) * shifted
    tail = ubuf[:, tokens:tokens + SUBLANES, :]
    ubuf[:, 0:SUBLANES, :] = tail

    @pl.when(j == n_tiles - 1)
    def _conv_out():
        cout_ref[...] = tail

    ucb = uc.astype(BF16)
    r_parts, i_parts = [], []
    for b in range(N_HEADS):
        ri = _mm(ucb[:, b * HEAD:(b + 1) * HEAD], _unpack_rows(wri_ref[b]))
        r_parts.append(ri[:, :HEAD])
        i_parts.append(ri[:, HEAD:])
    r_gate = _sigmoid(jnp.concatenate(r_parts, axis=1) + prow(P_RG_B))
    i_gate = _sigmoid(jnp.concatenate(i_parts, axis=1) + prow(P_IG_B))
    lam = prow(P_LAMBDA)
    softplus_neg_lam = jnp.maximum(-lam, 0.0) + jnp.log1p(jnp.exp(-jnp.abs(lam)))
    log_a = (-LRU_C) * r_gate * softplus_neg_lam
    a_cum = jnp.exp(log_a)
    th = jnp.tanh(log_a)
    mult = jnp.sqrt(-2.0 * th / (1.0 - th))
    if reset_first:
        mult = jnp.where(jnp.logical_and(row_id == 0, j == 0), 1.0, mult)
    b_cum = mult * i_gate * uc

    a_cum = a_cum.reshape(rows // SUBLANES, SUBLANES, D)
    b_cum = b_cum.reshape(rows // SUBLANES, SUBLANES, D)
    sublane = lax.broadcasted_iota(jnp.int32, (1, SUBLANES, 1), 1)
    for s in (1, 2, 4):
        keep = sublane >= s
        a_prev = jnp.where(keep, pltpu.roll(a_cum, s, 1), 1.0)
        b_prev = jnp.where(keep, pltpu.roll(b_cum, s, 1), 0.0)
        b_cum = a_cum * b_prev + b_cum
        a_cum = a_cum * a_prev
    a_cum = a_cum.reshape(rows, D)
    b_cum = b_cum.reshape(rows, D)

    if per_seq:
        h_in = jnp.broadcast_to(hout_ref[...], (seqs, SUBLANES, D)).reshape(rows, D)
        h_all = a_cum * h_in + b_cum
        hout_ref[...] = h_all.reshape(seqs, SUBLANES, D)[:, SUBLANES - 1:SUBLANES, :]
    else:
        h = hout_ref[0]
        for g in range(rows // SUBLANES):
            sl = slice(g * SUBLANES, (g + 1) * SUBLANES)
            hg = a_cum[sl] * h + b_cum[sl]
            hbuf[sl, :] = hg
            h = hg[SUBLANES - 1:SUBLANES, :]
        hout_ref[0] = h
        h_all = hbuf[...]

    ya = (h_all * _gelu_tanh(_mm(xn, wcol(C_GATE)))).astype(BF16)
    ga_s[...] = _sigmoid(_mm(xn, wcol(C_MA))) * _mm(ya, _unpack_rows(wa_ref[...]))
    sgb_s[...] = _sigmoid(_mm(xn, wcol(C_MB)))
    og = _mm(xn, wcol(C_OG))
    sog_s[...] = og * _sigmoid(og)

    q = _mm(xn, wcol(C_Q))
    qf_s[...] = q * _sigmoid(q)
    lb_raw = pv[P_LB:P_LB + 2, :]
    lb_exp = jnp.exp(lb_raw - jnp.max(lb_raw, axis=0, keepdims=True))
    lb = lb_exp[0:1, :] / jnp.sum(lb_exp, axis=0, keepdims=True)
    fg = lb + (1.0 - lb) * _sigmoid(_mm(xn, wcol(C_F)))
    logf = jnp.log(fg)
    kk_s[...] = 1.0 - fg
    v_s[...] = _mm(xn, wcol(C_I))

    ri2 = lax.broadcasted_iota(jnp.int32, (rows, rows), 0)
    ci2 = lax.broadcasted_iota(jnp.int32, (rows, rows), 1)
    if per_seq:
        same_seq = (ri2 // tokens) == (ci2 // tokens)
        tri = jnp.logical_and(ci2 <= ri2, same_seq)
        tot_s[...] = _mask_matmul(same_seq.astype(BF16), logf)
    else:
        tri = ci2 <= ri2
    cum_s[...] = _mask_matmul(tri.astype(BF16), logf)

    @pl.when(j >= 0)
    def _second_region():
        qf = qf_s[...]
        kk = kk_s[...]
        cum = cum_s[...]
        v = v_s[...]
        vb = v.astype(BF16)
        o_heads = []

        if per_seq:
            total = tot_s[...]
            qi = qf * jnp.exp(cum)
            kd = (kk * jnp.exp(-cum)).astype(BF16)
            qf_s[...] = qi
            kk_s[...] = kk * jnp.exp(total - cum)
            qib = qi.astype(BF16)
            ri = lax.broadcasted_iota(jnp.int32, (rows, rows), 0)
            ci = lax.broadcasted_iota(jnp.int32, (rows, rows), 1)
            causal = jnp.logical_and(ci <= ri, (ri // tokens) == (ci // tokens))

            def seq_body(g, carry):
                rsl = pl.ds(pl.multiple_of(g * SUBLANES, SUBLANES), SUBLANES)
                for hd in range(N_HEADS):
                    hs = slice(hd * HEAD, (hd + 1) * HEAD)
                    s_old = s0_ref[g, hd]
                    oin_s[rsl, hs] = _mm(qf_s[rsl, hs].astype(BF16), s_old.astype(BF16))
                    upd = _mm_tn(kk_s[rsl, hs].astype(BF16), v_s[rsl, hs].astype(BF16))
                    decay = _column_of(jnp.exp(tot_s[pl.ds(g * SUBLANES, 1), hs]))
                    sout_ref[g, hd] = decay * s_old + upd
                return carry

            lax.fori_loop(0, seqs, seq_body, 0)
            o_inter = oin_s[...]
            for hd in range(N_HEADS):
                hs = slice(hd * HEAD, (hd + 1) * HEAD)
                att = jnp.where(causal, _mm_nt(qib[:, hs], kd[:, hs]), 0.0).astype(BF16)
                o_h = o_inter[:, hs] + _mm(att, vb[:, hs])
                ms = jnp.mean(o_h * o_h, axis=-1, keepdims=True)
                o_heads.append(o_h * lax.rsqrt(ms + EPS))
        else:
            n_sub = rows // sub
            last = cum[rows - 1:rows, :]
            qib = (qf * jnp.exp(cum)).astype(BF16)
            ksb = (kk * jnp.exp(last - cum)).astype(BF16)
            decay_row = jnp.exp(last)
            for hd in range(N_HEADS):
                hs = slice(hd * HEAD, (hd + 1) * HEAD)
                s_old = sout_ref[0, hd]
                parts = []
                for i in range(n_sub):
                    r0 = i * sub
                    r1 = r0 + sub
                    width = min(rows, -(-r1 // HEAD) * HEAD)
                    ref_row = cum[r0 - 1:r0, hs] if i > 0 else jnp.zeros((1, HEAD), F32)
                    qd = (qf[r0:r1, hs] * jnp.exp(cum[r0:r1, hs] - ref_row)).astype(BF16)
                    arg = ref_row - cum[0:width, hs]
                    if width > r1:
                        arg = jnp.where(lax.broadcasted_iota(jnp.int32, (width, 1), 0) < r1, arg, 0.0)
                    kdi = (kk[0:width, hs] * jnp.exp(arg)).astype(BF16)
                    att = _mm_nt(qd, kdi)
                    rr = lax.broadcasted_iota(jnp.int32, (sub, width), 0) + r0
                    cc = lax.broadcasted_iota(jnp.int32, (sub, width), 1)
                    att = jnp.where(cc <= rr, att, 0.0).astype(BF16)
                    if width < rows:
                        att = jnp.concatenate([att, jnp.zeros((sub, rows - width), BF16)], axis=1)
                    parts.append(att)
                att_full = jnp.concatenate(parts, axis=0) if n_sub > 1 else parts[0]
                if rows % HEAD == 0:
                    o_h = _mm(jnp.concatenate([att_full, qib[:, hs]], axis=1),
                              jnp.concatenate([vb[:, hs], s_old.astype(BF16)], axis=0))
                else:
                    o_h = _mm(att_full, vb[:, hs]) + _mm(qib[:, hs], s_old.astype(BF16))
                upd = _mm_tn(ksb[:, hs], vb[:, hs])
                sout_ref[0, hd] = _column_of(decay_row[:, hs]) * s_old + upd
                ms = jnp.mean(o_h * o_h, axis=-1, keepdims=True)
                o_heads.append(o_h * lax.rsqrt(ms + EPS))

        o_n = jnp.concatenate(o_heads, axis=1) * prow(P_GNORM)
        yb = (o_n * sog_s[...]).astype(BF16)

        mixed = ga_s[...] + sgb_s[...] * _mm(yb, _unpack_rows(wb_ref[...]))
        z = _mm(mixed.astype(BF16), _unpack_rows(wout_ref[...]))
        x1 = x_ref[...].reshape(rows, D) + _rms(z, prow(P_GAIN + 1))
        x1_ref[...] = x1.reshape(seqs, tokens, D)


def _resident(shape):
    return pl.BlockSpec(shape, lambda i, j: (0,) * len(shape), pipeline_mode=pl.Buffered(1))


def _mixer(x, cext, h0, s0, pvec, w_in, w_ri, w_a, w_b, w_out, *, seqs, tokens, reset_first, sub, name):
    n_seq, length, _ = x.shape
    n_tiles = length // tokens
    assert n_seq % seqs == 0 and length % tokens == 0 and tokens % SUBLANES == 0
    assert seqs == 1 or (n_tiles == 1 and tokens == SUBLANES)
    assert (seqs * tokens) % sub == 0
    rows = seqs * tokens
    grid = (n_seq // seqs, n_tiles)
    kern = functools.partial(_mixer_kernel, seqs=seqs, tokens=tokens, n_tiles=n_tiles,
                             reset_first=reset_first, sub=sub)
    per_seq_rows = rows if seqs > 1 else SUBLANES
    tile_f32 = pltpu.VMEM((rows, D), F32)
    return pl.pallas_call(
        kern,
        grid=grid,
        in_specs=[
            pl.BlockSpec((seqs, tokens, D), lambda i, j: (i, j, 0)),
            pl.BlockSpec((seqs, SUBLANES, D), lambda i, j: (i, 0, 0)),
            pl.BlockSpec((seqs, 1, D), lambda i, j: (i, 0, 0)),
            pl.BlockSpec((seqs, N_HEADS, HEAD, HEAD), lambda i, j: (i, 0, 0, 0)),
            _resident((P_ROWS, D)),
            _resident(w_in.shape),
            _resident(w_ri.shape),
            _resident(w_a.shape),
            _resident(w_b.shape),
            _resident(w_out.shape),
        ],
        out_specs=[
            pl.BlockSpec((seqs, tokens, D), lambda i, j: (i, j, 0)),
            pl.BlockSpec((seqs, SUBLANES, D), lambda i, j: (i, 0, 0)),
            pl.BlockSpec((seqs, 1, D), lambda i, j: (i, 0, 0)),
            pl.BlockSpec((seqs, N_HEADS, HEAD, HEAD), lambda i, j: (i, 0, 0, 0)),
        ],
        out_shape=[
            jax.ShapeDtypeStruct((n_seq, length, D), F32),
            jax.ShapeDtypeStruct((n_seq, SUBLANES, D), F32),
            jax.ShapeDtypeStruct((n_seq, 1, D), F32),
            jax.ShapeDtypeStruct((n_seq, N_HEADS, HEAD, HEAD), F32),
        ],
        scratch_shapes=[
            pltpu.VMEM((seqs, SUBLANES + tokens, D), F32),
            tile_f32,
            tile_f32, tile_f32, tile_f32,
            tile_f32, tile_f32, tile_f32, tile_f32,
            pltpu.VMEM((per_seq_rows, D), F32),
            pltpu.VMEM((per_seq_rows, D), F32),
        ],
        compiler_params=pltpu.CompilerParams(
            dimension_semantics=("arbitrary", "arbitrary"),
            vmem_limit_bytes=VMEM_LIMIT_BYTES),
        name=name,
    )(x, cext, h0, s0, pvec, w_in, w_ri, w_a, w_b, w_out)


def _mlp_kernel(x_ref, g_ref, wup_ref, wdn_ref, o_ref):
    x = x_ref[...]
    hn = _rms(x, g_ref[0:1, :]).astype(BF16)
    acc = jnp.zeros(x.shape, F32)
    for c in range(D_FF // D):
        t = _mm(hn, _unpack_rows(wup_ref[:, c * D:(c + 1) * D]))
        t = jnp.square(jnp.maximum(t, 0.0)).astype(BF16)
        acc = acc + _mm(t, _unpack_rows(wdn_ref[c * (D // 2):(c + 1) * (D // 2), :]))
    o_ref[...] = x + _rms(acc, g_ref[1:2, :])


def _mlp(x2d, gains, w_up, w_down, *, tile, name):
    n = x2d.shape[0]
    assert n % tile == 0
    return pl.pallas_call(
        _mlp_kernel,
        grid=(n // tile,),
        in_specs=[
            pl.BlockSpec((tile, D), lambda i: (i, 0)),
            pl.BlockSpec((2, D), lambda i: (0, 0), pipeline_mode=pl.Buffered(1)),
            pl.BlockSpec(w_up.shape, lambda i: (0, 0), pipeline_mode=pl.Buffered(1)),
            pl.BlockSpec(w_down.shape, lambda i: (0, 0), pipeline_mode=pl.Buffered(1)),
        ],
        out_specs=pl.BlockSpec((tile, D), lambda i: (i, 0)),
        out_shape=jax.ShapeDtypeStruct((n, D), F32),
        compiler_params=pltpu.CompilerParams(
            dimension_semantics=("arbitrary",),
            vmem_limit_bytes=VMEM_LIMIT_BYTES),
        name=name,
    )(x2d, gains, w_up, w_down)


def kernel(x_prompt, x_sample, state_conv, state_rglru, state_hgrn, meta_tokens, norm_gains, w_in,
           conv_w, conv_b, rg_w, rg_b, ig_w, ig_b, lru_lambda, hgrn_lb, hgrn_gnorm,
           w_branch_a, w_branch_b, w_out, w_up, w_down):
    bp, seq, _ = x_prompt.shape
    bs, dec_seq, _ = x_sample.shape
    layer = 0
    pvec = jnp.concatenate([
        norm_gains[layer], conv_w[layer], conv_b[layer][None], rg_b[layer][None], ig_b[layer][None],
        lru_lambda[layer][None], hgrn_lb[layer:layer + 2], hgrn_gnorm[layer][None],
        jnp.zeros((1, D), F32)], axis=0).astype(F32)
    w_in_b = _pack_rows(w_in[layer], name="pack_w_in")
    w_gates = jnp.concatenate([rg_w[layer], ig_w[layer]], axis=-1)
    w_ri = _pack_rows(w_gates.reshape(N_HEADS * HEAD, 2 * HEAD), name="pack_w_gates")
    w_ri = w_ri.reshape(N_HEADS, HEAD // 2, 2 * HEAD)
    w_a = _pack_rows(w_branch_a[layer], name="pack_w_a")
    w_b = _pack_rows(w_branch_b[layer], name="pack_w_b")
    w_o = _pack_rows(w_out[layer], name="pack_w_out")
    w_u = _pack_rows(w_up[layer], name="pack_w_up")
    w_d = _pack_rows(w_down[layer], name="pack_w_down")
    weights = (pvec, w_in_b, w_ri, w_a, w_b, w_o)

    zeros_c = jnp.zeros((1, SUBLANES, D), F32)
    zeros_h = jnp.zeros((1, 1, D), F32)
    zeros_s = jnp.zeros((1, N_HEADS, HEAD, HEAD), F32)
    _, c_m, h_m, s_m = _mixer(meta_tokens[None].astype(F32), zeros_c, zeros_h, zeros_s, *weights,
                              seqs=1, tokens=N_META, reset_first=True, sub=N_META, name="mixer_meta")

    x1_p, c_p, h_p, s_p = _mixer(
        x_prompt,
        jnp.broadcast_to(c_m, (bp, SUBLANES, D)),
        jnp.broadcast_to(h_m, (bp, 1, D)),
        jnp.broadcast_to(s_m, (bp, N_HEADS, HEAD, HEAD)),
        *weights, seqs=1, tokens=256, reset_first=False, sub=64, name="mixer_prompt")

    cext_s = jnp.pad(state_conv[layer], ((0, 0), (SUBLANES - 3, 0), (0, 0)))
    x1_s, c_s, h_s, s_s = _mixer(
        x_sample, cext_s, state_rglru[layer][:, None, :], state_hgrn[layer],
        *weights, seqs=8, tokens=dec_seq, reset_first=False, sub=8, name="mixer_sample")

    gains_mlp = norm_gains[layer, 2:4]
    y_p = _mlp(x1_p.reshape(bp * seq, D), gains_mlp, w_u, w_d, tile=512, name="mlp_prompt")
    y_s = _mlp(x1_s.reshape(bs * dec_seq, D), gains_mlp, w_u, w_d, tile=512, name="mlp_sample")

    return (y_p.reshape(bp, seq, D), y_s.reshape(bs, dec_seq, D),
            c_p[:, SUBLANES - 3:, :][None], h_p[:, 0, :][None], s_p[None],
            c_s[:, SUBLANES - 3:, :][None], h_s[:, 0, :][None], s_s[None])
```

```python
import functools

import jax
import jax.numpy as jnp
from jax import lax
from jax.experimental import pallas as pl
from jax.experimental.pallas import tpu as pltpu

D = 1024
N_HEADS = 8
HEAD = 128
N_META = 16
LRU_C = 8.0
EPS = 1e-6
D_FF = 4096
SUBLANES = 8
F32 = jnp.float32
BF16 = jnp.bfloat16

P_GAIN, P_CONV_W, P_CONV_B, P_RG_B, P_IG_B, P_LAMBDA, P_LB, P_GNORM, P_ROWS = 0, 4, 8, 9, 10, 11, 12, 14, 16

C_U, C_GATE, C_Q, C_F, C_I, C_OG, C_MA, C_MB = range(8)

VMEM_LIMIT_BYTES = 56 * 1024 * 1024
PACK_BLOCK_ELEMS = 1 << 20


def _rms(x, g):
    ms = jnp.mean(x * x, axis=-1, keepdims=True)
    return x * lax.rsqrt(ms + EPS) * g


def _sigmoid(x):
    return 0.5 * jnp.tanh(0.5 * x) + 0.5


_GELU_C1 = 0.7978845608028654
_GELU_C2 = 0.7978845608028654 * 0.044715


def _gelu_tanh(x):
    half = 0.5 * x
    return half + half * jnp.tanh(x * (_GELU_C1 + _GELU_C2 * (x * x)))


def _mm(a, b):
    return jnp.dot(a, b, preferred_element_type=F32)


def _mm_nt(a, b):
    return lax.dot_general(a, b, (((1,), (1,)), ((), ())), preferred_element_type=F32)


def _mm_tn(a, b):
    return lax.dot_general(a, b, (((0,), (0,)), ((), ())), preferred_element_type=F32)


def _split3(x):
    hi = x.astype(BF16)
    r1 = x - hi.astype(F32)
    mid = r1.astype(BF16)
    lo = (r1 - mid.astype(F32)).astype(BF16)
    return hi, mid, lo


def _mask_matmul(mask_bf, x):
    hi, mid, lo = _split3(x)
    return _mm(mask_bf, hi) + _mm(mask_bf, mid) + _mm(mask_bf, lo)


def _pack_kernel(w_ref, o_ref):
    o_ref[...] = pltpu.bitcast(w_ref[...].astype(BF16), jnp.int32)


def _pack_rows(w, *, name):
    k, n = w.shape
    tile = min(k, PACK_BLOCK_ELEMS // n)
    assert k % tile == 0 and tile % (2 * SUBLANES) == 0
    return pl.pallas_call(
        _pack_kernel,
        grid=(k // tile,),
        in_specs=[pl.BlockSpec((tile, n), lambda i: (i, 0))],
        out_specs=pl.BlockSpec((tile // 2, n), lambda i: (i, 0)),
        out_shape=jax.ShapeDtypeStruct((k // 2, n), jnp.int32),
        compiler_params=pltpu.CompilerParams(dimension_semantics=("arbitrary",)),
        name=name,
    )(w)


def _unpack_rows(words):
    return pltpu.bitcast(words, BF16)


def _column_of(row):
    return jnp.transpose(jnp.broadcast_to(row, (HEAD, HEAD)))


def _mixer_kernel(x_ref, cext_ref, h0_ref, s0_ref, pvec_ref, win_ref, wri_ref, wa_ref, wb_ref, wout_ref,
                  x1_ref, cout_ref, hout_ref, sout_ref, *scratch,
                  seqs, tokens, n_tiles, reset_first, sub):
    j = pl.program_id(1)
    rows = seqs * tokens
    decode = seqs > 1
    if decode:
        ubuf, ga_s, sgb_s, sog_s, qi_h, ks_h, kd_h, v_h, tot_h, oin_h = scratch
    else:
        ubuf, ga_s, sgb_s, sog_s, hbuf, qf_s, kk_s, cum_s, v_s = scratch

    pv = pvec_ref[...]

    def prow(r):
        return pv[r:r + 1, :]

    def wcol(g):
        return _unpack_rows(win_ref[:, g * D:(g + 1) * D])

    def head_cols(hd):
        return slice(hd * HEAD, (hd + 1) * HEAD)

    def front():
        x = x_ref[...].reshape(rows, D)
        xn = _rms(x, prow(P_GAIN + 0)).astype(BF16)
        row_id = lax.broadcasted_iota(jnp.int32, (rows, 1), 0)

        u = _mm(xn, wcol(C_U))
        ubuf[:, SUBLANES:SUBLANES + tokens, :] = u.reshape(seqs, tokens, D)
        uc = prow(P_CONV_B) + prow(P_CONV_W + 3) * u
        for k in range(1, 4):
            shifted = ubuf[:, SUBLANES - k:SUBLANES - k + tokens, :].reshape(rows, D)
            uc = uc + prow(P_CONV_W + 3 - k) * shifted
        tail = ubuf[:, tokens:tokens + SUBLANES, :]
        ubuf[:, 0:SUBLANES, :] = tail
        if decode:
            cout_ref[...] = tail
        else:
            @pl.when(j == n_tiles - 1)
            def _conv_out():
                cout_ref[...] = tail

        ucb = uc.astype(BF16)
        r_parts, i_parts = [], []
        for b in range(N_HEADS):
            ri = _mm(ucb[:, head_cols(b)], _unpack_rows(wri_ref[b]))
            r_parts.append(ri[:, :HEAD])
            i_parts.append(ri[:, HEAD:])
        r_gate = _sigmoid(jnp.concatenate(r_parts, axis=1) + prow(P_RG_B))
        i_gate = _sigmoid(jnp.concatenate(i_parts, axis=1) + prow(P_IG_B))
        lam = prow(P_LAMBDA)
        softplus_neg_lam = jnp.maximum(-lam, 0.0) + jnp.log1p(jnp.exp(-jnp.abs(lam)))
        log_a = (-LRU_C) * r_gate * softplus_neg_lam
        a_cum = jnp.exp(log_a)
        th = jnp.tanh(log_a)
        mult = jnp.sqrt(-2.0 * th / (1.0 - th))
        if reset_first:
            mult = jnp.where(jnp.logical_and(row_id == 0, j == 0), 1.0, mult)
        b_cum = mult * i_gate * uc

        a_cum = a_cum.reshape(rows // SUBLANES, SUBLANES, D)
        b_cum = b_cum.reshape(rows // SUBLANES, SUBLANES, D)
        sublane = lax.broadcasted_iota(jnp.int32, (1, SUBLANES, 1), 1)
        for s in (1, 2, 4):
            keep = sublane >= s
            a_prev = jnp.where(keep, pltpu.roll(a_cum, s, 1), 1.0)
            b_prev = jnp.where(keep, pltpu.roll(b_cum, s, 1), 0.0)
            b_cum = a_cum * b_prev + b_cum
            a_cum = a_cum * a_prev
        a_cum = a_cum.reshape(rows, D)
        b_cum = b_cum.reshape(rows, D)

        if decode:
            h_in = jnp.broadcast_to(h0_ref[...], (seqs, SUBLANES, D)).reshape(rows, D)
            h_all = a_cum * h_in + b_cum
            hout_ref[...] = h_all.reshape(seqs, SUBLANES, D)[:, SUBLANES - 1:SUBLANES, :]
        else:
            h = hout_ref[0]
            for g in range(rows // SUBLANES):
                sl = slice(g * SUBLANES, (g + 1) * SUBLANES)
                hg = a_cum[sl] * h + b_cum[sl]
                hbuf[sl, :] = hg
                h = hg[SUBLANES - 1:SUBLANES, :]
            hout_ref[0] = h
            h_all = hbuf[...]

        ya = (h_all * _gelu_tanh(_mm(xn, wcol(C_GATE)))).astype(BF16)
        ga_s[...] = _sigmoid(_mm(xn, wcol(C_MA))) * _mm(ya, _unpack_rows(wa_ref[...]))
        sgb_s[...] = _sigmoid(_mm(xn, wcol(C_MB)))
        og = _mm(xn, wcol(C_OG))
        sog_s[...] = og * _sigmoid(og)

        q = _mm(xn, wcol(C_Q))
        qf = q * _sigmoid(q)
        lb_raw = pv[P_LB:P_LB + 2, :]
        lb_exp = jnp.exp(lb_raw - jnp.max(lb_raw, axis=0, keepdims=True))
        lb = lb_exp[0:1, :] / jnp.sum(lb_exp, axis=0, keepdims=True)
        fg = lb + (1.0 - lb) * _sigmoid(_mm(xn, wcol(C_F)))
        logf = jnp.log(fg)
        kk = 1.0 - fg
        v = _mm(xn, wcol(C_I))

        ri2 = lax.broadcasted_iota(jnp.int32, (rows, rows), 0)
        ci2 = lax.broadcasted_iota(jnp.int32, (rows, rows), 1)
        if decode:
            same_seq = (ri2 // tokens) == (ci2 // tokens)
            total = _mask_matmul(same_seq.astype(BF16), logf)
            cum = _mask_matmul(jnp.logical_and(ci2 <= ri2, same_seq).astype(BF16), logf)
            qi = qf * jnp.exp(cum)
            kd = kk * jnp.exp(-cum)
            ks = kk * jnp.exp(total - cum)
            for hd in range(N_HEADS):
                hs = head_cols(hd)
                qi_h[hd] = qi[:, hs]
                kd_h[hd] = kd[:, hs]
                ks_h[hd] = ks[:, hs]
                v_h[hd] = v[:, hs]
                tot_h[hd] = total[:, hs]
        else:
            qf_s[...] = qf
            kk_s[...] = kk
            v_s[...] = v
            cum_s[...] = _mask_matmul((ci2 <= ri2).astype(BF16), logf)

    def finish(o_heads):
        o_n = jnp.concatenate(o_heads, axis=1) * prow(P_GNORM)
        yb = (o_n * sog_s[...]).astype(BF16)
        mixed = ga_s[...] + sgb_s[...] * _mm(yb, _unpack_rows(wb_ref[...]))
        z = _mm(mixed.astype(BF16), _unpack_rows(wout_ref[...]))
        x1 = x_ref[...].reshape(rows, D) + _rms(z, prow(P_GAIN + 1))
        x1_ref[...] = x1.reshape(seqs, tokens, D)

    def head_norm(o_h):
        ms = jnp.mean(o_h * o_h, axis=-1, keepdims=True)
        return o_h * lax.rsqrt(ms + EPS)

    def back_prompt():
        qf = qf_s[...]
        kk = kk_s[...]
        cum = cum_s[...]
        vb = v_s[...].astype(BF16)
        n_sub = rows // sub
        last = cum[rows - 1:rows, :]
        qib = (qf * jnp.exp(cum)).astype(BF16)
        ksb = (kk * jnp.exp(last - cum)).astype(BF16)
        decay_row = jnp.exp(last)
        o_heads = []
        for hd in range(N_HEADS):
            hs = head_cols(hd)
            s_old = sout_ref[0, hd]
            parts = []
            for i in range(n_sub):
                r0 = i * sub
                r1 = r0 + sub
                width = min(rows, -(-r1 // HEAD) * HEAD)
                ref_row = cum[r0 - 1:r0, hs] if i > 0 else jnp.zeros((1, HEAD), F32)
                qd = (qf[r0:r1, hs] * jnp.exp(cum[r0:r1, hs] - ref_row)).astype(BF16)
                arg = ref_row - cum[0:width, hs]
                if width > r1:
                    arg = jnp.where(lax.broadcasted_iota(jnp.int32, (width, 1), 0) < r1, arg, 0.0)
                kdi = (kk[0:width, hs] * jnp.exp(arg)).astype(BF16)
                att = _mm_nt(qd, kdi)
                rr = lax.broadcasted_iota(jnp.int32, (sub, width), 0) + r0
                cc = lax.broadcasted_iota(jnp.int32, (sub, width), 1)
                att = jnp.where(cc <= rr, att, 0.0).astype(BF16)
                if width < rows:
                    att = jnp.concatenate([att, jnp.zeros((sub, rows - width), BF16)], axis=1)
                parts.append(att)
            att_full = jnp.concatenate(parts, axis=0) if n_sub > 1 else parts[0]
            if rows % HEAD == 0:
                o_h = _mm(jnp.concatenate([att_full, qib[:, hs]], axis=1),
                          jnp.concatenate([vb[:, hs], s_old.astype(BF16)], axis=0))
            else:
                o_h = _mm(att_full, vb[:, hs]) + _mm(qib[:, hs], s_old.astype(BF16))
            upd = _mm_tn(ksb[:, hs], vb[:, hs])
            sout_ref[0, hd] = _column_of(decay_row[:, hs]) * s_old + upd
            o_heads.append(head_norm(o_h))
        finish(o_heads)

    def decode_state_step():
        def seq_body(g, carry):
            rsl = pl.ds(pl.multiple_of(g * SUBLANES, SUBLANES), SUBLANES)
            s_old = s0_ref[g, 0]
            oin_h[j, rsl, :] = _mm(qi_h[j, rsl, :].astype(BF16), s_old.astype(BF16))
            upd = _mm_tn(ks_h[j, rsl, :].astype(BF16), v_h[j, rsl, :].astype(BF16))
            decay = _column_of(jnp.exp(tot_h[j, pl.ds(g * SUBLANES, 1), :]))
            sout_ref[g, 0] = decay * s_old + upd
            return carry

        lax.fori_loop(0, seqs, seq_body, 0, unroll=SUBLANES)

    def back_decode():
        ri = lax.broadcasted_iota(jnp.int32, (rows, rows), 0)
        ci = lax.broadcasted_iota(jnp.int32, (rows, rows), 1)
        causal = jnp.logical_and(ci <= ri, (ri // tokens) == (ci // tokens))
        o_heads = []
        for hd in range(N_HEADS):
            scores = _mm_nt(qi_h[hd].astype(BF16), kd_h[hd].astype(BF16))
            att = jnp.where(causal, scores, 0.0).astype(BF16)
            o_heads.append(head_norm(oin_h[hd] + _mm(att, v_h[hd].astype(BF16))))
        finish(o_heads)

    if decode:
        @pl.when(j == 0)
        def _front_region():
            ubuf[:, 0:SUBLANES, :] = cext_ref[...]
            front()

        decode_state_step()

        @pl.when(j == N_HEADS - 1)
        def _back_region():
            back_decode()
    else:
        @pl.when(j == 0)
        def _init():
            ubuf[:, 0:SUBLANES, :] = cext_ref[...]
            hout_ref[...] = h0_ref[...]
            sout_ref[...] = s0_ref[...]

        front()

        @pl.when(j >= 0)
        def _back_region():
            back_prompt()


def _resident(shape):
    return pl.BlockSpec(shape, lambda i, j: (0,) * len(shape), pipeline_mode=pl.Buffered(1))


def _mixer(x, cext, h0, s0, pvec, w_in, w_ri, w_a, w_b, w_out, *, seqs, tokens, reset_first, sub, name):
    n_seq, length, _ = x.shape
    n_tiles = length // tokens
    decode = seqs > 1
    assert n_seq % seqs == 0 and length % tokens == 0 and tokens % SUBLANES == 0
    assert not decode or (n_tiles == 1 and tokens == SUBLANES)
    assert (seqs * tokens) % sub == 0
    rows = seqs * tokens
    kern = functools.partial(_mixer_kernel, seqs=seqs, tokens=tokens, n_tiles=n_tiles,
                             reset_first=reset_first, sub=sub)
    tile_f32 = pltpu.VMEM((rows, D), F32)
    by_head_f32 = pltpu.VMEM((N_HEADS, rows, HEAD), F32)
    common = [pltpu.VMEM((seqs, SUBLANES + tokens, D), F32),
              tile_f32, tile_f32, tile_f32]
    if decode:
        grid = (n_seq // seqs, N_HEADS)
        x_map = lambda i, j: (i, 0, 0)
        s_block = (seqs, 1, HEAD, HEAD)
        s_map = lambda i, j: (i, j, 0, 0)
        scratch = common + [by_head_f32] * 6
    else:
        grid = (n_seq, n_tiles)
        x_map = lambda i, j: (i, j, 0)
        s_block = (1, N_HEADS, HEAD, HEAD)
        s_map = lambda i, j: (i, 0, 0, 0)
        scratch = common + [tile_f32] * 5
    seq_map = lambda i, j: (i, 0, 0)
    return pl.pallas_call(
        kern,
        grid=grid,
        in_specs=[
            pl.BlockSpec((seqs, tokens, D), x_map),
            pl.BlockSpec((seqs, SUBLANES, D), seq_map),
            pl.BlockSpec((seqs, 1, D), seq_map),
            pl.BlockSpec(s_block, s_map),
            _resident((P_ROWS, D)),
            _resident(w_in.shape),
            _resident(w_ri.shape),
            _resident(w_a.shape),
            _resident(w_b.shape),
            _resident(w_out.shape),
        ],
        out_specs=[
            pl.BlockSpec((seqs, tokens, D), x_map),
            pl.BlockSpec((seqs, SUBLANES, D), seq_map),
            pl.BlockSpec((seqs, 1, D), seq_map),
            pl.BlockSpec(s_block, s_map),
        ],
        out_shape=[
            jax.ShapeDtypeStruct((n_seq, length, D), F32),
            jax.ShapeDtypeStruct((n_seq, SUBLANES, D), F32),
            jax.ShapeDtypeStruct((n_seq, 1, D), F32),
            jax.ShapeDtypeStruct((n_seq, N_HEADS, HEAD, HEAD), F32),
        ],
        scratch_shapes=scratch,
        compiler_params=pltpu.CompilerParams(
            dimension_semantics=("arbitrary", "arbitrary"),
            vmem_limit_bytes=VMEM_LIMIT_BYTES),
        name=name,
    )(x, cext, h0, s0, pvec, w_in, w_ri, w_a, w_b, w_out)


def _mlp_kernel(x_ref, g_ref, wup_ref, wdn_ref, o_ref):
    x = x_ref[...]
    hn = _rms(x, g_ref[0:1, :]).astype(BF16)
    acc = jnp.zeros(x.shape, F32)
    for c in range(D_FF // D):
        t = _mm(hn, _unpack_rows(wup_ref[:, c * D:(c + 1) * D]))
        t = jnp.square(jnp.maximum(t, 0.0)).astype(BF16)
        acc = acc + _mm(t, _unpack_rows(wdn_ref[c * (D // 2):(c + 1) * (D // 2), :]))
    o_ref[...] = x + _rms(acc, g_ref[1:2, :])


def _mlp(x2d, gains, w_up, w_down, *, tile, name):
    n = x2d.shape[0]
    assert n % tile == 0
    return pl.pallas_call(
        _mlp_kernel,
        grid=(n // tile,),
        in_specs=[
            pl.BlockSpec((tile, D), lambda i: (i, 0)),
            pl.BlockSpec((2, D), lambda i: (0, 0), pipeline_mode=pl.Buffered(1)),
            pl.BlockSpec(w_up.shape, lambda i: (0, 0), pipeline_mode=pl.Buffered(1)),
            pl.BlockSpec(w_down.shape, lambda i: (0, 0), pipeline_mode=pl.Buffered(1)),
        ],
        out_specs=pl.BlockSpec((tile, D), lambda i: (i, 0)),
        out_shape=jax.ShapeDtypeStruct((n, D), F32),
        compiler_params=pltpu.CompilerParams(
            dimension_semantics=("arbitrary",),
            vmem_limit_bytes=VMEM_LIMIT_BYTES),
        name=name,
    )(x2d, gains, w_up, w_down)


def kernel(x_prompt, x_sample, state_conv, state_rglru, state_hgrn, meta_tokens, norm_gains, w_in,
           conv_w, conv_b, rg_w, rg_b, ig_w, ig_b, lru_lambda, hgrn_lb, hgrn_gnorm,
           w_branch_a, w_branch_b, w_out, w_up, w_down):
    bp, seq, _ = x_prompt.shape
    bs, dec_seq, _ = x_sample.shape
    layer = 0
    pvec = jnp.concatenate([
        norm_gains[layer], conv_w[layer], conv_b[layer][None], rg_b[layer][None], ig_b[layer][None],
        lru_lambda[layer][None], hgrn_lb[layer:layer + 2], hgrn_gnorm[layer][None],
        jnp.zeros((1, D), F32)], axis=0).astype(F32)
    w_in_b = _pack_rows(w_in[layer], name="pack_w_in")
    w_gates = jnp.concatenate([rg_w[layer], ig_w[layer]], axis=-1)
    w_ri = _pack_rows(w_gates.reshape(N_HEADS * HEAD, 2 * HEAD), name="pack_w_gates")
    w_ri = w_ri.reshape(N_HEADS, HEAD // 2, 2 * HEAD)
    w_a = _pack_rows(w_branch_a[layer], name="pack_w_a")
    w_b = _pack_rows(w_branch_b[layer], name="pack_w_b")
    w_o = _pack_rows(w_out[layer], name="pack_w_out")
    w_u = _pack_rows(w_up[layer], name="pack_w_up")
    w_d = _pack_rows(w_down[layer], name="pack_w_down")
    weights = (pvec, w_in_b, w_ri, w_a, w_b, w_o)

    zeros_c = jnp.zeros((1, SUBLANES, D), F32)
    zeros_h = jnp.zeros((1, 1, D), F32)
    zeros_s = jnp.zeros((1, N_HEADS, HEAD, HEAD), F32)
    _, c_m, h_m, s_m = _mixer(meta_tokens[None].astype(F32), zeros_c, zeros_h, zeros_s, *weights,
                              seqs=1, tokens=N_META, reset_first=True, sub=N_META, name="mixer_meta")

    x1_p, c_p, h_p, s_p = _mixer(
        x_prompt,
        jnp.broadcast_to(c_m, (bp, SUBLANES, D)),
        jnp.broadcast_to(h_m, (bp, 1, D)),
        jnp.broadcast_to(s_m, (bp, N_HEADS, HEAD, HEAD)),
        *weights, seqs=1, tokens=256, reset_first=False, sub=64, name="mixer_prompt")

    cext_s = jnp.pad(state_conv[layer], ((0, 0), (SUBLANES - 3, 0), (0, 0)))
    x1_s, c_s, h_s, s_s = _mixer(
        x_sample, cext_s, state_rglru[layer][:, None, :], state_hgrn[layer],
        *weights, seqs=32, tokens=dec_seq, reset_first=False, sub=8, name="mixer_sample")

    gains_mlp = norm_gains[layer, 2:4]
    y_p = _mlp(x1_p.reshape(bp * seq, D), gains_mlp, w_u, w_d, tile=512, name="mlp_prompt")
    y_s = _mlp(x1_s.reshape(bs * dec_seq, D), gains_mlp, w_u, w_d, tile=512, name="mlp_sample")

    return (y_p.reshape(bp, seq, D), y_s.reshape(bs, dec_seq, D),
            c_p[:, SUBLANES - 3:, :][None], h_p[:, 0, :][None], s_p[None],
            c_s[:, SUBLANES - 3:, :][None], h_s[:, 0, :][None], s_s[None])
```

```python
import functools

import jax
import jax.numpy as jnp
from jax import lax
from jax.experimental import pallas as pl
from jax.experimental.pallas import tpu as pltpu

D = 1024
N_HEADS = 8
HEAD = 128
N_META = 16
LRU_C = 8.0
EPS = 1e-6
D_FF = 4096
SUBLANES = 8
F32 = jnp.float32
BF16 = jnp.bfloat16

P_GAIN, P_CONV_W, P_CONV_B, P_RG_B, P_IG_B, P_LAMBDA, P_LB, P_GNORM, P_ROWS = 0, 4, 8, 9, 10, 11, 12, 14, 16

C_U, C_GATE, C_Q, C_F, C_I, C_OG, C_MA, C_MB = range(8)

VMEM_LIMIT_BYTES = 56 * 1024 * 1024
PACK_BLOCK_ELEMS = 1 << 20
SAFE_DECAY_RANGE = 64.0


def _rms(x, g):
    ms = jnp.mean(x * x, axis=-1, keepdims=True)
    return x * lax.rsqrt(ms + EPS) * g


def _sigmoid(x):
    return 0.5 * jnp.tanh(0.5 * x) + 0.5


_GELU_C1 = 0.7978845608028654
_GELU_C2 = 0.7978845608028654 * 0.044715


def _gelu_tanh(x):
    half = 0.5 * x
    return half + half * jnp.tanh(x * (_GELU_C1 + _GELU_C2 * (x * x)))


def _mm(a, b):
    return jnp.dot(a, b, preferred_element_type=F32)


def _mm_nt(a, b):
    return lax.dot_general(a, b, (((1,), (1,)), ((), ())), preferred_element_type=F32)


def _mm_tn(a, b):
    return lax.dot_general(a, b, (((0,), (0,)), ((), ())), preferred_element_type=F32)


def _split3(x):
    hi = x.astype(BF16)
    r1 = x - hi.astype(F32)
    mid = r1.astype(BF16)
    lo = (r1 - mid.astype(F32)).astype(BF16)
    return hi, mid, lo


def _mask_matmul(mask_bf, x):
    hi, mid, lo = _split3(x)
    return _mm(mask_bf, hi) + _mm(mask_bf, mid) + _mm(mask_bf, lo)


def _pack_kernel(w_ref, o_ref):
    o_ref[...] = pltpu.bitcast(w_ref[...].astype(BF16), jnp.int32)


def _pack_rows(w, *, name):
    k, n = w.shape
    tile = min(k, PACK_BLOCK_ELEMS // n)
    assert k % tile == 0 and tile % (2 * SUBLANES) == 0
    return pl.pallas_call(
        _pack_kernel,
        grid=(k // tile,),
        in_specs=[pl.BlockSpec((tile, n), lambda i: (i, 0))],
        out_specs=pl.BlockSpec((tile // 2, n), lambda i: (i, 0)),
        out_shape=jax.ShapeDtypeStruct((k // 2, n), jnp.int32),
        compiler_params=pltpu.CompilerParams(dimension_semantics=("arbitrary",)),
        name=name,
    )(w)


def _unpack_rows(words):
    return pltpu.bitcast(words, BF16)


def _exact_block(q_b, k_b, c_b, v_b):
    t_idx = lax.broadcasted_iota(jnp.int32, (SUBLANES, 1), 0)
    acc = jnp.zeros((SUBLANES, HEAD), F32)
    for s in range(SUBLANES):
        decay = jnp.exp(jnp.minimum(c_b - c_b[s:s + 1, :], 0.0))
        score = jnp.sum(q_b * k_b[s:s + 1, :] * decay, axis=-1, keepdims=True)
        acc = acc + jnp.where(t_idx >= s, score, 0.0) * v_b[s:s + 1, :]
    return acc


def _column_of(row):
    return jnp.transpose(jnp.broadcast_to(row, (HEAD, HEAD)))


def _mixer_kernel(x_ref, cext_ref, h0_ref, s0_ref, pvec_ref, win_ref, wri_ref, wa_ref, wb_ref, wout_ref,
                  x1_ref, cout_ref, hout_ref, sout_ref, *scratch,
                  seqs, tokens, n_tiles, reset_first, sub):
    j = pl.program_id(1)
    rows = seqs * tokens
    decode = seqs > 1
    if decode:
        ubuf, ga_s, sgb_s, sog_s, qf_h, kk_h, cum_h, v_h, tot_h, oin_h, oi_h = scratch
    else:
        ubuf, ga_s, sgb_s, sog_s, hbuf, qf_s, kk_s, cum_s, v_s, o_s, sold_s = scratch

    pv = pvec_ref[...]

    def prow(r):
        return pv[r:r + 1, :]

    def wcol(g):
        return _unpack_rows(win_ref[:, g * D:(g + 1) * D])

    def head_cols(hd):
        return slice(hd * HEAD, (hd + 1) * HEAD)

    def front():
        x = x_ref[...].reshape(rows, D)
        xn = _rms(x, prow(P_GAIN + 0)).astype(BF16)
        row_id = lax.broadcasted_iota(jnp.int32, (rows, 1), 0)

        u = _mm(xn, wcol(C_U))
        ubuf[:, SUBLANES:SUBLANES + tokens, :] = u.reshape(seqs, tokens, D)
        uc = prow(P_CONV_B) + prow(P_CONV_W + 3) * u
        for k in range(1, 4):
            shifted = ubuf[:, SUBLANES - k:SUBLANES - k + tokens, :].reshape(rows, D)
            uc = uc + prow(P_CONV_W + 3 - k) * shifted
        tail = ubuf[:, tokens:tokens + SUBLANES, :]
        ubuf[:, 0:SUBLANES, :] = tail
        if decode:
            cout_ref[...] = tail
        else:
            @pl.when(j == n_tiles - 1)
            def _conv_out():
                cout_ref[...] = tail

        ucb = uc.astype(BF16)
        r_parts, i_parts = [], []
        for b in range(N_HEADS):
            ri = _mm(ucb[:, head_cols(b)], _unpack_rows(wri_ref[b]))
            r_parts.append(ri[:, :HEAD])
            i_parts.append(ri[:, HEAD:])
        r_gate = _sigmoid(jnp.concatenate(r_parts, axis=1) + prow(P_RG_B))
        i_gate = _sigmoid(jnp.concatenate(i_parts, axis=1) + prow(P_IG_B))
        lam = prow(P_LAMBDA)
        softplus_neg_lam = jnp.maximum(-lam, 0.0) + jnp.log1p(jnp.exp(-jnp.abs(lam)))
        log_a = (-LRU_C) * r_gate * softplus_neg_lam
        a_cum = jnp.exp(log_a)
        th = jnp.tanh(log_a)
        mult = jnp.sqrt(-2.0 * th / (1.0 - th))
        if reset_first:
            mult = jnp.where(jnp.logical_and(row_id == 0, j == 0), 1.0, mult)
        b_cum = mult * i_gate * uc

        a_cum = a_cum.reshape(rows // SUBLANES, SUBLANES, D)
        b_cum = b_cum.reshape(rows // SUBLANES, SUBLANES, D)
        sublane = lax.broadcasted_iota(jnp.int32, (1, SUBLANES, 1), 1)
        for s in (1, 2, 4):
            keep = sublane >= s
            a_prev = jnp.where(keep, pltpu.roll(a_cum, s, 1), 1.0)
            b_prev = jnp.where(keep, pltpu.roll(b_cum, s, 1), 0.0)
            b_cum = a_cum * b_prev + b_cum
            a_cum = a_cum * a_prev
        a_cum = a_cum.reshape(rows, D)
        b_cum = b_cum.reshape(rows, D)

        if decode:
            h_in = jnp.broadcast_to(h0_ref[...], (seqs, SUBLANES, D)).reshape(rows, D)
            h_all = a_cum * h_in + b_cum
            hout_ref[...] = h_all.reshape(seqs, SUBLANES, D)[:, SUBLANES - 1:SUBLANES, :]
        else:
            h = hout_ref[0]
            for g in range(rows // SUBLANES):
                sl = slice(g * SUBLANES, (g + 1) * SUBLANES)
                hg = a_cum[sl] * h + b_cum[sl]
                hbuf[sl, :] = hg
                h = hg[SUBLANES - 1:SUBLANES, :]
            hout_ref[0] = h
            h_all = hbuf[...]

        ya = (h_all * _gelu_tanh(_mm(xn, wcol(C_GATE)))).astype(BF16)
        ga_s[...] = _sigmoid(_mm(xn, wcol(C_MA))) * _mm(ya, _unpack_rows(wa_ref[...]))
        sgb_s[...] = _sigmoid(_mm(xn, wcol(C_MB)))
        og = _mm(xn, wcol(C_OG))
        sog_s[...] = og * _sigmoid(og)

        q = _mm(xn, wcol(C_Q))
        qf = q * _sigmoid(q)
        lb_raw = pv[P_LB:P_LB + 2, :]
        lb_exp = jnp.exp(lb_raw - jnp.max(lb_raw, axis=0, keepdims=True))
        lb = lb_exp[0:1, :] / jnp.sum(lb_exp, axis=0, keepdims=True)
        fg = lb + (1.0 - lb) * _sigmoid(_mm(xn, wcol(C_F)))
        logf = jnp.log(fg)
        kk = 1.0 - fg
        v = _mm(xn, wcol(C_I))

        ri2 = lax.broadcasted_iota(jnp.int32, (rows, rows), 0)
        ci2 = lax.broadcasted_iota(jnp.int32, (rows, rows), 1)
        if decode:
            same_seq = (ri2 // tokens) == (ci2 // tokens)
            total = _mask_matmul(same_seq.astype(BF16), logf)
            cum = _mask_matmul(jnp.logical_and(ci2 <= ri2, same_seq).astype(BF16), logf)
            for hd in range(N_HEADS):
                hs = head_cols(hd)
                v_h[hd] = v[:, hs]
                tot_h[hd] = total[:, hs]
                qf_h[hd] = qf[:, hs]
                kk_h[hd] = kk[:, hs]
                cum_h[hd] = cum[:, hs]
        else:
            qf_s[...] = qf
            kk_s[...] = kk
            v_s[...] = v
            cum_s[...] = _mask_matmul((ci2 <= ri2).astype(BF16), logf)

    def head_norm(o_h):
        ms = jnp.mean(o_h * o_h, axis=-1, keepdims=True)
        return o_h * lax.rsqrt(ms + EPS)

    def finish(o_heads):
        o_n = jnp.concatenate([head_norm(o_h) for o_h in o_heads], axis=1) * prow(P_GNORM)
        yb = (o_n * sog_s[...]).astype(BF16)
        mixed = ga_s[...] + sgb_s[...] * _mm(yb, _unpack_rows(wb_ref[...]))
        z = _mm(mixed.astype(BF16), _unpack_rows(wout_ref[...]))
        x1 = x_ref[...].reshape(rows, D) + _rms(z, prow(P_GAIN + 1))
        x1_ref[...] = x1.reshape(seqs, tokens, D)

    def back_prompt():
        qf = qf_s[...]
        kk = kk_s[...]
        cum = cum_s[...]
        vb = v_s[...].astype(BF16)
        n_sub = rows // sub
        last = cum[rows - 1:rows, :]
        qib = (qf * jnp.exp(cum)).astype(BF16)
        ksb = (kk * jnp.exp(last - cum)).astype(BF16)
        decay_row = jnp.exp(last)
        for hd in range(N_HEADS):
            hs = head_cols(hd)
            s_old = sout_ref[0, hd]
            parts = []
            for i in range(n_sub):
                r0 = i * sub
                r1 = r0 + sub
                width = min(rows, -(-r1 // HEAD) * HEAD)
                ref_row = cum[r0 - 1:r0, hs] if i > 0 else jnp.zeros((1, HEAD), F32)
                qd = (qf[r0:r1, hs] * jnp.exp(cum[r0:r1, hs] - ref_row)).astype(BF16)
                arg = ref_row - cum[0:width, hs]
                if width > r1:
                    arg = jnp.where(lax.broadcasted_iota(jnp.int32, (width, 1), 0) < r1, arg, 0.0)
                kdi = (kk[0:width, hs] * jnp.exp(arg)).astype(BF16)
                att = _mm_nt(qd, kdi)
                rr = lax.broadcasted_iota(jnp.int32, (sub, width), 0) + r0
                cc = lax.broadcasted_iota(jnp.int32, (sub, width), 1)
                att = jnp.where(cc <= rr, att, 0.0).astype(BF16)
                if width < rows:
                    att = jnp.concatenate([att, jnp.zeros((sub, rows - width), BF16)], axis=1)
                parts.append(att)
            att_full = jnp.concatenate(parts, axis=0) if n_sub > 1 else parts[0]
            if rows % HEAD == 0:
                o_h = _mm(jnp.concatenate([att_full, qib[:, hs]], axis=1),
                          jnp.concatenate([vb[:, hs], s_old.astype(BF16)], axis=0))
            else:
                o_h = _mm(att_full, vb[:, hs]) + _mm(qib[:, hs], s_old.astype(BF16))
            upd = _mm_tn(ksb[:, hs], vb[:, hs])
            sout_ref[0, hd] = _column_of(decay_row[:, hs]) * s_old + upd
            sold_s[hd] = s_old
            o_s[:, hs] = o_h

        worst = jnp.zeros((1, D), F32)
        for i in range(n_sub):
            start = cum[i * sub - 1:i * sub, :] if i > 0 else jnp.zeros((1, D), F32)
            worst = jnp.maximum(worst, start - cum[(i + 1) * sub - 1:(i + 1) * sub, :])
        out_of_range = jnp.max(worst) > SAFE_DECAY_RANGE

        @pl.when(out_of_range)
        def _exact_scores():
            row_i = lax.broadcasted_iota(jnp.int32, (rows, 1), 0)
            for hd in range(N_HEADS):
                hs = head_cols(hd)
                qib_h = (qf_s[:, hs] * jnp.exp(cum_s[:, hs])).astype(BF16)
                o_s[:, hs] = _mm(qib_h, sold_s[hd].astype(BF16))

                def block_body(b, carry, hs=hs):
                    r0 = pl.multiple_of(b * SUBLANES, SUBLANES)
                    rs = pl.ds(r0, SUBLANES)
                    q_b, k_b, c_b, v_b = qf_s[rs, hs], kk_s[rs, hs], cum_s[rs, hs], v_s[rs, hs]
                    prev_start = pl.multiple_of(jnp.maximum(r0 - SUBLANES, 0), SUBLANES)
                    before = cum_s[pl.ds(prev_start, SUBLANES), hs][SUBLANES - 1:SUBLANES, :]
                    ref_row = jnp.where(b > 0, before, 0.0)
                    qd = (q_b * jnp.exp(c_b - ref_row)).astype(BF16)
                    arg = jnp.minimum(ref_row - cum_s[:, hs], 0.0)
                    kd = jnp.where(row_i < r0, kk_s[:, hs] * jnp.exp(arg), 0.0).astype(BF16)
                    earlier = _mm(_mm_nt(qd, kd).astype(BF16), v_s[:, hs].astype(BF16))
                    o_s[rs, hs] = o_s[rs, hs] + earlier + _exact_block(q_b, k_b, c_b, v_b)
                    return carry

                lax.fori_loop(0, rows // SUBLANES, block_body, 0)

        finish([o_s[:, head_cols(hd)] for hd in range(N_HEADS)])

    def decode_state_step():
        def seq_body(g, carry):
            rsl = pl.ds(pl.multiple_of(g * SUBLANES, SUBLANES), SUBLANES)
            s_old = s0_ref[g, 0]
            c_b = cum_h[j, rsl, :]
            total_b = tot_h[j, rsl, :]
            qi = qf_h[j, rsl, :] * jnp.exp(c_b)
            ks = kk_h[j, rsl, :] * jnp.exp(total_b - c_b)
            oin_h[j, rsl, :] = _mm(qi.astype(BF16), s_old.astype(BF16))
            upd = _mm_tn(ks.astype(BF16), v_h[j, rsl, :].astype(BF16))
            decay = _column_of(jnp.exp(total_b[0:1, :]))
            sout_ref[g, 0] = decay * s_old + upd
            return carry

        lax.fori_loop(0, seqs, seq_body, 0, unroll=SUBLANES)

    def back_decode():
        ri = lax.broadcasted_iota(jnp.int32, (rows, rows), 0)
        ci = lax.broadcasted_iota(jnp.int32, (rows, rows), 1)
        causal = jnp.logical_and(ci <= ri, (ri // tokens) == (ci // tokens))
        for hd in range(N_HEADS):
            c_h = cum_h[hd]
            scores = _mm_nt((qf_h[hd] * jnp.exp(c_h)).astype(BF16), (kk_h[hd] * jnp.exp(-c_h)).astype(BF16))
            att = jnp.where(causal, scores, 0.0).astype(BF16)
            oi_h[hd] = _mm(att, v_h[hd].astype(BF16))

        worst = jnp.zeros((1, HEAD), F32)
        for hd in range(N_HEADS):
            worst = jnp.maximum(worst, jnp.max(-tot_h[hd], axis=0, keepdims=True))
        out_of_range = jnp.max(worst) > SAFE_DECAY_RANGE

        @pl.when(out_of_range)
        def _exact_scores():
            for hd in range(N_HEADS):
                def seq_body(g, carry, hd=hd):
                    rs = pl.ds(pl.multiple_of(g * SUBLANES, SUBLANES), SUBLANES)
                    oi_h[hd, rs, :] = _exact_block(qf_h[hd, rs, :], kk_h[hd, rs, :], cum_h[hd, rs, :], v_h[hd, rs, :])
                    return carry

                lax.fori_loop(0, seqs, seq_body, 0)

        finish([oin_h[hd] + oi_h[hd] for hd in range(N_HEADS)])

    if decode:
        @pl.when(j == 0)
        def _front_region():
            ubuf[:, 0:SUBLANES, :] = cext_ref[...]
            front()

        decode_state_step()

        @pl.when(j == N_HEADS - 1)
        def _back_region():
            back_decode()
    else:
        @pl.when(j == 0)
        def _init():
            ubuf[:, 0:SUBLANES, :] = cext_ref[...]
            hout_ref[...] = h0_ref[...]
            sout_ref[...] = s0_ref[...]

        front()

        @pl.when(j >= 0)
        def _back_region():
            back_prompt()


def _resident(shape):
    return pl.BlockSpec(shape, lambda i, j: (0,) * len(shape), pipeline_mode=pl.Buffered(1))


def _mixer(x, cext, h0, s0, pvec, w_in, w_ri, w_a, w_b, w_out, *, seqs, tokens, reset_first, sub, name):
    n_seq, length, _ = x.shape
    n_tiles = length // tokens
    decode = seqs > 1
    assert n_seq % seqs == 0 and length % tokens == 0 and tokens % SUBLANES == 0
    assert not decode or (n_tiles == 1 and tokens == SUBLANES)
    assert (seqs * tokens) % sub == 0
    rows = seqs * tokens
    kern = functools.partial(_mixer_kernel, seqs=seqs, tokens=tokens, n_tiles=n_tiles,
                             reset_first=reset_first, sub=sub)
    tile_f32 = pltpu.VMEM((rows, D), F32)
    by_head_f32 = pltpu.VMEM((N_HEADS, rows, HEAD), F32)
    common = [pltpu.VMEM((seqs, SUBLANES + tokens, D), F32),
              tile_f32, tile_f32, tile_f32]
    if decode:
        grid = (n_seq // seqs, N_HEADS)
        x_map = lambda i, j: (i, 0, 0)
        s_block = (seqs, 1, HEAD, HEAD)
        s_map = lambda i, j: (i, j, 0, 0)
        scratch = common + [by_head_f32] * 7
    else:
        grid = (n_seq, n_tiles)
        x_map = lambda i, j: (i, j, 0)
        s_block = (1, N_HEADS, HEAD, HEAD)
        s_map = lambda i, j: (i, 0, 0, 0)
        scratch = common + [tile_f32] * 6 + [pltpu.VMEM((N_HEADS, HEAD, HEAD), F32)]
    seq_map = lambda i, j: (i, 0, 0)
    return pl.pallas_call(
        kern,
        grid=grid,
        in_specs=[
            pl.BlockSpec((seqs, tokens, D), x_map),
            pl.BlockSpec((seqs, SUBLANES, D), seq_map),
            pl.BlockSpec((seqs, 1, D), seq_map),
            pl.BlockSpec(s_block, s_map),
            _resident((P_ROWS, D)),
            _resident(w_in.shape),
            _resident(w_ri.shape),
            _resident(w_a.shape),
            _resident(w_b.shape),
            _resident(w_out.shape),
        ],
        out_specs=[
            pl.BlockSpec((seqs, tokens, D), x_map),
            pl.BlockSpec((seqs, SUBLANES, D), seq_map),
            pl.BlockSpec((seqs, 1, D), seq_map),
            pl.BlockSpec(s_block, s_map),
        ],
        out_shape=[
            jax.ShapeDtypeStruct((n_seq, length, D), F32),
            jax.ShapeDtypeStruct((n_seq, SUBLANES, D), F32),
            jax.ShapeDtypeStruct((n_seq, 1, D), F32),
            jax.ShapeDtypeStruct((n_seq, N_HEADS, HEAD, HEAD), F32),
        ],
        scratch_shapes=scratch,
        compiler_params=pltpu.CompilerParams(
            dimension_semantics=("arbitrary", "arbitrary"),
            vmem_limit_bytes=VMEM_LIMIT_BYTES),
        name=name,
    )(x, cext, h0, s0, pvec, w_in, w_ri, w_a, w_b, w_out)


def _mlp_kernel(x_ref, g_ref, wup_ref, wdn_ref, o_ref):
    x = x_ref[...]
    hn = _rms(x, g_ref[0:1, :]).astype(BF16)
    acc = jnp.zeros(x.shape, F32)
    for c in range(D_FF // D):
        t = _mm(hn, _unpack_rows(wup_ref[:, c * D:(c + 1) * D]))
        t = jnp.square(jnp.maximum(t, 0.0)).astype(BF16)
        acc = acc + _mm(t, _unpack_rows(wdn_ref[c * (D // 2):(c + 1) * (D // 2), :]))
    o_ref[...] = x + _rms(acc, g_ref[1:2, :])


def _mlp(x2d, gains, w_up, w_down, *, tile, name):
    n = x2d.shape[0]
    assert n % tile == 0
    return pl.pallas_call(
        _mlp_kernel,
        grid=(n // tile,),
        in_specs=[
            pl.BlockSpec((tile, D), lambda i: (i, 0)),
            pl.BlockSpec((2, D), lambda i: (0, 0), pipeline_mode=pl.Buffered(1)),
            pl.BlockSpec(w_up.shape, lambda i: (0, 0), pipeline_mode=pl.Buffered(1)),
            pl.BlockSpec(w_down.shape, lambda i: (0, 0), pipeline_mode=pl.Buffered(1)),
        ],
        out_specs=pl.BlockSpec((tile, D), lambda i: (i, 0)),
        out_shape=jax.ShapeDtypeStruct((n, D), F32),
        compiler_params=pltpu.CompilerParams(
            dimension_semantics=("arbitrary",),
            vmem_limit_bytes=VMEM_LIMIT_BYTES),
        name=name,
    )(x2d, gains, w_up, w_down)


def kernel(x_prompt, x_sample, state_conv, state_rglru, state_hgrn, meta_tokens, norm_gains, w_in,
           conv_w, conv_b, rg_w, rg_b, ig_w, ig_b, lru_lambda, hgrn_lb, hgrn_gnorm,
           w_branch_a, w_branch_b, w_out, w_up, w_down):
    bp, seq, _ = x_prompt.shape
    bs, dec_seq, _ = x_sample.shape
    layer = 0
    pvec = jnp.concatenate([
        norm_gains[layer], conv_w[layer], conv_b[layer][None], rg_b[layer][None], ig_b[layer][None],
        lru_lambda[layer][None], hgrn_lb[layer:layer + 2], hgrn_gnorm[layer][None],
        jnp.zeros((1, D), F32)], axis=0).astype(F32)
    w_in_b = _pack_rows(w_in[layer], name="pack_w_in")
    w_gates = jnp.concatenate([rg_w[layer], ig_w[layer]], axis=-1)
    w_ri = _pack_rows(w_gates.reshape(N_HEADS * HEAD, 2 * HEAD), name="pack_w_gates")
    w_ri = w_ri.reshape(N_HEADS, HEAD // 2, 2 * HEAD)
    w_a = _pack_rows(w_branch_a[layer], name="pack_w_a")
    w_b = _pack_rows(w_branch_b[layer], name="pack_w_b")
    w_o = _pack_rows(w_out[layer], name="pack_w_out")
    w_u = _pack_rows(w_up[layer], name="pack_w_up")
    w_d = _pack_rows(w_down[layer], name="pack_w_down")
    weights = (pvec, w_in_b, w_ri, w_a, w_b, w_o)

    zeros_c = jnp.zeros((1, SUBLANES, D), F32)
    zeros_h = jnp.zeros((1, 1, D), F32)
    zeros_s = jnp.zeros((1, N_HEADS, HEAD, HEAD), F32)
    _, c_m, h_m, s_m = _mixer(meta_tokens[None].astype(F32), zeros_c, zeros_h, zeros_s, *weights,
                              seqs=1, tokens=N_META, reset_first=True, sub=N_META, name="mixer_meta")

    x1_p, c_p, h_p, s_p = _mixer(
        x_prompt,
        jnp.broadcast_to(c_m, (bp, SUBLANES, D)),
        jnp.broadcast_to(h_m, (bp, 1, D)),
        jnp.broadcast_to(s_m, (bp, N_HEADS, HEAD, HEAD)),
        *weights, seqs=1, tokens=256, reset_first=False, sub=64, name="mixer_prompt")

    cext_s = jnp.pad(state_conv[layer], ((0, 0), (SUBLANES - 3, 0), (0, 0)))
    x1_s, c_s, h_s, s_s = _mixer(
        x_sample, cext_s, state_rglru[layer][:, None, :], state_hgrn[layer],
        *weights, seqs=32, tokens=dec_seq, reset_first=False, sub=8, name="mixer_sample")

    gains_mlp = norm_gains[layer, 2:4]
    y_p = _mlp(x1_p.reshape(bp * seq, D), gains_mlp, w_u, w_d, tile=512, name="mlp_prompt")
    y_s = _mlp(x1_s.reshape(bs * dec_seq, D), gains_mlp, w_u, w_d, tile=512, name="mlp_sample")

    return (y_p.reshape(bp, seq, D), y_s.reshape(bs, dec_seq, D),
            c_p[:, SUBLANES - 3:, :][None], h_p[:, 0, :][None], s_p[None],
            c_s[:, SUBLANES - 3:, :][None], h_s[:, 0, :][None], s_s[None])
```

```python
import functools

import jax
import jax.numpy as jnp
from jax import lax
from jax.experimental import pallas as pl
from jax.experimental.pallas import tpu as pltpu

D = 1024
N_HEADS = 8
HEAD = 128
N_META = 16
LRU_C = 8.0
EPS = 1e-6
D_FF = 4096
SUBLANES = 8
F32 = jnp.float32
BF16 = jnp.bfloat16

P_GAIN, P_CONV_W, P_CONV_B, P_RG_B, P_IG_B, P_LAMBDA, P_LB, P_GNORM, P_ROWS = 0, 4, 8, 9, 10, 11, 12, 14, 16

C_U, C_GATE, C_Q, C_F, C_I, C_OG, C_MA, C_MB = range(8)

VMEM_LIMIT_BYTES = 56 * 1024 * 1024
PACK_BLOCK_ELEMS = 1 << 20
SAFE_DECAY_RANGE = 64.0


def _rms(x, g):
    ms = jnp.mean(x * x, axis=-1, keepdims=True)
    return x * lax.rsqrt(ms + EPS) * g


def _sigmoid_of_twice(h):
    return 0.5 * jnp.tanh(h) + 0.5


def _silu_of_twice(h):
    return h + h * jnp.tanh(h)


_GELU_C1 = 2.0 * 0.7978845608028654
_GELU_C2 = 8.0 * 0.7978845608028654 * 0.044715


def _gelu_tanh_of_twice(h):
    return h + h * jnp.tanh(h * (_GELU_C1 + _GELU_C2 * (h * h)))


def _mm(a, b):
    return jnp.dot(a, b, preferred_element_type=F32)


def _mm_nt(a, b):
    return lax.dot_general(a, b, (((1,), (1,)), ((), ())), preferred_element_type=F32)


def _mm_tn(a, b):
    return lax.dot_general(a, b, (((0,), (0,)), ((), ())), preferred_element_type=F32)


def _split3(x):
    hi = x.astype(BF16)
    r1 = x - hi.astype(F32)
    mid = r1.astype(BF16)
    lo = (r1 - mid.astype(F32)).astype(BF16)
    return hi, mid, lo


def _mask_matmul(mask_bf, x):
    hi, mid, lo = _split3(x)
    return _mm(mask_bf, hi) + _mm(mask_bf, mid) + _mm(mask_bf, lo)


def _pack_kernel(w_ref, scale_ref, o_ref):
    o_ref[...] = pltpu.bitcast((w_ref[...] * scale_ref[...]).astype(BF16), jnp.int32)


def _pack_rows(w, col_scale, *, name):
    k, n = w.shape
    tile = min(k, PACK_BLOCK_ELEMS // n)
    assert k % tile == 0 and tile % (2 * SUBLANES) == 0
    return pl.pallas_call(
        _pack_kernel,
        grid=(k // tile,),
        in_specs=[pl.BlockSpec((tile, n), lambda i: (i, 0)),
                  pl.BlockSpec((1, n), lambda i: (0, 0))],
        out_specs=pl.BlockSpec((tile // 2, n), lambda i: (i, 0)),
        out_shape=jax.ShapeDtypeStruct((k // 2, n), jnp.int32),
        compiler_params=pltpu.CompilerParams(dimension_semantics=("arbitrary",)),
        name=name,
    )(w, col_scale)


def _unpack_rows(words):
    return pltpu.bitcast(words, BF16)


def _exact_block(q_b, k_b, c_b, v_b):
    t_idx = lax.broadcasted_iota(jnp.int32, (SUBLANES, 1), 0)
    acc = jnp.zeros((SUBLANES, HEAD), F32)
    for s in range(SUBLANES):
        decay = jnp.exp(jnp.minimum(c_b - c_b[s:s + 1, :], 0.0))
        score = jnp.sum(q_b * k_b[s:s + 1, :] * decay, axis=-1, keepdims=True)
        acc = acc + jnp.where(t_idx >= s, score, 0.0) * v_b[s:s + 1, :]
    return acc


def _column_of(row):
    return jnp.transpose(jnp.broadcast_to(row, (HEAD, HEAD)))


def _mixer_kernel(x_ref, cext_ref, h0_ref, s0_ref, pvec_ref, win_ref, wri_ref, wa_ref, wb_ref, wout_ref,
                  x1_ref, cout_ref, hout_ref, sout_ref, *scratch,
                  seqs, tokens, n_tiles, reset_first, sub):
    j = pl.program_id(1)
    rows = seqs * tokens
    decode = seqs > 1
    if decode:
        ubuf, ga_s, sgb_s, sog_s, qf_h, kk_h, cum_h, v_h, tot_h, oin_h, oi_h = scratch
    else:
        ubuf, ga_s, sgb_s, sog_s, hbuf, qf_s, kk_s, cum_s, v_s, o_s, sold_s = scratch

    pv = pvec_ref[...]

    def prow(r):
        return pv[r:r + 1, :]

    def wcol(g):
        return _unpack_rows(win_ref[:, g * D:(g + 1) * D])

    def head_cols(hd):
        return slice(hd * HEAD, (hd + 1) * HEAD)

    def front():
        x = x_ref[...].reshape(rows, D)
        xn = _rms(x, prow(P_GAIN + 0)).astype(BF16)
        row_id = lax.broadcasted_iota(jnp.int32, (rows, 1), 0)

        u = _mm(xn, wcol(C_U))
        ubuf[:, SUBLANES:SUBLANES + tokens, :] = u.reshape(seqs, tokens, D)
        uc = prow(P_CONV_B) + prow(P_CONV_W + 3) * u
        for k in range(1, 4):
            shifted = ubuf[:, SUBLANES - k:SUBLANES - k + tokens, :].reshape(rows, D)
            uc = uc + prow(P_CONV_W + 3 - k) * shifted
        tail = ubuf[:, tokens:tokens + SUBLANES, :]
        ubuf[:, 0:SUBLANES, :] = tail
        if decode:
            cout_ref[...] = tail
        else:
            @pl.when(j == n_tiles - 1)
            def _conv_out():
                cout_ref[...] = tail

        ucb = uc.astype(BF16)
        r_parts, i_parts = [], []
        for b in range(N_HEADS):
            ri = _mm(ucb[:, head_cols(b)], _unpack_rows(wri_ref[b]))
            r_parts.append(ri[:, :HEAD])
            i_parts.append(ri[:, HEAD:])
        r_gate = _sigmoid_of_twice(jnp.concatenate(r_parts, axis=1) + prow(P_RG_B))
        i_gate = _sigmoid_of_twice(jnp.concatenate(i_parts, axis=1) + prow(P_IG_B))
        lam = prow(P_LAMBDA)
        softplus_neg_lam = jnp.maximum(-lam, 0.0) + jnp.log1p(jnp.exp(-jnp.abs(lam)))
        log_a = (-LRU_C) * r_gate * softplus_neg_lam
        a_cum = jnp.exp(log_a)
        th = jnp.tanh(log_a)
        sq = -2.0 * th / (1.0 - th)
        mult = jnp.where(sq > 0.0, sq * lax.rsqrt(sq), 0.0)
        if reset_first:
            mult = jnp.where(jnp.logical_and(row_id == 0, j == 0), 1.0, mult)
        b_cum = mult * i_gate * uc

        a_cum = a_cum.reshape(rows // SUBLANES, SUBLANES, D)
        b_cum = b_cum.reshape(rows // SUBLANES, SUBLANES, D)
        sublane = lax.broadcasted_iota(jnp.int32, (1, SUBLANES, 1), 1)
        for s in (1, 2, 4):
            keep = sublane >= s
            a_prev = jnp.where(keep, pltpu.roll(a_cum, s, 1), 1.0)
            b_prev = jnp.where(keep, pltpu.roll(b_cum, s, 1), 0.0)
            b_cum = a_cum * b_prev + b_cum
            a_cum = a_cum * a_prev
        a_cum = a_cum.reshape(rows, D)
        b_cum = b_cum.reshape(rows, D)

        if decode:
            h_in = jnp.broadcast_to(h0_ref[...], (seqs, SUBLANES, D)).reshape(rows, D)
            h_all = a_cum * h_in + b_cum
            hout_ref[...] = h_all.reshape(seqs, SUBLANES, D)[:, SUBLANES - 1:SUBLANES, :]
        else:
            h = hout_ref[0]
            for g in range(rows // SUBLANES):
                sl = slice(g * SUBLANES, (g + 1) * SUBLANES)
                hg = a_cum[sl] * h + b_cum[sl]
                hbuf[sl, :] = hg
                h = hg[SUBLANES - 1:SUBLANES, :]
            hout_ref[0] = h
            h_all = hbuf[...]

        ya = (h_all * _gelu_tanh_of_twice(_mm(xn, wcol(C_GATE)))).astype(BF16)
        ga_s[...] = _sigmoid_of_twice(_mm(xn, wcol(C_MA))) * _mm(ya, _unpack_rows(wa_ref[...]))
        sgb_s[...] = _sigmoid_of_twice(_mm(xn, wcol(C_MB)))
        sog_s[...] = _silu_of_twice(_mm(xn, wcol(C_OG)))

        qf = _silu_of_twice(_mm(xn, wcol(C_Q)))
        lb_raw = pv[P_LB:P_LB + 2, :]
        lb_exp = jnp.exp(lb_raw - jnp.max(lb_raw, axis=0, keepdims=True))
        lb = lb_exp[0:1, :] / jnp.sum(lb_exp, axis=0, keepdims=True)
        fg = lb + (1.0 - lb) / (1.0 + jnp.exp(-_mm(xn, wcol(C_F))))
        logf = jnp.log(fg)
        kk = 1.0 - fg
        v = _mm(xn, wcol(C_I))

        ri2 = lax.broadcasted_iota(jnp.int32, (rows, rows), 0)
        ci2 = lax.broadcasted_iota(jnp.int32, (rows, rows), 1)
        if decode:
            same_seq = (ri2 // tokens) == (ci2 // tokens)
            total = _mask_matmul(same_seq.astype(BF16), logf)
            cum = _mask_matmul(jnp.logical_and(ci2 <= ri2, same_seq).astype(BF16), logf)
            for hd in range(N_HEADS):
                hs = head_cols(hd)
                v_h[hd] = v[:, hs]
                tot_h[hd] = total[:, hs]
                qf_h[hd] = qf[:, hs]
                kk_h[hd] = kk[:, hs]
                cum_h[hd] = cum[:, hs]
        else:
            qf_s[...] = qf
            kk_s[...] = kk
            v_s[...] = v
            cum_s[...] = _mask_matmul((ci2 <= ri2).astype(BF16), logf)

    def head_norm(o_h):
        ms = jnp.mean(o_h * o_h, axis=-1, keepdims=True)
        return o_h * lax.rsqrt(ms + EPS)

    def finish(o_heads):
        o_n = jnp.concatenate([head_norm(o_h) for o_h in o_heads], axis=1) * prow(P_GNORM)
        yb = (o_n * sog_s[...]).astype(BF16)
        mixed = ga_s[...] + sgb_s[...] * _mm(yb, _unpack_rows(wb_ref[...]))
        z = _mm(mixed.astype(BF16), _unpack_rows(wout_ref[...]))
        x1 = x_ref[...].reshape(rows, D) + _rms(z, prow(P_GAIN + 1))
        x1_ref[...] = x1.reshape(seqs, tokens, D)

    def back_prompt():
        qf = qf_s[...]
        kk = kk_s[...]
        cum = cum_s[...]
        vb = v_s[...].astype(BF16)
        n_sub = rows // sub
        last = cum[rows - 1:rows, :]
        qib = (qf * jnp.exp(cum)).astype(BF16)
        ksb = (kk * jnp.exp(last - cum)).astype(BF16)
        decay_row = jnp.exp(last)
        for hd in range(N_HEADS):
            hs = head_cols(hd)
            s_old = sout_ref[0, hd]
            parts = []
            for i in range(n_sub):
                r0 = i * sub
                r1 = r0 + sub
                width = min(rows, -(-r1 // HEAD) * HEAD)
                ref_row = cum[r0 - 1:r0, hs] if i > 0 else jnp.zeros((1, HEAD), F32)
                qd = (qf[r0:r1, hs] * jnp.exp(cum[r0:r1, hs] - ref_row)).astype(BF16)
                arg = ref_row - cum[0:width, hs]
                if width > r1:
                    arg = jnp.where(lax.broadcasted_iota(jnp.int32, (width, 1), 0) < r1, arg, 0.0)
                kdi = (kk[0:width, hs] * jnp.exp(arg)).astype(BF16)
                att = _mm_nt(qd, kdi)
                rr = lax.broadcasted_iota(jnp.int32, (sub, width), 0) + r0
                cc = lax.broadcasted_iota(jnp.int32, (sub, width), 1)
                att = jnp.where(cc <= rr, att, 0.0).astype(BF16)
                if width < rows:
                    att = jnp.concatenate([att, jnp.zeros((sub, rows - width), BF16)], axis=1)
                parts.append(att)
            att_full = jnp.concatenate(parts, axis=0) if n_sub > 1 else parts[0]
            if rows % HEAD == 0:
                o_h = _mm(jnp.concatenate([att_full, qib[:, hs]], axis=1),
                          jnp.concatenate([vb[:, hs], s_old.astype(BF16)], axis=0))
            else:
                o_h = _mm(att_full, vb[:, hs]) + _mm(qib[:, hs], s_old.astype(BF16))
            upd = _mm_tn(ksb[:, hs], vb[:, hs])
            sout_ref[0, hd] = _column_of(decay_row[:, hs]) * s_old + upd
            sold_s[hd] = s_old
            o_s[:, hs] = o_h

        worst = jnp.zeros((1, D), F32)
        for i in range(n_sub):
            start = cum[i * sub - 1:i * sub, :] if i > 0 else jnp.zeros((1, D), F32)
            worst = jnp.maximum(worst, start - cum[(i + 1) * sub - 1:(i + 1) * sub, :])
        out_of_range = jnp.max(worst) > SAFE_DECAY_RANGE

        @pl.when(out_of_range)
        def _exact_scores():
            row_i = lax.broadcasted_iota(jnp.int32, (rows, 1), 0)
            for hd in range(N_HEADS):
                hs = head_cols(hd)
                qib_h = (qf_s[:, hs] * jnp.exp(cum_s[:, hs])).astype(BF16)
                o_s[:, hs] = _mm(qib_h, sold_s[hd].astype(BF16))

                def block_body(b, carry, hs=hs):
                    r0 = pl.multiple_of(b * SUBLANES, SUBLANES)
                    rs = pl.ds(r0, SUBLANES)
                    q_b, k_b, c_b, v_b = qf_s[rs, hs], kk_s[rs, hs], cum_s[rs, hs], v_s[rs, hs]
                    prev_start = pl.multiple_of(jnp.maximum(r0 - SUBLANES, 0), SUBLANES)
                    before = cum_s[pl.ds(prev_start, SUBLANES), hs][SUBLANES - 1:SUBLANES, :]
                    ref_row = jnp.where(b > 0, before, 0.0)
                    qd = (q_b * jnp.exp(c_b - ref_row)).astype(BF16)
                    arg = jnp.minimum(ref_row - cum_s[:, hs], 0.0)
                    kd = jnp.where(row_i < r0, kk_s[:, hs] * jnp.exp(arg), 0.0).astype(BF16)
                    earlier = _mm(_mm_nt(qd, kd).astype(BF16), v_s[:, hs].astype(BF16))
                    o_s[rs, hs] = o_s[rs, hs] + earlier + _exact_block(q_b, k_b, c_b, v_b)
                    return carry

                lax.fori_loop(0, rows // SUBLANES, block_body, 0)

        finish([o_s[:, head_cols(hd)] for hd in range(N_HEADS)])

    def decode_state_step():
        def seq_body(g, carry):
            rsl = pl.ds(pl.multiple_of(g * SUBLANES, SUBLANES), SUBLANES)
            s_old = s0_ref[g, 0]
            c_b = cum_h[j, rsl, :]
            total_b = tot_h[j, rsl, :]
            qi = qf_h[j, rsl, :] * jnp.exp(c_b)
            ks = kk_h[j, rsl, :] * jnp.exp(total_b - c_b)
            oin_h[j, rsl, :] = _mm(qi.astype(BF16), s_old.astype(BF16))
            upd = _mm_tn(ks.astype(BF16), v_h[j, rsl, :].astype(BF16))
            decay = _column_of(jnp.exp(total_b[0:1, :]))
            sout_ref[g, 0] = decay * s_old + upd
            return carry

        lax.fori_loop(0, seqs, seq_body, 0, unroll=SUBLANES)

    def back_decode():
        ri = lax.broadcasted_iota(jnp.int32, (rows, rows), 0)
        ci = lax.broadcasted_iota(jnp.int32, (rows, rows), 1)
        causal = jnp.logical_and(ci <= ri, (ri // tokens) == (ci // tokens))
        for hd in range(N_HEADS):
            c_h = cum_h[hd]
            scores = _mm_nt((qf_h[hd] * jnp.exp(c_h)).astype(BF16), (kk_h[hd] * jnp.exp(-c_h)).astype(BF16))
            att = jnp.where(causal, scores, 0.0).astype(BF16)
            oi_h[hd] = _mm(att, v_h[hd].astype(BF16))

        worst = jnp.zeros((1, HEAD), F32)
        for hd in range(N_HEADS):
            worst = jnp.maximum(worst, jnp.max(-tot_h[hd], axis=0, keepdims=True))
        out_of_range = jnp.max(worst) > SAFE_DECAY_RANGE

        @pl.when(out_of_range)
        def _exact_scores():
            for hd in range(N_HEADS):
                def seq_body(g, carry, hd=hd):
                    rs = pl.ds(pl.multiple_of(g * SUBLANES, SUBLANES), SUBLANES)
                    oi_h[hd, rs, :] = _exact_block(qf_h[hd, rs, :], kk_h[hd, rs, :], cum_h[hd, rs, :], v_h[hd, rs, :])
                    return carry

                lax.fori_loop(0, seqs, seq_body, 0)

        finish([oin_h[hd] + oi_h[hd] for hd in range(N_HEADS)])

    if decode:
        @pl.when(j == 0)
        def _front_region():
            ubuf[:, 0:SUBLANES, :] = cext_ref[...]
            front()

        decode_state_step()

        @pl.when(j == N_HEADS - 1)
        def _back_region():
            back_decode()
    else:
        @pl.when(j == 0)
        def _init():
            ubuf[:, 0:SUBLANES, :] = cext_ref[...]
            hout_ref[...] = h0_ref[...]
            sout_ref[...] = s0_ref[...]

        front()

        @pl.when(j >= 0)
        def _back_region():
            back_prompt()


def _resident(shape):
    return pl.BlockSpec(shape, lambda i, j: (0,) * len(shape), pipeline_mode=pl.Buffered(1))


def _mixer(x, cext, h0, s0, pvec, w_in, w_ri, w_a, w_b, w_out, *, seqs, tokens, reset_first, sub, name):
    n_seq, length, _ = x.shape
    n_tiles = length // tokens
    decode = seqs > 1
    assert n_seq % seqs == 0 and length % tokens == 0 and tokens % SUBLANES == 0
    assert not decode or (n_tiles == 1 and tokens == SUBLANES)
    assert (seqs * tokens) % sub == 0
    rows = seqs * tokens
    kern = functools.partial(_mixer_kernel, seqs=seqs, tokens=tokens, n_tiles=n_tiles,
                             reset_first=reset_first, sub=sub)
    tile_f32 = pltpu.VMEM((rows, D), F32)
    by_head_f32 = pltpu.VMEM((N_HEADS, rows, HEAD), F32)
    common = [pltpu.VMEM((seqs, SUBLANES + tokens, D), F32),
              tile_f32, tile_f32, tile_f32]
    if decode:
        grid = (n_seq // seqs, N_HEADS)
        x_map = lambda i, j: (i, 0, 0)
        s_block = (seqs, 1, HEAD, HEAD)
        s_map = lambda i, j: (i, j, 0, 0)
        scratch = common + [by_head_f32] * 7
    else:
        grid = (n_seq, n_tiles)
        x_map = lambda i, j: (i, j, 0)
        s_block = (1, N_HEADS, HEAD, HEAD)
        s_map = lambda i, j: (i, 0, 0, 0)
        scratch = common + [tile_f32] * 6 + [pltpu.VMEM((N_HEADS, HEAD, HEAD), F32)]
    seq_map = lambda i, j: (i, 0, 0)
    return pl.pallas_call(
        kern,
        grid=grid,
        in_specs=[
            pl.BlockSpec((seqs, tokens, D), x_map),
            pl.BlockSpec((seqs, SUBLANES, D), seq_map),
            pl.BlockSpec((seqs, 1, D), seq_map),
            pl.BlockSpec(s_block, s_map),
            _resident((P_ROWS, D)),
            _resident(w_in.shape),
            _resident(w_ri.shape),
            _resident(w_a.shape),
            _resident(w_b.shape),
            _resident(w_out.shape),
        ],
        out_specs=[
            pl.BlockSpec((seqs, tokens, D), x_map),
            pl.BlockSpec((seqs, SUBLANES, D), seq_map),
            pl.BlockSpec((seqs, 1, D), seq_map),
            pl.BlockSpec(s_block, s_map),
        ],
        out_shape=[
            jax.ShapeDtypeStruct((n_seq, length, D), F32),
            jax.ShapeDtypeStruct((n_seq, SUBLANES, D), F32),
            jax.ShapeDtypeStruct((n_seq, 1, D), F32),
            jax.ShapeDtypeStruct((n_seq, N_HEADS, HEAD, HEAD), F32),
        ],
        scratch_shapes=scratch,
        compiler_params=pltpu.CompilerParams(
            dimension_semantics=("arbitrary", "arbitrary"),
            vmem_limit_bytes=VMEM_LIMIT_BYTES),
        name=name,
    )(x, cext, h0, s0, pvec, w_in, w_ri, w_a, w_b, w_out)


def _mlp_kernel(x_ref, g_ref, wup_ref, wdn_ref, o_ref):
    x = x_ref[...]
    hn = _rms(x, g_ref[0:1, :]).astype(BF16)
    acc = jnp.zeros(x.shape, F32)
    for c in range(D_FF // D):
        t = _mm(hn, _unpack_rows(wup_ref[:, c * D:(c + 1) * D]))
        t = jnp.square(jnp.maximum(t, 0.0)).astype(BF16)
        acc = acc + _mm(t, _unpack_rows(wdn_ref[c * (D // 2):(c + 1) * (D // 2), :]))
    o_ref[...] = x + _rms(acc, g_ref[1:2, :])


def _mlp(x2d, gains, w_up, w_down, *, tile, name):
    n = x2d.shape[0]
    assert n % tile == 0
    return pl.pallas_call(
        _mlp_kernel,
        grid=(n // tile,),
        in_specs=[
            pl.BlockSpec((tile, D), lambda i: (i, 0)),
            pl.BlockSpec((2, D), lambda i: (0, 0), pipeline_mode=pl.Buffered(1)),
            pl.BlockSpec(w_up.shape, lambda i: (0, 0), pipeline_mode=pl.Buffered(1)),
            pl.BlockSpec(w_down.shape, lambda i: (0, 0), pipeline_mode=pl.Buffered(1)),
        ],
        out_specs=pl.BlockSpec((tile, D), lambda i: (i, 0)),
        out_shape=jax.ShapeDtypeStruct((n, D), F32),
        compiler_params=pltpu.CompilerParams(
            dimension_semantics=("arbitrary",),
            vmem_limit_bytes=VMEM_LIMIT_BYTES),
        name=name,
    )(x2d, gains, w_up, w_down)


def kernel(x_prompt, x_sample, state_conv, state_rglru, state_hgrn, meta_tokens, norm_gains, w_in,
           conv_w, conv_b, rg_w, rg_b, ig_w, ig_b, lru_lambda, hgrn_lb, hgrn_gnorm,
           w_branch_a, w_branch_b, w_out, w_up, w_down):
    bp, seq, _ = x_prompt.shape
    bs, dec_seq, _ = x_sample.shape
    layer = 0
    pvec = jnp.concatenate([
        norm_gains[layer], conv_w[layer], conv_b[layer][None], 0.5 * rg_b[layer][None], 0.5 * ig_b[layer][None],
        lru_lambda[layer][None], hgrn_lb[layer:layer + 2], hgrn_gnorm[layer][None],
        jnp.zeros((1, D), F32)], axis=0).astype(F32)
    halved = (C_GATE, C_Q, C_OG, C_MA, C_MB)
    scale_in = jnp.concatenate([jnp.full((1, D), 0.5 if g in halved else 1.0, F32) for g in range(8)], axis=1)

    def ones(n):
        return jnp.ones((1, n), F32)

    w_in_b = _pack_rows(w_in[layer], scale_in, name="pack_w_in")
    w_gates = jnp.concatenate([rg_w[layer], ig_w[layer]], axis=-1)
    w_ri = _pack_rows(w_gates.reshape(N_HEADS * HEAD, 2 * HEAD), 0.5 * ones(2 * HEAD), name="pack_w_gates")
    w_ri = w_ri.reshape(N_HEADS, HEAD // 2, 2 * HEAD)
    w_a = _pack_rows(w_branch_a[layer], ones(D), name="pack_w_a")
    w_b = _pack_rows(w_branch_b[layer], ones(D), name="pack_w_b")
    w_o = _pack_rows(w_out[layer], ones(D), name="pack_w_out")
    w_u = _pack_rows(w_up[layer], ones(D_FF), name="pack_w_up")
    w_d = _pack_rows(w_down[layer], ones(D), name="pack_w_down")
    weights = (pvec, w_in_b, w_ri, w_a, w_b, w_o)

    zeros_c = jnp.zeros((1, SUBLANES, D), F32)
    zeros_h = jnp.zeros((1, 1, D), F32)
    zeros_s = jnp.zeros((1, N_HEADS, HEAD, HEAD), F32)
    _, c_m, h_m, s_m = _mixer(meta_tokens[None].astype(F32), zeros_c, zeros_h, zeros_s, *weights,
                              seqs=1, tokens=N_META, reset_first=True, sub=N_META, name="mixer_meta")

    x1_p, c_p, h_p, s_p = _mixer(
        x_prompt,
        jnp.broadcast_to(c_m, (bp, SUBLANES, D)),
        jnp.broadcast_to(h_m, (bp, 1, D)),
        jnp.broadcast_to(s_m, (bp, N_HEADS, HEAD, HEAD)),
        *weights, seqs=1, tokens=256, reset_first=False, sub=64, name="mixer_prompt")

    cext_s = jnp.pad(state_conv[layer], ((0, 0), (SUBLANES - 3, 0), (0, 0)))
    x1_s, c_s, h_s, s_s = _mixer(
        x_sample, cext_s, state_rglru[layer][:, None, :], state_hgrn[layer],
        *weights, seqs=32, tokens=dec_seq, reset_first=False, sub=8, name="mixer_sample")

    gains_mlp = norm_gains[layer, 2:4]
    y_p = _mlp(x1_p.reshape(bp * seq, D), gains_mlp, w_u, w_d, tile=512, name="mlp_prompt")
    y_s = _mlp(x1_s.reshape(bs * dec_seq, D), gains_mlp, w_u, w_d, tile=512, name="mlp_sample")

    return (y_p.reshape(bp, seq, D), y_s.reshape(bs, dec_seq, D),
            c_p[:, SUBLANES - 3:, :][None], h_p[:, 0, :][None], s_p[None],
            c_s[:, SUBLANES - 3:, :][None], h_s[:, 0, :][None], s_s[None])
```

```python
import functools

import jax
import jax.numpy as jnp
from jax import lax
from jax.experimental import pallas as pl
from jax.experimental.pallas import tpu as pltpu

D = 1024
N_HEADS = 8
HEAD = 128
N_META = 16
LRU_C = 8.0
EPS = 1e-6
D_FF = 4096
SUBLANES = 8
F32 = jnp.float32
BF16 = jnp.bfloat16

P_GAIN, P_CONV_W, P_CONV_B, P_RG_B, P_IG_B, P_LAMBDA, P_LB, P_GNORM, P_ROWS = 0, 4, 8, 9, 10, 11, 12, 14, 16

C_U, C_GATE, C_Q, C_F, C_I, C_OG, C_MA, C_MB = range(8)

VMEM_LIMIT_BYTES = 60 * 1024 * 1024
PACK_BLOCK_ELEMS = 1 << 20
SAFE_DECAY_RANGE = 64.0


def _rms(x, g):
    ms = jnp.mean(x * x, axis=-1, keepdims=True)
    return x * lax.rsqrt(ms + EPS) * g


def _sigmoid_of_twice(h):
    return 0.5 * jnp.tanh(h) + 0.5


def _silu_of_twice(h):
    return h + h * jnp.tanh(h)


_GELU_C1 = 2.0 * 0.7978845608028654
_GELU_C2 = 8.0 * 0.7978845608028654 * 0.044715


def _gelu_tanh_of_twice(h):
    return h + h * jnp.tanh(h * (_GELU_C1 + _GELU_C2 * (h * h)))


def _mm(a, b):
    return jnp.dot(a, b, preferred_element_type=F32)


def _mm_nt(a, b):
    return lax.dot_general(a, b, (((1,), (1,)), ((), ())), preferred_element_type=F32)


def _mm_tn(a, b):
    return lax.dot_general(a, b, (((0,), (0,)), ((), ())), preferred_element_type=F32)


def _mask_matmul(mask_bf, x):
    hi = x.astype(BF16)
    lo = (x - hi.astype(F32)).astype(BF16)
    return _mm(mask_bf, hi) + _mm(mask_bf, lo)


def _pack_kernel(w_ref, scale_ref, o_ref):
    o_ref[...] = pltpu.bitcast((w_ref[...] * scale_ref[...]).astype(BF16), jnp.int32)


def _pack_rows(w, col_scale, *, name):
    k, n = w.shape
    tile = min(k, PACK_BLOCK_ELEMS // n)
    assert k % tile == 0 and tile % (2 * SUBLANES) == 0
    return pl.pallas_call(
        _pack_kernel,
        grid=(k // tile,),
        in_specs=[pl.BlockSpec((tile, n), lambda i: (i, 0)),
                  pl.BlockSpec((1, n), lambda i: (0, 0))],
        out_specs=pl.BlockSpec((tile // 2, n), lambda i: (i, 0)),
        out_shape=jax.ShapeDtypeStruct((k // 2, n), jnp.int32),
        compiler_params=pltpu.CompilerParams(dimension_semantics=("arbitrary",)),
        name=name,
    )(w, col_scale)


def _unpack_rows(words):
    return pltpu.bitcast(words, BF16)


def _exact_block(q_b, k_b, c_b, v_b):
    t_idx = lax.broadcasted_iota(jnp.int32, (SUBLANES, 1), 0)
    acc = jnp.zeros((SUBLANES, HEAD), F32)
    for s in range(SUBLANES):
        decay = jnp.exp(jnp.minimum(c_b - c_b[s:s + 1, :], 0.0))
        score = jnp.sum(q_b * k_b[s:s + 1, :] * decay, axis=-1, keepdims=True)
        acc = acc + jnp.where(t_idx >= s, score, 0.0) * v_b[s:s + 1, :]
    return acc


def _column_of(row):
    return jnp.transpose(jnp.broadcast_to(row, (HEAD, HEAD)))


def _mixer_kernel(x_ref, cext_ref, h0_ref, s0_ref, pvec_ref, win_ref, wri_ref, wa_ref, wb_ref, wout_ref,
                  x1_ref, cout_ref, hout_ref, sout_ref, *scratch,
                  seqs, tokens, n_tiles, reset_first, sub):
    j = pl.program_id(1)
    rows = seqs * tokens
    decode = seqs > 1
    if decode:
        ubuf, ga_s, sgb_s, sog_s, qf_h, kk_h, cum_h, v_h, tot_h, oin_h, oi_h = scratch
    else:
        ubuf, ga_s, sgb_s, sog_s, hbuf, qf_s, kk_s, cum_s, v_s, o_s, sold_s = scratch

    pv = pvec_ref[...]

    def prow(r):
        return pv[r:r + 1, :]

    def wcol(g):
        return _unpack_rows(win_ref[:, g * D:(g + 1) * D])

    def head_cols(hd):
        return slice(hd * HEAD, (hd + 1) * HEAD)

    def front():
        x = x_ref[...].reshape(rows, D)
        xn = _rms(x, prow(P_GAIN + 0)).astype(BF16)
        row_id = lax.broadcasted_iota(jnp.int32, (rows, 1), 0)

        u = _mm(xn, wcol(C_U))
        ubuf[:, SUBLANES:SUBLANES + tokens, :] = u.reshape(seqs, tokens, D)
        gates_and_hgrn_operands(xn)
        uc = prow(P_CONV_B) + prow(P_CONV_W + 3) * u
        for k in range(1, 4):
            shifted = ubuf[:, SUBLANES - k:SUBLANES - k + tokens, :].reshape(rows, D)
            uc = uc + prow(P_CONV_W + 3 - k) * shifted
        tail = ubuf[:, tokens:tokens + SUBLANES, :]
        ubuf[:, 0:SUBLANES, :] = tail
        if decode:
            cout_ref[...] = tail
        else:
            @pl.when(j == n_tiles - 1)
            def _conv_out():
                cout_ref[...] = tail

        ucb = uc.astype(BF16)
        r_parts, i_parts = [], []
        for b in range(N_HEADS):
            ri = _mm(ucb[:, head_cols(b)], _unpack_rows(wri_ref[b]))
            r_parts.append(ri[:, :HEAD])
            i_parts.append(ri[:, HEAD:])
        r_gate = _sigmoid_of_twice(jnp.concatenate(r_parts, axis=1) + prow(P_RG_B))
        i_gate = _sigmoid_of_twice(jnp.concatenate(i_parts, axis=1) + prow(P_IG_B))
        lam = prow(P_LAMBDA)
        softplus_neg_lam = jnp.maximum(-lam, 0.0) + jnp.log1p(jnp.exp(-jnp.abs(lam)))
        log_a = (-LRU_C) * r_gate * softplus_neg_lam
        a_cum = jnp.exp(log_a)
        th = jnp.tanh(log_a)
        sq = -2.0 * th / (1.0 - th)
        mult = jnp.where(sq > 0.0, sq * lax.rsqrt(sq), 0.0)
        if reset_first:
            mult = jnp.where(jnp.logical_and(row_id == 0, j == 0), 1.0, mult)
        b_cum = mult * i_gate * uc

        a_cum = a_cum.reshape(rows // SUBLANES, SUBLANES, D)
        b_cum = b_cum.reshape(rows // SUBLANES, SUBLANES, D)
        sublane = lax.broadcasted_iota(jnp.int32, (1, SUBLANES, 1), 1)
        for s in (1, 2, 4):
            keep = sublane >= s
            a_prev = jnp.where(keep, pltpu.roll(a_cum, s, 1), 1.0)
            b_prev = jnp.where(keep, pltpu.roll(b_cum, s, 1), 0.0)
            b_cum = a_cum * b_prev + b_cum
            a_cum = a_cum * a_prev
        a_cum = a_cum.reshape(rows, D)
        b_cum = b_cum.reshape(rows, D)

        if decode:
            h_in = jnp.broadcast_to(h0_ref[...], (seqs, SUBLANES, D)).reshape(rows, D)
            h_all = a_cum * h_in + b_cum
            hout_ref[...] = h_all.reshape(seqs, SUBLANES, D)[:, SUBLANES - 1:SUBLANES, :]
        else:
            h = hout_ref[0]
            for g in range(rows // SUBLANES):
                sl = slice(g * SUBLANES, (g + 1) * SUBLANES)
                hg = a_cum[sl] * h + b_cum[sl]
                hbuf[sl, :] = hg
                h = hg[SUBLANES - 1:SUBLANES, :]
            hout_ref[0] = h
            h_all = hbuf[...]

        ya = (h_all * _gelu_tanh_of_twice(_mm(xn, wcol(C_GATE)))).astype(BF16)
        ga_s[...] = _sigmoid_of_twice(_mm(xn, wcol(C_MA))) * _mm(ya, _unpack_rows(wa_ref[...]))

    def gates_and_hgrn_operands(xn):
        sgb_s[...] = _sigmoid_of_twice(_mm(xn, wcol(C_MB)))
        sog_s[...] = _silu_of_twice(_mm(xn, wcol(C_OG)))

        qf = _silu_of_twice(_mm(xn, wcol(C_Q)))
        lb_raw = pv[P_LB:P_LB + 2, :]
        lb_exp = jnp.exp(lb_raw - jnp.max(lb_raw, axis=0, keepdims=True))
        lb = lb_exp[0:1, :] / jnp.sum(lb_exp, axis=0, keepdims=True)
        fg = lb + (1.0 - lb) / (1.0 + jnp.exp(-_mm(xn, wcol(C_F))))
        logf = jnp.log(fg)
        kk = 1.0 - fg
        v = _mm(xn, wcol(C_I))

        ri2 = lax.broadcasted_iota(jnp.int32, (rows, rows), 0)
        ci2 = lax.broadcasted_iota(jnp.int32, (rows, rows), 1)
        if decode:
            same_seq = (ri2 // tokens) == (ci2 // tokens)
            total = _mask_matmul(same_seq.astype(BF16), logf)
            cum = _mask_matmul(jnp.logical_and(ci2 <= ri2, same_seq).astype(BF16), logf)
            for hd in range(N_HEADS):
                hs = head_cols(hd)
                v_h[hd] = v[:, hs]
                tot_h[hd] = total[:, hs]
                qf_h[hd] = qf[:, hs]
                kk_h[hd] = kk[:, hs]
                cum_h[hd] = cum[:, hs]
        else:
            qf_s[...] = qf
            kk_s[...] = kk
            v_s[...] = v
            cum_s[...] = _mask_matmul((ci2 <= ri2).astype(BF16), logf)

    def head_norm(o_h):
        ms = jnp.mean(o_h * o_h, axis=-1, keepdims=True)
        return o_h * lax.rsqrt(ms + EPS)

    def finish(o_heads):
        o_n = jnp.concatenate([head_norm(o_h) for o_h in o_heads], axis=1) * prow(P_GNORM)
        yb = (o_n * sog_s[...]).astype(BF16)
        mixed = ga_s[...] + sgb_s[...] * _mm(yb, _unpack_rows(wb_ref[...]))
        z = _mm(mixed.astype(BF16), _unpack_rows(wout_ref[...]))
        x1 = x_ref[...].reshape(rows, D) + _rms(z, prow(P_GAIN + 1))
        x1_ref[...] = x1.reshape(seqs, tokens, D)

    def back_prompt():
        qf = qf_s[...]
        kk = kk_s[...]
        cum = cum_s[...]
        vb = v_s[...].astype(BF16)
        n_sub = rows // sub
        last = cum[rows - 1:rows, :]
        qib = (qf * jnp.exp(cum)).astype(BF16)
        ksb = (kk * jnp.exp(last - cum)).astype(BF16)
        decay_row = jnp.exp(last)
        for hd in range(N_HEADS):
            hs = head_cols(hd)
            s_old = sout_ref[0, hd]
            parts = []
            for i in range(n_sub):
                r0 = i * sub
                r1 = r0 + sub
                width = min(rows, -(-r1 // HEAD) * HEAD)
                ref_row = cum[r0 - 1:r0, hs] if i > 0 else jnp.zeros((1, HEAD), F32)
                qd = (qf[r0:r1, hs] * jnp.exp(cum[r0:r1, hs] - ref_row)).astype(BF16)
                arg = ref_row - cum[0:width, hs]
                if width > r1:
                    arg = jnp.where(lax.broadcasted_iota(jnp.int32, (width, 1), 0) < r1, arg, 0.0)
                kdi = (kk[0:width, hs] * jnp.exp(arg)).astype(BF16)
                att = _mm_nt(qd, kdi)
                rr = lax.broadcasted_iota(jnp.int32, (sub, width), 0) + r0
                cc = lax.broadcasted_iota(jnp.int32, (sub, width), 1)
                att = jnp.where(cc <= rr, att, 0.0).astype(BF16)
                if width < rows:
                    att = jnp.concatenate([att, jnp.zeros((sub, rows - width), BF16)], axis=1)
                parts.append(att)
            att_full = jnp.concatenate(parts, axis=0) if n_sub > 1 else parts[0]
            if rows % HEAD == 0:
                o_h = _mm(jnp.concatenate([att_full, qib[:, hs]], axis=1),
                          jnp.concatenate([vb[:, hs], s_old.astype(BF16)], axis=0))
            else:
                o_h = _mm(att_full, vb[:, hs]) + _mm(qib[:, hs], s_old.astype(BF16))
            upd = _mm_tn(ksb[:, hs], vb[:, hs])
            sout_ref[0, hd] = _column_of(decay_row[:, hs]) * s_old + upd
            sold_s[hd] = s_old
            o_s[:, hs] = o_h

        worst = jnp.zeros((1, D), F32)
        for i in range(n_sub):
            start = cum[i * sub - 1:i * sub, :] if i > 0 else jnp.zeros((1, D), F32)
            worst = jnp.maximum(worst, start - cum[(i + 1) * sub - 1:(i + 1) * sub, :])
        out_of_range = jnp.max(worst) > SAFE_DECAY_RANGE

        @pl.when(out_of_range)
        def _exact_scores():
            row_i = lax.broadcasted_iota(jnp.int32, (rows, 1), 0)
            for hd in range(N_HEADS):
                hs = head_cols(hd)
                qib_h = (qf_s[:, hs] * jnp.exp(cum_s[:, hs])).astype(BF16)
                o_s[:, hs] = _mm(qib_h, sold_s[hd].astype(BF16))

                def block_body(b, carry, hs=hs):
                    r0 = pl.multiple_of(b * SUBLANES, SUBLANES)
                    rs = pl.ds(r0, SUBLANES)
                    q_b, k_b, c_b, v_b = qf_s[rs, hs], kk_s[rs, hs], cum_s[rs, hs], v_s[rs, hs]
                    prev_start = pl.multiple_of(jnp.maximum(r0 - SUBLANES, 0), SUBLANES)
                    before = cum_s[pl.ds(prev_start, SUBLANES), hs][SUBLANES - 1:SUBLANES, :]
                    ref_row = jnp.where(b > 0, before, 0.0)
                    qd = (q_b * jnp.exp(c_b - ref_row)).astype(BF16)
                    arg = jnp.minimum(ref_row - cum_s[:, hs], 0.0)
                    kd = jnp.where(row_i < r0, kk_s[:, hs] * jnp.exp(arg), 0.0).astype(BF16)
                    earlier = _mm(_mm_nt(qd, kd).astype(BF16), v_s[:, hs].astype(BF16))
                    o_s[rs, hs] = o_s[rs, hs] + earlier + _exact_block(q_b, k_b, c_b, v_b)
                    return carry

                lax.fori_loop(0, rows // SUBLANES, block_body, 0)

        finish([o_s[:, head_cols(hd)] for hd in range(N_HEADS)])

    def decode_state_step():
        def seq_body(g, carry):
            rsl = pl.ds(pl.multiple_of(g * SUBLANES, SUBLANES), SUBLANES)
            s_old = s0_ref[g, 0]
            c_b = cum_h[j, rsl, :]
            total_b = tot_h[j, rsl, :]
            qi = qf_h[j, rsl, :] * jnp.exp(c_b)
            ks = kk_h[j, rsl, :] * jnp.exp(total_b - c_b)
            oin_h[j, rsl, :] = _mm(qi.astype(BF16), s_old.astype(BF16))
            upd = _mm_tn(ks.astype(BF16), v_h[j, rsl, :].astype(BF16))
            decay = _column_of(jnp.exp(total_b[0:1, :]))
            sout_ref[g, 0] = decay * s_old + upd
            return carry

        lax.fori_loop(0, seqs, seq_body, 0, unroll=SUBLANES)

    def back_decode():
        ri = lax.broadcasted_iota(jnp.int32, (rows, rows), 0)
        ci = lax.broadcasted_iota(jnp.int32, (rows, rows), 1)
        causal = jnp.logical_and(ci <= ri, (ri // tokens) == (ci // tokens))
        for hd in range(N_HEADS):
            c_h = cum_h[hd]
            scores = _mm_nt((qf_h[hd] * jnp.exp(c_h)).astype(BF16), (kk_h[hd] * jnp.exp(-c_h)).astype(BF16))
            att = jnp.where(causal, scores, 0.0).astype(BF16)
            oi_h[hd] = _mm(att, v_h[hd].astype(BF16))

        worst = jnp.zeros((1, HEAD), F32)
        for hd in range(N_HEADS):
            worst = jnp.maximum(worst, jnp.max(-tot_h[hd], axis=0, keepdims=True))
        out_of_range = jnp.max(worst) > SAFE_DECAY_RANGE

        @pl.when(out_of_range)
        def _exact_scores():
            for hd in range(N_HEADS):
                def seq_body(g, carry, hd=hd):
                    rs = pl.ds(pl.multiple_of(g * SUBLANES, SUBLANES), SUBLANES)
                    oi_h[hd, rs, :] = _exact_block(qf_h[hd, rs, :], kk_h[hd, rs, :], cum_h[hd, rs, :], v_h[hd, rs, :])
                    return carry

                lax.fori_loop(0, seqs, seq_body, 0)

        finish([oin_h[hd] + oi_h[hd] for hd in range(N_HEADS)])

    if decode:
        @pl.when(j == 0)
        def _front_region():
            ubuf[:, 0:SUBLANES, :] = cext_ref[...]
            front()

        decode_state_step()

        @pl.when(j == N_HEADS - 1)
        def _back_region():
            back_decode()
    else:
        @pl.when(j == 0)
        def _init():
            ubuf[:, 0:SUBLANES, :] = cext_ref[...]
            hout_ref[...] = h0_ref[...]
            sout_ref[...] = s0_ref[...]

        front()

        @pl.when(j >= 0)
        def _back_region():
            back_prompt()


def _resident(shape):
    return pl.BlockSpec(shape, lambda i, j: (0,) * len(shape), pipeline_mode=pl.Buffered(1))


def _mixer(x, cext, h0, s0, pvec, w_in, w_ri, w_a, w_b, w_out, *, seqs, tokens, reset_first, sub, name):
    n_seq, length, _ = x.shape
    n_tiles = length // tokens
    decode = seqs > 1
    assert n_seq % seqs == 0 and length % tokens == 0 and tokens % SUBLANES == 0
    assert not decode or (n_tiles == 1 and tokens == SUBLANES)
    assert (seqs * tokens) % sub == 0
    rows = seqs * tokens
    kern = functools.partial(_mixer_kernel, seqs=seqs, tokens=tokens, n_tiles=n_tiles,
                             reset_first=reset_first, sub=sub)
    tile_f32 = pltpu.VMEM((rows, D), F32)
    by_head_f32 = pltpu.VMEM((N_HEADS, rows, HEAD), F32)
    common = [pltpu.VMEM((seqs, SUBLANES + tokens, D), F32),
              tile_f32, tile_f32, tile_f32]
    if decode:
        grid = (n_seq // seqs, N_HEADS)
        x_map = lambda i, j: (i, 0, 0)
        s_block = (seqs, 1, HEAD, HEAD)
        s_map = lambda i, j: (i, j, 0, 0)
        scratch = common + [by_head_f32] * 7
    else:
        grid = (n_seq, n_tiles)
        x_map = lambda i, j: (i, j, 0)
        s_block = (1, N_HEADS, HEAD, HEAD)
        s_map = lambda i, j: (i, 0, 0, 0)
        scratch = common + [tile_f32] * 6 + [pltpu.VMEM((N_HEADS, HEAD, HEAD), F32)]
    seq_map = lambda i, j: (i, 0, 0)
    return pl.pallas_call(
        kern,
        grid=grid,
        in_specs=[
            pl.BlockSpec((seqs, tokens, D), x_map),
            pl.BlockSpec((seqs, SUBLANES, D), seq_map),
            pl.BlockSpec((seqs, 1, D), seq_map),
            pl.BlockSpec(s_block, s_map),
            _resident((P_ROWS, D)),
            _resident(w_in.shape),
            _resident(w_ri.shape),
            _resident(w_a.shape),
            _resident(w_b.shape),
            _resident(w_out.shape),
        ],
        out_specs=[
            pl.BlockSpec((seqs, tokens, D), x_map),
            pl.BlockSpec((seqs, SUBLANES, D), seq_map),
            pl.BlockSpec((seqs, 1, D), seq_map),
            pl.BlockSpec(s_block, s_map),
        ],
        out_shape=[
            jax.ShapeDtypeStruct((n_seq, length, D), F32),
            jax.ShapeDtypeStruct((n_seq, SUBLANES, D), F32),
            jax.ShapeDtypeStruct((n_seq, 1, D), F32),
            jax.ShapeDtypeStruct((n_seq, N_HEADS, HEAD, HEAD), F32),
        ],
        scratch_shapes=scratch,
        compiler_params=pltpu.CompilerParams(
            dimension_semantics=("arbitrary", "arbitrary"),
            vmem_limit_bytes=VMEM_LIMIT_BYTES),
        name=name,
    )(x, cext, h0, s0, pvec, w_in, w_ri, w_a, w_b, w_out)


def _mlp_kernel(x_ref, g_ref, wup_ref, wdn_ref, o_ref):
    x = x_ref[...]
    hn = _rms(x, g_ref[0:1, :]).astype(BF16)
    acc = jnp.zeros(x.shape, F32)
    for c in range(D_FF // D):
        t = _mm(hn, _unpack_rows(wup_ref[:, c * D:(c + 1) * D]))
        t = jnp.square(jnp.maximum(t, 0.0)).astype(BF16)
        acc = acc + _mm(t, _unpack_rows(wdn_ref[c * (D // 2):(c + 1) * (D // 2), :]))
    o_ref[...] = x + _rms(acc, g_ref[1:2, :])


def _mlp(x2d, gains, w_up, w_down, *, tile, name):
    n = x2d.shape[0]
    assert n % tile == 0
    return pl.pallas_call(
        _mlp_kernel,
        grid=(n // tile,),
        in_specs=[
            pl.BlockSpec((tile, D), lambda i: (i, 0)),
            pl.BlockSpec((2, D), lambda i: (0, 0), pipeline_mode=pl.Buffered(1)),
            pl.BlockSpec(w_up.shape, lambda i: (0, 0), pipeline_mode=pl.Buffered(1)),
            pl.BlockSpec(w_down.shape, lambda i: (0, 0), pipeline_mode=pl.Buffered(1)),
        ],
        out_specs=pl.BlockSpec((tile, D), lambda i: (i, 0)),
        out_shape=jax.ShapeDtypeStruct((n, D), F32),
        compiler_params=pltpu.CompilerParams(
            dimension_semantics=("arbitrary",),
            vmem_limit_bytes=VMEM_LIMIT_BYTES),
        name=name,
    )(x2d, gains, w_up, w_down)


def kernel(x_prompt, x_sample, state_conv, state_rglru, state_hgrn, meta_tokens, norm_gains, w_in,
           conv_w, conv_b, rg_w, rg_b, ig_w, ig_b, lru_lambda, hgrn_lb, hgrn_gnorm,
           w_branch_a, w_branch_b, w_out, w_up, w_down):
    bp, seq, _ = x_prompt.shape
    bs, dec_seq, _ = x_sample.shape
    layer = 0
    pvec = jnp.concatenate([
        norm_gains[layer], conv_w[layer], conv_b[layer][None], 0.5 * rg_b[layer][None], 0.5 * ig_b[layer][None],
        lru_lambda[layer][None], hgrn_lb[layer:layer + 2], hgrn_gnorm[layer][None],
        jnp.zeros((1, D), F32)], axis=0).astype(F32)
    halved = (C_GATE, C_Q, C_OG, C_MA, C_MB)
    scale_in = jnp.concatenate([jnp.full((1, D), 0.5 if g in halved else 1.0, F32) for g in range(8)], axis=1)

    def ones(n):
        return jnp.ones((1, n), F32)

    w_in_b = _pack_rows(w_in[layer], scale_in, name="pack_w_in")
    w_gates = jnp.concatenate([rg_w[layer], ig_w[layer]], axis=-1)
    w_ri = _pack_rows(w_gates.reshape(N_HEADS * HEAD, 2 * HEAD), 0.5 * ones(2 * HEAD), name="pack_w_gates")
    w_ri = w_ri.reshape(N_HEADS, HEAD // 2, 2 * HEAD)
    w_a = _pack_rows(w_branch_a[layer], ones(D), name="pack_w_a")
    w_b = _pack_rows(w_branch_b[layer], ones(D), name="pack_w_b")
    w_o = _pack_rows(w_out[layer], ones(D), name="pack_w_out")
    w_u = _pack_rows(w_up[layer], ones(D_FF), name="pack_w_up")
    w_d = _pack_rows(w_down[layer], ones(D), name="pack_w_down")
    weights = (pvec, w_in_b, w_ri, w_a, w_b, w_o)

    zeros_c = jnp.zeros((1, SUBLANES, D), F32)
    zeros_h = jnp.zeros((1, 1, D), F32)
    zeros_s = jnp.zeros((1, N_HEADS, HEAD, HEAD), F32)
    _, c_m, h_m, s_m = _mixer(meta_tokens[None].astype(F32), zeros_c, zeros_h, zeros_s, *weights,
                              seqs=1, tokens=N_META, reset_first=True, sub=N_META, name="mixer_meta")

    x1_p, c_p, h_p, s_p = _mixer(
        x_prompt,
        jnp.broadcast_to(c_m, (bp, SUBLANES, D)),
        jnp.broadcast_to(h_m, (bp, 1, D)),
        jnp.broadcast_to(s_m, (bp, N_HEADS, HEAD, HEAD)),
        *weights, seqs=1, tokens=256, reset_first=False, sub=64, name="mixer_prompt")

    cext_s = jnp.pad(state_conv[layer], ((0, 0), (SUBLANES - 3, 0), (0, 0)))
    x1_s, c_s, h_s, s_s = _mixer(
        x_sample, cext_s, state_rglru[layer][:, None, :], state_hgrn[layer],
        *weights, seqs=32, tokens=dec_seq, reset_first=False, sub=8, name="mixer_sample")

    gains_mlp = norm_gains[layer, 2:4]
    y_p = _mlp(x1_p.reshape(bp * seq, D), gains_mlp, w_u, w_d, tile=512, name="mlp_prompt")
    y_s = _mlp(x1_s.reshape(bs * dec_seq, D), gains_mlp, w_u, w_d, tile=512, name="mlp_sample")

    return (y_p.reshape(bp, seq, D), y_s.reshape(bs, dec_seq, D),
            c_p[:, SUBLANES - 3:, :][None], h_p[:, 0, :][None], s_p[None],
            c_s[:, SUBLANES - 3:, :][None], h_s[:, 0, :][None], s_s[None])
```

```python
import functools

import jax
import jax.numpy as jnp
from jax import lax
from jax.experimental import pallas as pl
from jax.experimental.pallas import tpu as pltpu

D = 1024
N_HEADS = 8
HEAD = 128
N_META = 16
LRU_C = 8.0
EPS = 1e-6
D_FF = 4096
SUBLANES = 8
F32 = jnp.float32
BF16 = jnp.bfloat16

P_GAIN, P_CONV_W, P_CONV_B, P_RG_B, P_IG_B, P_LAMBDA, P_LB, P_GNORM, P_ROWS = 0, 4, 8, 9, 10, 11, 12, 14, 16

C_U, C_GATE, C_Q, C_F, C_I, C_OG, C_MA, C_MB = range(8)

VMEM_LIMIT_BYTES = 60 * 1024 * 1024
PACK_BLOCK_ELEMS = 1 << 20
SAFE_DECAY_RANGE = 64.0


def _rms(x, g):
    ms = jnp.mean(x * x, axis=-1, keepdims=True)
    return x * lax.rsqrt(ms + EPS) * g


def _sigmoid_of_twice(h):
    return 0.5 * jnp.tanh(h) + 0.5


def _silu_of_twice(h):
    return h + h * jnp.tanh(h)


_GELU_C1 = 2.0 * 0.7978845608028654
_GELU_C2 = 8.0 * 0.7978845608028654 * 0.044715


def _gelu_tanh_of_twice(h):
    return h + h * jnp.tanh(h * (_GELU_C1 + _GELU_C2 * (h * h)))


def _mm(a, b):
    return jnp.dot(a, b, preferred_element_type=F32)


def _mm_nt(a, b):
    return lax.dot_general(a, b, (((1,), (1,)), ((), ())), preferred_element_type=F32)


def _mm_tn(a, b):
    return lax.dot_general(a, b, (((0,), (0,)), ((), ())), preferred_element_type=F32)


def _mask_matmul(mask_bf, x):
    hi = x.astype(BF16)
    lo = (x - hi.astype(F32)).astype(BF16)
    return _mm(mask_bf, hi) + _mm(mask_bf, lo)


def _pack_kernel(w_ref, scale_ref, o_ref):
    o_ref[...] = pltpu.bitcast((w_ref[...] * scale_ref[...]).astype(BF16), jnp.int32)


def _pack_rows(w, col_scale, *, name):
    k, n = w.shape
    tile = min(k, PACK_BLOCK_ELEMS // n)
    assert k % tile == 0 and tile % (2 * SUBLANES) == 0
    return pl.pallas_call(
        _pack_kernel,
        grid=(k // tile,),
        in_specs=[pl.BlockSpec((tile, n), lambda i: (i, 0)),
                  pl.BlockSpec((1, n), lambda i: (0, 0))],
        out_specs=pl.BlockSpec((tile // 2, n), lambda i: (i, 0)),
        out_shape=jax.ShapeDtypeStruct((k // 2, n), jnp.int32),
        compiler_params=pltpu.CompilerParams(dimension_semantics=("arbitrary",)),
        name=name,
    )(w, col_scale)


def _unpack_rows(words):
    return pltpu.bitcast(words, BF16)


def _exact_block(q_b, k_b, c_b, v_b):
    t_idx = lax.broadcasted_iota(jnp.int32, (SUBLANES, 1), 0)
    acc = jnp.zeros((SUBLANES, HEAD), F32)
    for s in range(SUBLANES):
        decay = jnp.exp(jnp.minimum(c_b - c_b[s:s + 1, :], 0.0))
        score = jnp.sum(q_b * k_b[s:s + 1, :] * decay, axis=-1, keepdims=True)
        acc = acc + jnp.where(t_idx >= s, score, 0.0) * v_b[s:s + 1, :]
    return acc


def _column_of(row):
    return jnp.transpose(jnp.broadcast_to(row, (HEAD, HEAD)))


def _mixer_kernel(x_ref, cext_ref, h0_ref, s0_ref, pvec_ref, win_ref, wri_ref, wa_ref, wb_ref, wout_ref,
                  x1_ref, cout_ref, hout_ref, sout_ref, *scratch,
                  seqs, tokens, n_tiles, reset_first, sub):
    j = pl.program_id(1)
    rows = seqs * tokens
    decode = seqs > 1
    if decode:
        ubuf, ga_s, sgb_s, sog_s, qf_h, kk_h, cum_h, v_h, tot_h, oin_h, oi_h = scratch
    else:
        ubuf, ga_s, sgb_s, sog_s, hbuf, qf_s, kk_s, cum_s, v_s, o_s, sold_s = scratch

    pv = pvec_ref[...]

    def prow(r):
        return pv[r:r + 1, :]

    def wcol(g):
        return _unpack_rows(win_ref[:, g * D:(g + 1) * D])

    def head_cols(hd):
        return slice(hd * HEAD, (hd + 1) * HEAD)

    def front():
        x = x_ref[...].reshape(rows, D)
        xn = _rms(x, prow(P_GAIN + 0)).astype(BF16)
        row_id = lax.broadcasted_iota(jnp.int32, (rows, 1), 0)

        u = _mm(xn, wcol(C_U))
        ubuf[:, SUBLANES:SUBLANES + tokens, :] = u.reshape(seqs, tokens, D)
        merge_gates(xn)
        uc = prow(P_CONV_B) + prow(P_CONV_W + 3) * u
        for k in range(1, 4):
            shifted = ubuf[:, SUBLANES - k:SUBLANES - k + tokens, :].reshape(rows, D)
            uc = uc + prow(P_CONV_W + 3 - k) * shifted
        tail = ubuf[:, tokens:tokens + SUBLANES, :]
        ubuf[:, 0:SUBLANES, :] = tail
        cout_ref[...] = tail

        ucb = uc.astype(BF16)
        r_parts, i_parts = [], []
        for b in range(N_HEADS):
            ri = _mm(ucb[:, head_cols(b)], _unpack_rows(wri_ref[b]))
            r_parts.append(ri[:, :HEAD])
            i_parts.append(ri[:, HEAD:])
        r_pre = jnp.concatenate(r_parts, axis=1)
        i_pre = jnp.concatenate(i_parts, axis=1)
        if not decode:
            hbuf[...] = r_pre
            o_s[...] = i_pre
        hgrn_operands(xn)
        if not decode:
            r_pre = hbuf[...]
            i_pre = o_s[...]
        r_gate = _sigmoid_of_twice(r_pre + prow(P_RG_B))
        i_gate = _sigmoid_of_twice(i_pre + prow(P_IG_B))
        lam = prow(P_LAMBDA)
        softplus_neg_lam = jnp.maximum(-lam, 0.0) + jnp.log1p(jnp.exp(-jnp.abs(lam)))
        log_a = (-LRU_C) * r_gate * softplus_neg_lam
        a_cum = jnp.exp(log_a)
        th = jnp.tanh(log_a)
        sq = -2.0 * th / (1.0 - th)
        mult = jnp.where(sq > 0.0, sq * lax.rsqrt(sq), 0.0)
        if reset_first:
            mult = jnp.where(jnp.logical_and(row_id == 0, j == 0), 1.0, mult)
        b_cum = mult * i_gate * uc

        a_cum = a_cum.reshape(rows // SUBLANES, SUBLANES, D)
        b_cum = b_cum.reshape(rows // SUBLANES, SUBLANES, D)
        sublane = lax.broadcasted_iota(jnp.int32, (1, SUBLANES, 1), 1)
        for s in (1, 2, 4):
            keep = sublane >= s
            a_prev = jnp.where(keep, pltpu.roll(a_cum, s, 1), 1.0)
            b_prev = jnp.where(keep, pltpu.roll(b_cum, s, 1), 0.0)
            b_cum = a_cum * b_prev + b_cum
            a_cum = a_cum * a_prev
        a_cum = a_cum.reshape(rows, D)
        b_cum = b_cum.reshape(rows, D)

        if decode:
            h_in = jnp.broadcast_to(h0_ref[...], (seqs, SUBLANES, D)).reshape(rows, D)
            h_all = a_cum * h_in + b_cum
            hout_ref[...] = h_all.reshape(seqs, SUBLANES, D)[:, SUBLANES - 1:SUBLANES, :]
        else:
            h = hout_ref[0]
            for g in range(rows // SUBLANES):
                sl = slice(g * SUBLANES, (g + 1) * SUBLANES)
                hg = a_cum[sl] * h + b_cum[sl]
                hbuf[sl, :] = hg
                h = hg[SUBLANES - 1:SUBLANES, :]
            hout_ref[0] = h
            h_all = hbuf[...]

        ya = (h_all * _gelu_tanh_of_twice(_mm(xn, wcol(C_GATE)))).astype(BF16)
        ga_s[...] = _sigmoid_of_twice(_mm(xn, wcol(C_MA))) * _mm(ya, _unpack_rows(wa_ref[...]))

    def merge_gates(xn):
        sgb_s[...] = _sigmoid_of_twice(_mm(xn, wcol(C_MB)))
        sog_s[...] = _silu_of_twice(_mm(xn, wcol(C_OG)))

    def hgrn_operands(xn):
        qf = _silu_of_twice(_mm(xn, wcol(C_Q)))
        lb_raw = pv[P_LB:P_LB + 2, :]
        lb_exp = jnp.exp(lb_raw - jnp.max(lb_raw, axis=0, keepdims=True))
        lb = lb_exp[0:1, :] / jnp.sum(lb_exp, axis=0, keepdims=True)
        fg = lb + (1.0 - lb) / (1.0 + jnp.exp(-_mm(xn, wcol(C_F))))
        logf = jnp.log(fg)
        kk = 1.0 - fg
        v = _mm(xn, wcol(C_I))

        ri2 = lax.broadcasted_iota(jnp.int32, (rows, rows), 0)
        ci2 = lax.broadcasted_iota(jnp.int32, (rows, rows), 1)
        if decode:
            same_seq = (ri2 // tokens) == (ci2 // tokens)
            total = _mask_matmul(same_seq.astype(BF16), logf)
            cum = _mask_matmul(jnp.logical_and(ci2 <= ri2, same_seq).astype(BF16), logf)
            for hd in range(N_HEADS):
                hs = head_cols(hd)
                v_h[hd] = v[:, hs]
                tot_h[hd] = total[:, hs]
                qf_h[hd] = qf[:, hs]
                kk_h[hd] = kk[:, hs]
                cum_h[hd] = cum[:, hs]
        else:
            qf_s[...] = qf
            kk_s[...] = kk
            v_s[...] = v
            cum_s[...] = _mask_matmul((ci2 <= ri2).astype(BF16), logf)

    def head_norm(o_h):
        ms = jnp.mean(o_h * o_h, axis=-1, keepdims=True)
        return o_h * lax.rsqrt(ms + EPS)

    def finish(o_heads):
        o_n = jnp.concatenate([head_norm(o_h) for o_h in o_heads], axis=1) * prow(P_GNORM)
        yb = (o_n * sog_s[...]).astype(BF16)
        mixed = ga_s[...] + sgb_s[...] * _mm(yb, _unpack_rows(wb_ref[...]))
        z = _mm(mixed.astype(BF16), _unpack_rows(wout_ref[...]))
        x1 = x_ref[...].reshape(rows, D) + _rms(z, prow(P_GAIN + 1))
        x1_ref[...] = x1.reshape(seqs, tokens, D)

    def back_prompt():
        qf = qf_s[...]
        kk = kk_s[...]
        cum = cum_s[...]
        vb = v_s[...].astype(BF16)
        n_sub = rows // sub
        last = cum[rows - 1:rows, :]
        qib = (qf * jnp.exp(cum)).astype(BF16)
        ksb = (kk * jnp.exp(last - cum)).astype(BF16)
        decay_row = jnp.exp(last)
        for hd in range(N_HEADS):
            hs = head_cols(hd)
            s_old = sout_ref[0, hd]
            parts = []
            for i in range(n_sub):
                r0 = i * sub
                r1 = r0 + sub
                width = min(rows, -(-r1 // HEAD) * HEAD)
                ref_row = cum[r0 - 1:r0, hs] if i > 0 else jnp.zeros((1, HEAD), F32)
                qd = (qf[r0:r1, hs] * jnp.exp(cum[r0:r1, hs] - ref_row)).astype(BF16)
                arg = ref_row - cum[0:width, hs]
                if width > r1:
                    arg = jnp.where(lax.broadcasted_iota(jnp.int32, (width, 1), 0) < r1, arg, 0.0)
                kdi = (kk[0:width, hs] * jnp.exp(arg)).astype(BF16)
                att = _mm_nt(qd, kdi)
                rr = lax.broadcasted_iota(jnp.int32, (sub, width), 0) + r0
                cc = lax.broadcasted_iota(jnp.int32, (sub, width), 1)
                att = jnp.where(cc <= rr, att, 0.0).astype(BF16)
                if width < rows:
                    att = jnp.concatenate([att, jnp.zeros((sub, rows - width), BF16)], axis=1)
                parts.append(att)
            att_full = jnp.concatenate(parts, axis=0) if n_sub > 1 else parts[0]
            if rows % HEAD == 0:
                o_h = _mm(jnp.concatenate([att_full, qib[:, hs]], axis=1),
                          jnp.concatenate([vb[:, hs], s_old.astype(BF16)], axis=0))
            else:
                o_h = _mm(att_full, vb[:, hs]) + _mm(qib[:, hs], s_old.astype(BF16))
            upd = _mm_tn(ksb[:, hs], vb[:, hs])
            sout_ref[0, hd] = _column_of(decay_row[:, hs]) * s_old + upd
            sold_s[hd] = s_old
            o_s[:, hs] = o_h

        worst = jnp.zeros((1, D), F32)
        for i in range(n_sub):
            start = cum[i * sub - 1:i * sub, :] if i > 0 else jnp.zeros((1, D), F32)
            worst = jnp.maximum(worst, start - cum[(i + 1) * sub - 1:(i + 1) * sub, :])
        out_of_range = jnp.max(worst) > SAFE_DECAY_RANGE

        @pl.when(out_of_range)
        def _exact_scores():
            row_i = lax.broadcasted_iota(jnp.int32, (rows, 1), 0)
            for hd in range(N_HEADS):
                hs = head_cols(hd)
                qib_h = (qf_s[:, hs] * jnp.exp(cum_s[:, hs])).astype(BF16)
                o_s[:, hs] = _mm(qib_h, sold_s[hd].astype(BF16))

                def block_body(b, carry, hs=hs):
                    r0 = pl.multiple_of(b * SUBLANES, SUBLANES)
                    rs = pl.ds(r0, SUBLANES)
                    q_b, k_b, c_b, v_b = qf_s[rs, hs], kk_s[rs, hs], cum_s[rs, hs], v_s[rs, hs]
                    prev_start = pl.multiple_of(jnp.maximum(r0 - SUBLANES, 0), SUBLANES)
                    before = cum_s[pl.ds(prev_start, SUBLANES), hs][SUBLANES - 1:SUBLANES, :]
                    ref_row = jnp.where(b > 0, before, 0.0)
                    qd = (q_b * jnp.exp(c_b - ref_row)).astype(BF16)
                    arg = jnp.minimum(ref_row - cum_s[:, hs], 0.0)
                    kd = jnp.where(row_i < r0, kk_s[:, hs] * jnp.exp(arg), 0.0).astype(BF16)
                    earlier = _mm(_mm_nt(qd, kd).astype(BF16), v_s[:, hs].astype(BF16))
                    o_s[rs, hs] = o_s[rs, hs] + earlier + _exact_block(q_b, k_b, c_b, v_b)
                    return carry

                lax.fori_loop(0, rows // SUBLANES, block_body, 0)

        finish([o_s[:, head_cols(hd)] for hd in range(N_HEADS)])

    def decode_state_step():
        def seq_body(g, carry):
            rsl = pl.ds(pl.multiple_of(g * SUBLANES, SUBLANES), SUBLANES)
            s_old = s0_ref[g, 0]
            c_b = cum_h[j, rsl, :]
            total_b = tot_h[j, rsl, :]
            qi = qf_h[j, rsl, :] * jnp.exp(c_b)
            ks = kk_h[j, rsl, :] * jnp.exp(total_b - c_b)
            oin_h[j, rsl, :] = _mm(qi.astype(BF16), s_old.astype(BF16))
            upd = _mm_tn(ks.astype(BF16), v_h[j, rsl, :].astype(BF16))
            decay = _column_of(jnp.exp(total_b[0:1, :]))
            sout_ref[g, 0] = decay * s_old + upd
            return carry

        lax.fori_loop(0, seqs, seq_body, 0, unroll=SUBLANES)

    def back_decode():
        ri = lax.broadcasted_iota(jnp.int32, (rows, rows), 0)
        ci = lax.broadcasted_iota(jnp.int32, (rows, rows), 1)
        causal = jnp.logical_and(ci <= ri, (ri // tokens) == (ci // tokens))
        for hd in range(N_HEADS):
            c_h = cum_h[hd]
            scores = _mm_nt((qf_h[hd] * jnp.exp(c_h)).astype(BF16), (kk_h[hd] * jnp.exp(-c_h)).astype(BF16))
            att = jnp.where(causal, scores, 0.0).astype(BF16)
            oi_h[hd] = _mm(att, v_h[hd].astype(BF16))

        worst = jnp.zeros((1, HEAD), F32)
        for hd in range(N_HEADS):
            worst = jnp.maximum(worst, jnp.max(-tot_h[hd], axis=0, keepdims=True))
        out_of_range = jnp.max(worst) > SAFE_DECAY_RANGE

        @pl.when(out_of_range)
        def _exact_scores():
            for hd in range(N_HEADS):
                def seq_body(g, carry, hd=hd):
                    rs = pl.ds(pl.multiple_of(g * SUBLANES, SUBLANES), SUBLANES)
                    oi_h[hd, rs, :] = _exact_block(qf_h[hd, rs, :], kk_h[hd, rs, :], cum_h[hd, rs, :], v_h[hd, rs, :])
                    return carry

                lax.fori_loop(0, seqs, seq_body, 0)

        finish([oin_h[hd] + oi_h[hd] for hd in range(N_HEADS)])

    if decode:
        @pl.when(j == 0)
        def _front_region():
            ubuf[:, 0:SUBLANES, :] = cext_ref[...]
            front()

        decode_state_step()

        @pl.when(j == N_HEADS - 1)
        def _back_region():
            back_decode()
    else:
        @pl.when(j == 0)
        def _init():
            ubuf[:, 0:SUBLANES, :] = cext_ref[...]
            hout_ref[...] = h0_ref[...]
            sout_ref[...] = s0_ref[...]

        front()

        @pl.when(j >= 0)
        def _back_region():
            back_prompt()


def _resident(shape):
    return pl.BlockSpec(shape, lambda i, j: (0,) * len(shape), pipeline_mode=pl.Buffered(1))


def _mixer(x, cext, h0, s0, pvec, w_in, w_ri, w_a, w_b, w_out, *, seqs, tokens, reset_first, sub, name):
    n_seq, length, _ = x.shape
    n_tiles = length // tokens
    decode = seqs > 1
    assert n_seq % seqs == 0 and length % tokens == 0 and tokens % SUBLANES == 0
    assert not decode or (n_tiles == 1 and tokens == SUBLANES)
    assert (seqs * tokens) % sub == 0
    rows = seqs * tokens
    kern = functools.partial(_mixer_kernel, seqs=seqs, tokens=tokens, n_tiles=n_tiles,
                             reset_first=reset_first, sub=sub)
    tile_f32 = pltpu.VMEM((rows, D), F32)
    by_head_f32 = pltpu.VMEM((N_HEADS, rows, HEAD), F32)
    common = [pltpu.VMEM((seqs, SUBLANES + tokens, D), F32),
              tile_f32, tile_f32, tile_f32]
    if decode:
        grid = (n_seq // seqs, N_HEADS)
        x_map = lambda i, j: (i, 0, 0)
        s_block = (seqs, 1, HEAD, HEAD)
        s_map = lambda i, j: (i, j, 0, 0)
        scratch = common + [by_head_f32] * 7
    else:
        grid = (n_seq, n_tiles)
        x_map = lambda i, j: (i, j, 0)
        s_block = (1, N_HEADS, HEAD, HEAD)
        s_map = lambda i, j: (i, 0, 0, 0)
        scratch = common + [tile_f32] * 6 + [pltpu.VMEM((N_HEADS, HEAD, HEAD), F32)]
    seq_map = lambda i, j: (i, 0, 0)
    return pl.pallas_call(
        kern,
        grid=grid,
        in_specs=[
            pl.BlockSpec((seqs, tokens, D), x_map),
            pl.BlockSpec((seqs, SUBLANES, D), seq_map),
            pl.BlockSpec((seqs, 1, D), seq_map),
            pl.BlockSpec(s_block, s_map),
            _resident((P_ROWS, D)),
            _resident(w_in.shape),
            _resident(w_ri.shape),
            _resident(w_a.shape),
            _resident(w_b.shape),
            _resident(w_out.shape),
        ],
        out_specs=[
            pl.BlockSpec((seqs, tokens, D), x_map),
            pl.BlockSpec((seqs, SUBLANES, D), seq_map),
            pl.BlockSpec((seqs, 1, D), seq_map),
            pl.BlockSpec(s_block, s_map),
        ],
        out_shape=[
            jax.ShapeDtypeStruct((n_seq, length, D), F32),
            jax.ShapeDtypeStruct((n_seq, SUBLANES, D), F32),
            jax.ShapeDtypeStruct((n_seq, 1, D), F32),
            jax.ShapeDtypeStruct((n_seq, N_HEADS, HEAD, HEAD), F32),
        ],
        scratch_shapes=scratch,
        compiler_params=pltpu.CompilerParams(
            dimension_semantics=("arbitrary", "arbitrary"),
            vmem_limit_bytes=VMEM_LIMIT_BYTES),
        name=name,
    )(x, cext, h0, s0, pvec, w_in, w_ri, w_a, w_b, w_out)


def _mlp_kernel(x_ref, g_ref, wup_ref, wdn_ref, o_ref):
    x = x_ref[...]
    hn = _rms(x, g_ref[0:1, :]).astype(BF16)
    acc = jnp.zeros(x.shape, F32)
    for c in range(D_FF // D):
        t = _mm(hn, _unpack_rows(wup_ref[:, c * D:(c + 1) * D]))
        t = jnp.square(jnp.maximum(t, 0.0)).astype(BF16)
        acc = acc + _mm(t, _unpack_rows(wdn_ref[c * (D // 2):(c + 1) * (D // 2), :]))
    o_ref[...] = x + _rms(acc, g_ref[1:2, :])


def _mlp(x2d, gains, w_up, w_down, *, tile, name):
    n = x2d.shape[0]
    assert n % tile == 0
    return pl.pallas_call(
        _mlp_kernel,
        grid=(n // tile,),
        in_specs=[
            pl.BlockSpec((tile, D), lambda i: (i, 0)),
            pl.BlockSpec((2, D), lambda i: (0, 0), pipeline_mode=pl.Buffered(1)),
            pl.BlockSpec(w_up.shape, lambda i: (0, 0), pipeline_mode=pl.Buffered(1)),
            pl.BlockSpec(w_down.shape, lambda i: (0, 0), pipeline_mode=pl.Buffered(1)),
        ],
        out_specs=pl.BlockSpec((tile, D), lambda i: (i, 0)),
        out_shape=jax.ShapeDtypeStruct((n, D), F32),
        compiler_params=pltpu.CompilerParams(
            dimension_semantics=("arbitrary",),
            vmem_limit_bytes=VMEM_LIMIT_BYTES),
        name=name,
    )(x2d, gains, w_up, w_down)


def kernel(x_prompt, x_sample, state_conv, state_rglru, state_hgrn, meta_tokens, norm_gains, w_in,
           conv_w, conv_b, rg_w, rg_b, ig_w, ig_b, lru_lambda, hgrn_lb, hgrn_gnorm,
           w_branch_a, w_branch_b, w_out, w_up, w_down):
    bp, seq, _ = x_prompt.shape
    bs, dec_seq, _ = x_sample.shape
    layer = 0
    pvec = jnp.concatenate([
        norm_gains[layer], conv_w[layer], conv_b[layer][None], 0.5 * rg_b[layer][None], 0.5 * ig_b[layer][None],
        lru_lambda[layer][None], hgrn_lb[layer:layer + 2], hgrn_gnorm[layer][None],
        jnp.zeros((1, D), F32)], axis=0).astype(F32)
    halved = (C_GATE, C_Q, C_OG, C_MA, C_MB)
    scale_in = jnp.concatenate([jnp.full((1, D), 0.5 if g in halved else 1.0, F32) for g in range(8)], axis=1)

    def ones(n):
        return jnp.ones((1, n), F32)

    w_in_b = _pack_rows(w_in[layer], scale_in, name="pack_w_in")
    w_gates = jnp.concatenate([rg_w[layer], ig_w[layer]], axis=-1)
    w_ri = _pack_rows(w_gates.reshape(N_HEADS * HEAD, 2 * HEAD), 0.5 * ones(2 * HEAD), name="pack_w_gates")
    w_ri = w_ri.reshape(N_HEADS, HEAD // 2, 2 * HEAD)
    w_a = _pack_rows(w_branch_a[layer], ones(D), name="pack_w_a")
    w_b = _pack_rows(w_branch_b[layer], ones(D), name="pack_w_b")
    w_o = _pack_rows(w_out[layer], ones(D), name="pack_w_out")
    w_u = _pack_rows(w_up[layer], ones(D_FF), name="pack_w_up")
    w_d = _pack_rows(w_down[layer], ones(D), name="pack_w_down")
    weights = (pvec, w_in_b, w_ri, w_a, w_b, w_o)

    zeros_c = jnp.zeros((1, SUBLANES, D), F32)
    zeros_h = jnp.zeros((1, 1, D), F32)
    zeros_s = jnp.zeros((1, N_HEADS, HEAD, HEAD), F32)
    _, c_m, h_m, s_m = _mixer(meta_tokens[None].astype(F32), zeros_c, zeros_h, zeros_s, *weights,
                              seqs=1, tokens=N_META, reset_first=True, sub=N_META, name="mixer_meta")

    x1_p, c_p, h_p, s_p = _mixer(
        x_prompt,
        jnp.broadcast_to(c_m, (bp, SUBLANES, D)),
        jnp.broadcast_to(h_m, (bp, 1, D)),
        jnp.broadcast_to(s_m, (bp, N_HEADS, HEAD, HEAD)),
        *weights, seqs=1, tokens=256, reset_first=False, sub=64, name="mixer_prompt")

    cext_s = jnp.pad(state_conv[layer], ((0, 0), (SUBLANES - 3, 0), (0, 0)))
    x1_s, c_s, h_s, s_s = _mixer(
        x_sample, cext_s, state_rglru[layer][:, None, :], state_hgrn[layer],
        *weights, seqs=32, tokens=dec_seq, reset_first=False, sub=8, name="mixer_sample")

    gains_mlp = norm_gains[layer, 2:4]
    y_p = _mlp(x1_p.reshape(bp * seq, D), gains_mlp, w_u, w_d, tile=512, name="mlp_prompt")
    y_s = _mlp(x1_s.reshape(bs * dec_seq, D), gains_mlp, w_u, w_d, tile=512, name="mlp_sample")

    return (y_p.reshape(bp, seq, D), y_s.reshape(bs, dec_seq, D),
            c_p[:, SUBLANES - 3:, :][None], h_p[:, 0, :][None], s_p[None],
            c_s[:, SUBLANES - 3:, :][None], h_s[:, 0, :][None], s_s[None])
```

```python
import functools

import jax
import jax.numpy as jnp
from jax import lax
from jax.experimental import pallas as pl
from jax.experimental.pallas import tpu as pltpu

D = 1024
N_HEADS = 8
HEAD = 128
N_META = 16
LRU_C = 8.0
EPS = 1e-6
D_FF = 4096
SUBLANES = 8
F32 = jnp.float32
BF16 = jnp.bfloat16

P_GAIN, P_CONV_W, P_CONV_B, P_RG_B, P_IG_B, P_LAMBDA, P_LB, P_GNORM, P_ROWS = 0, 4, 8, 9, 10, 11, 12, 14, 16

C_U, C_GATE, C_Q, C_F, C_I, C_OG, C_MA, C_MB = range(8)

VMEM_LIMIT_BYTES = 60 * 1024 * 1024
PACK_BLOCK_ELEMS = 1 << 20
SAFE_DECAY_RANGE = 64.0


def _rms(x, g):
    ms = jnp.mean(x * x, axis=-1, keepdims=True)
    return x * lax.rsqrt(ms + EPS) * g


def _sigmoid_of_twice(h):
    return 0.5 * jnp.tanh(h) + 0.5


def _silu_of_twice(h):
    return h + h * jnp.tanh(h)


_GELU_C1 = 2.0 * 0.7978845608028654
_GELU_C2 = 8.0 * 0.7978845608028654 * 0.044715


def _gelu_tanh_of_twice(h):
    return h + h * jnp.tanh(h * (_GELU_C1 + _GELU_C2 * (h * h)))


def _mm(a, b):
    return jnp.dot(a, b, preferred_element_type=F32)


def _mm_nt(a, b):
    return lax.dot_general(a, b, (((1,), (1,)), ((), ())), preferred_element_type=F32)


def _mm_tn(a, b):
    return lax.dot_general(a, b, (((0,), (0,)), ((), ())), preferred_element_type=F32)


def _mask_matmul(mask_bf, x):
    hi = x.astype(BF16)
    lo = (x - hi.astype(F32)).astype(BF16)
    return _mm(mask_bf, hi) + _mm(mask_bf, lo)


def _pack_kernel(w_ref, scale_ref, o_ref):
    o_ref[...] = pltpu.bitcast((w_ref[...] * scale_ref[...]).astype(BF16), jnp.int32)


def _pack_rows(w, col_scale, *, name):
    k, n = w.shape
    tile = min(k, PACK_BLOCK_ELEMS // n)
    assert k % tile == 0 and tile % (2 * SUBLANES) == 0
    return pl.pallas_call(
        _pack_kernel,
        grid=(k // tile,),
        in_specs=[pl.BlockSpec((tile, n), lambda i: (i, 0)),
                  pl.BlockSpec((1, n), lambda i: (0, 0))],
        out_specs=pl.BlockSpec((tile // 2, n), lambda i: (i, 0)),
        out_shape=jax.ShapeDtypeStruct((k // 2, n), jnp.int32),
        compiler_params=pltpu.CompilerParams(dimension_semantics=("arbitrary",)),
        name=name,
    )(w, col_scale)


def _unpack_rows(words):
    return pltpu.bitcast(words, BF16)


def _exact_block(q_b, k_b, c_b, v_b):
    t_idx = lax.broadcasted_iota(jnp.int32, (SUBLANES, 1), 0)
    acc = jnp.zeros((SUBLANES, HEAD), F32)
    for s in range(SUBLANES):
        decay = jnp.exp(jnp.minimum(c_b - c_b[s:s + 1, :], 0.0))
        score = jnp.sum(q_b * k_b[s:s + 1, :] * decay, axis=-1, keepdims=True)
        acc = acc + jnp.where(t_idx >= s, score, 0.0) * v_b[s:s + 1, :]
    return acc


def _column_of(row):
    return jnp.transpose(jnp.broadcast_to(row, (HEAD, HEAD)))


def _mixer_kernel(x_ref, cext_ref, h0_ref, s0_ref, pvec_ref, win_ref, wri_ref, wa_ref, wb_ref, wout_ref,
                  x1_ref, cout_ref, hout_ref, sout_ref, *scratch,
                  seqs, tokens, n_tiles, reset_first, sub):
    j = pl.program_id(1)
    rows = seqs * tokens
    decode = seqs > 1
    if decode:
        ubuf, ga_s, sgb_s, sog_s, qf_h, kk_h, cum_h, v_h, tot_h, oin_h, oi_h = scratch
    else:
        ubuf, ga_s, sgb_s, sog_s, hbuf, qf_s, kk_s, cum_s, v_s, o_s, sold_s, xn_s = scratch

    pv = pvec_ref[...]

    def prow(r):
        return pv[r:r + 1, :]

    def wcol(g):
        return _unpack_rows(win_ref[:, g * D:(g + 1) * D])

    def head_cols(hd):
        return slice(hd * HEAD, (hd + 1) * HEAD)

    def front():
        x = x_ref[...].reshape(rows, D)
        xn = _rms(x, prow(P_GAIN + 0)).astype(BF16)
        row_id = lax.broadcasted_iota(jnp.int32, (rows, 1), 0)

        u = _mm(xn, wcol(C_U))
        ubuf[:, SUBLANES:SUBLANES + tokens, :] = u.reshape(seqs, tokens, D)
        if decode:
            merge_gates(xn)
        else:
            xn_s[...] = xn
        uc = prow(P_CONV_B) + prow(P_CONV_W + 3) * u
        for k in range(1, 4):
            shifted = ubuf[:, SUBLANES - k:SUBLANES - k + tokens, :].reshape(rows, D)
            uc = uc + prow(P_CONV_W + 3 - k) * shifted
        tail = ubuf[:, tokens:tokens + SUBLANES, :]
        ubuf[:, 0:SUBLANES, :] = tail
        cout_ref[...] = tail

        ucb = uc.astype(BF16)
        r_parts, i_parts = [], []
        for b in range(N_HEADS):
            ri = _mm(ucb[:, head_cols(b)], _unpack_rows(wri_ref[b]))
            r_parts.append(ri[:, :HEAD])
            i_parts.append(ri[:, HEAD:])
        r_pre = jnp.concatenate(r_parts, axis=1)
        i_pre = jnp.concatenate(i_parts, axis=1)
        if not decode:
            hbuf[...] = r_pre
            o_s[...] = i_pre
        hgrn_operands(xn)
        if not decode:
            r_pre = hbuf[...]
            i_pre = o_s[...]
        r_gate = _sigmoid_of_twice(r_pre + prow(P_RG_B))
        i_gate = _sigmoid_of_twice(i_pre + prow(P_IG_B))
        lam = prow(P_LAMBDA)
        softplus_neg_lam = jnp.maximum(-lam, 0.0) + jnp.log1p(jnp.exp(-jnp.abs(lam)))
        log_a = (-LRU_C) * r_gate * softplus_neg_lam
        a_cum = jnp.exp(log_a)
        th = jnp.tanh(log_a)
        sq = -2.0 * th / (1.0 - th)
        mult = jnp.where(sq > 0.0, sq * lax.rsqrt(sq), 0.0)
        if reset_first:
            mult = jnp.where(jnp.logical_and(row_id == 0, j == 0), 1.0, mult)
        b_cum = mult * i_gate * uc

        a_cum = a_cum.reshape(rows // SUBLANES, SUBLANES, D)
        b_cum = b_cum.reshape(rows // SUBLANES, SUBLANES, D)
        sublane = lax.broadcasted_iota(jnp.int32, (1, SUBLANES, 1), 1)
        for s in (1, 2, 4):
            keep = sublane >= s
            a_prev = jnp.where(keep, pltpu.roll(a_cum, s, 1), 1.0)
            b_prev = jnp.where(keep, pltpu.roll(b_cum, s, 1), 0.0)
            b_cum = a_cum * b_prev + b_cum
            a_cum = a_cum * a_prev
        a_cum = a_cum.reshape(rows, D)
        b_cum = b_cum.reshape(rows, D)

        if decode:
            h_in = jnp.broadcast_to(h0_ref[...], (seqs, SUBLANES, D)).reshape(rows, D)
            h_all = a_cum * h_in + b_cum
            hout_ref[...] = h_all.reshape(seqs, SUBLANES, D)[:, SUBLANES - 1:SUBLANES, :]
        else:
            h = hout_ref[0]
            for g in range(rows // SUBLANES):
                sl = slice(g * SUBLANES, (g + 1) * SUBLANES)
                hg = a_cum[sl] * h + b_cum[sl]
                hbuf[sl, :] = hg
                h = hg[SUBLANES - 1:SUBLANES, :]
            hout_ref[0] = h
            h_all = hbuf[...]

        ya = (h_all * _gelu_tanh_of_twice(_mm(xn, wcol(C_GATE)))).astype(BF16)
        ga_s[...] = _sigmoid_of_twice(_mm(xn, wcol(C_MA))) * _mm(ya, _unpack_rows(wa_ref[...]))

    def merge_gates(xn):
        sgb_s[...] = _sigmoid_of_twice(_mm(xn, wcol(C_MB)))
        sog_s[...] = _silu_of_twice(_mm(xn, wcol(C_OG)))

    def hgrn_operands(xn):
        qf = _silu_of_twice(_mm(xn, wcol(C_Q)))
        lb_raw = pv[P_LB:P_LB + 2, :]
        lb_exp = jnp.exp(lb_raw - jnp.max(lb_raw, axis=0, keepdims=True))
        lb = lb_exp[0:1, :] / jnp.sum(lb_exp, axis=0, keepdims=True)
        fg = lb + (1.0 - lb) / (1.0 + jnp.exp(-_mm(xn, wcol(C_F))))
        logf = jnp.log(fg)
        kk = 1.0 - fg
        v = _mm(xn, wcol(C_I))

        ri2 = lax.broadcasted_iota(jnp.int32, (rows, rows), 0)
        ci2 = lax.broadcasted_iota(jnp.int32, (rows, rows), 1)
        if decode:
            same_seq = (ri2 // tokens) == (ci2 // tokens)
            total = _mask_matmul(same_seq.astype(BF16), logf)
            cum = _mask_matmul(jnp.logical_and(ci2 <= ri2, same_seq).astype(BF16), logf)
            for hd in range(N_HEADS):
                hs = head_cols(hd)
                v_h[hd] = v[:, hs]
                tot_h[hd] = total[:, hs]
                qf_h[hd] = qf[:, hs]
                kk_h[hd] = kk[:, hs]
                cum_h[hd] = cum[:, hs]
        else:
            qf_s[...] = qf
            kk_s[...] = kk
            v_s[...] = v
            cum_s[...] = _mask_matmul((ci2 <= ri2).astype(BF16), logf)

    def head_norm(o_h):
        ms = jnp.mean(o_h * o_h, axis=-1, keepdims=True)
        return o_h * lax.rsqrt(ms + EPS)

    def finish(o_heads):
        o_n = jnp.concatenate([head_norm(o_h) for o_h in o_heads], axis=1) * prow(P_GNORM)
        yb = (o_n * sog_s[...]).astype(BF16)
        mixed = ga_s[...] + sgb_s[...] * _mm(yb, _unpack_rows(wb_ref[...]))
        z = _mm(mixed.astype(BF16), _unpack_rows(wout_ref[...]))
        x1 = x_ref[...].reshape(rows, D) + _rms(z, prow(P_GAIN + 1))
        x1_ref[...] = x1.reshape(seqs, tokens, D)

    def back_prompt():
        xnb = xn_s[...]
        qf = qf_s[...]
        kk = kk_s[...]
        cum = cum_s[...]
        vb = v_s[...].astype(BF16)
        n_sub = rows // sub
        last = cum[rows - 1:rows, :]
        qib = (qf * jnp.exp(cum)).astype(BF16)
        ksb = (kk * jnp.exp(last - cum)).astype(BF16)
        decay_row = jnp.exp(last)
        for hd in range(N_HEADS):
            hs = head_cols(hd)
            s_old = sout_ref[0, hd]
            parts = []
            for i in range(n_sub):
                r0 = i * sub
                r1 = r0 + sub
                width = min(rows, -(-r1 // HEAD) * HEAD)
                ref_row = cum[r0 - 1:r0, hs] if i > 0 else jnp.zeros((1, HEAD), F32)
                qd = (qf[r0:r1, hs] * jnp.exp(cum[r0:r1, hs] - ref_row)).astype(BF16)
                arg = ref_row - cum[0:width, hs]
                if width > r1:
                    arg = jnp.where(lax.broadcasted_iota(jnp.int32, (width, 1), 0) < r1, arg, 0.0)
                kdi = (kk[0:width, hs] * jnp.exp(arg)).astype(BF16)
                att = _mm_nt(qd, kdi)
                rr = lax.broadcasted_iota(jnp.int32, (sub, width), 0) + r0
                cc = lax.broadcasted_iota(jnp.int32, (sub, width), 1)
                att = jnp.where(cc <= rr, att, 0.0).astype(BF16)
                if width < rows:
                    att = jnp.concatenate([att, jnp.zeros((sub, rows - width), BF16)], axis=1)
                parts.append(att)
            att_full = jnp.concatenate(parts, axis=0) if n_sub > 1 else parts[0]
            if rows % HEAD == 0:
                o_h = _mm(jnp.concatenate([att_full, qib[:, hs]], axis=1),
                          jnp.concatenate([vb[:, hs], s_old.astype(BF16)], axis=0))
            else:
                o_h = _mm(att_full, vb[:, hs]) + _mm(qib[:, hs], s_old.astype(BF16))
            upd = _mm_tn(ksb[:, hs], vb[:, hs])
            sout_ref[0, hd] = _column_of(decay_row[:, hs]) * s_old + upd
            sold_s[hd] = s_old
            o_s[:, hs] = o_h
            quarter = D // 4
            grp, part = (C_MB, hd) if hd < 4 else (C_OG, hd - 4)
            cols = slice(part * quarter, (part + 1) * quarter)
            proj = _mm(xnb, _unpack_rows(win_ref[:, grp * D + part * quarter:grp * D + (part + 1) * quarter]))
            if hd < 4:
                sgb_s[:, cols] = _sigmoid_of_twice(proj)
            else:
                sog_s[:, cols] = _silu_of_twice(proj)

        worst = jnp.zeros((1, D), F32)
        for i in range(n_sub):
            start = cum[i * sub - 1:i * sub, :] if i > 0 else jnp.zeros((1, D), F32)
            worst = jnp.maximum(worst, start - cum[(i + 1) * sub - 1:(i + 1) * sub, :])
        out_of_range = jnp.max(worst) > SAFE_DECAY_RANGE

        @pl.when(out_of_range)
        def _exact_scores():
            row_i = lax.broadcasted_iota(jnp.int32, (rows, 1), 0)
            for hd in range(N_HEADS):
                hs = head_cols(hd)
                qib_h = (qf_s[:, hs] * jnp.exp(cum_s[:, hs])).astype(BF16)
                o_s[:, hs] = _mm(qib_h, sold_s[hd].astype(BF16))

                def block_body(b, carry, hs=hs):
                    r0 = pl.multiple_of(b * SUBLANES, SUBLANES)
                    rs = pl.ds(r0, SUBLANES)
                    q_b, k_b, c_b, v_b = qf_s[rs, hs], kk_s[rs, hs], cum_s[rs, hs], v_s[rs, hs]
                    prev_start = pl.multiple_of(jnp.maximum(r0 - SUBLANES, 0), SUBLANES)
                    before = cum_s[pl.ds(prev_start, SUBLANES), hs][SUBLANES - 1:SUBLANES, :]
                    ref_row = jnp.where(b > 0, before, 0.0)
                    qd = (q_b * jnp.exp(c_b - ref_row)).astype(BF16)
                    arg = jnp.minimum(ref_row - cum_s[:, hs], 0.0)
                    kd = jnp.where(row_i < r0, kk_s[:, hs] * jnp.exp(arg), 0.0).astype(BF16)
                    earlier = _mm(_mm_nt(qd, kd).astype(BF16), v_s[:, hs].astype(BF16))
                    o_s[rs, hs] = o_s[rs, hs] + earlier + _exact_block(q_b, k_b, c_b, v_b)
                    return carry

                lax.fori_loop(0, rows // SUBLANES, block_body, 0)

        finish([o_s[:, head_cols(hd)] for hd in range(N_HEADS)])

    def decode_state_step():
        def seq_body(g, carry):
            rsl = pl.ds(pl.multiple_of(g * SUBLANES, SUBLANES), SUBLANES)
            s_old = s0_ref[g, 0]
            c_b = cum_h[j, rsl, :]
            total_b = tot_h[j, rsl, :]
            qi = qf_h[j, rsl, :] * jnp.exp(c_b)
            ks = kk_h[j, rsl, :] * jnp.exp(total_b - c_b)
            oin_h[j, rsl, :] = _mm(qi.astype(BF16), s_old.astype(BF16))
            upd = _mm_tn(ks.astype(BF16), v_h[j, rsl, :].astype(BF16))
            decay = _column_of(jnp.exp(total_b[0:1, :]))
            sout_ref[g, 0] = decay * s_old + upd
            return carry

        lax.fori_loop(0, seqs, seq_body, 0, unroll=SUBLANES)

    def back_decode():
        ri = lax.broadcasted_iota(jnp.int32, (rows, rows), 0)
        ci = lax.broadcasted_iota(jnp.int32, (rows, rows), 1)
        causal = jnp.logical_and(ci <= ri, (ri // tokens) == (ci // tokens))
        for hd in range(N_HEADS):
            c_h = cum_h[hd]
            scores = _mm_nt((qf_h[hd] * jnp.exp(c_h)).astype(BF16), (kk_h[hd] * jnp.exp(-c_h)).astype(BF16))
            att = jnp.where(causal, scores, 0.0).astype(BF16)
            oi_h[hd] = _mm(att, v_h[hd].astype(BF16))

        worst = jnp.zeros((1, HEAD), F32)
        for hd in range(N_HEADS):
            worst = jnp.maximum(worst, jnp.max(-tot_h[hd], axis=0, keepdims=True))
        out_of_range = jnp.max(worst) > SAFE_DECAY_RANGE

        @pl.when(out_of_range)
        def _exact_scores():
            for hd in range(N_HEADS):
                def seq_body(g, carry, hd=hd):
                    rs = pl.ds(pl.multiple_of(g * SUBLANES, SUBLANES), SUBLANES)
                    oi_h[hd, rs, :] = _exact_block(qf_h[hd, rs, :], kk_h[hd, rs, :], cum_h[hd, rs, :], v_h[hd, rs, :])
                    return carry

                lax.fori_loop(0, seqs, seq_body, 0)

        finish([oin_h[hd] + oi_h[hd] for hd in range(N_HEADS)])

    if decode:
        @pl.when(j == 0)
        def _front_region():
            ubuf[:, 0:SUBLANES, :] = cext_ref[...]
            front()

        decode_state_step()

        @pl.when(j == N_HEADS - 1)
        def _back_region():
            back_decode()
    else:
        @pl.when(j == 0)
        def _init():
            ubuf[:, 0:SUBLANES, :] = cext_ref[...]
            hout_ref[...] = h0_ref[...]
            sout_ref[...] = s0_ref[...]

        front()

        @pl.when(j >= 0)
        def _back_region():
            back_prompt()


def _resident(shape):
    return pl.BlockSpec(shape, lambda i, j: (0,) * len(shape), pipeline_mode=pl.Buffered(1))


def _mixer(x, cext, h0, s0, pvec, w_in, w_ri, w_a, w_b, w_out, *, seqs, tokens, reset_first, sub, name):
    n_seq, length, _ = x.shape
    n_tiles = length // tokens
    decode = seqs > 1
    assert n_seq % seqs == 0 and length % tokens == 0 and tokens % SUBLANES == 0
    assert not decode or (n_tiles == 1 and tokens == SUBLANES)
    assert (seqs * tokens) % sub == 0
    rows = seqs * tokens
    kern = functools.partial(_mixer_kernel, seqs=seqs, tokens=tokens, n_tiles=n_tiles,
                             reset_first=reset_first, sub=sub)
    tile_f32 = pltpu.VMEM((rows, D), F32)
    by_head_f32 = pltpu.VMEM((N_HEADS, rows, HEAD), F32)
    common = [pltpu.VMEM((seqs, SUBLANES + tokens, D), F32),
              tile_f32, tile_f32, tile_f32]
    if decode:
        grid = (n_seq // seqs, N_HEADS)
        x_map = lambda i, j: (i, 0, 0)
        s_block = (seqs, 1, HEAD, HEAD)
        s_map = lambda i, j: (i, j, 0, 0)
        scratch = common + [by_head_f32] * 7
    else:
        grid = (n_seq, n_tiles)
        x_map = lambda i, j: (i, j, 0)
        s_block = (1, N_HEADS, HEAD, HEAD)
        s_map = lambda i, j: (i, 0, 0, 0)
        scratch = common + [tile_f32] * 6 + [pltpu.VMEM((N_HEADS, HEAD, HEAD), F32), pltpu.VMEM((rows, D), BF16)]
    seq_map = lambda i, j: (i, 0, 0)
    return pl.pallas_call(
        kern,
        grid=grid,
        in_specs=[
            pl.BlockSpec((seqs, tokens, D), x_map),
            pl.BlockSpec((seqs, SUBLANES, D), seq_map),
            pl.BlockSpec((seqs, 1, D), seq_map),
            pl.BlockSpec(s_block, s_map),
            _resident((P_ROWS, D)),
            _resident(w_in.shape),
            _resident(w_ri.shape),
            _resident(w_a.shape),
            _resident(w_b.shape),
            _resident(w_out.shape),
        ],
        out_specs=[
            pl.BlockSpec((seqs, tokens, D), x_map),
            pl.BlockSpec((seqs, SUBLANES, D), seq_map),
            pl.BlockSpec((seqs, 1, D), seq_map),
            pl.BlockSpec(s_block, s_map),
        ],
        out_shape=[
            jax.ShapeDtypeStruct((n_seq, length, D), F32),
            jax.ShapeDtypeStruct((n_seq, SUBLANES, D), F32),
            jax.ShapeDtypeStruct((n_seq, 1, D), F32),
            jax.ShapeDtypeStruct((n_seq, N_HEADS, HEAD, HEAD), F32),
        ],
        scratch_shapes=scratch,
        compiler_params=pltpu.CompilerParams(
            dimension_semantics=("arbitrary", "arbitrary"),
            vmem_limit_bytes=VMEM_LIMIT_BYTES),
        name=name,
    )(x, cext, h0, s0, pvec, w_in, w_ri, w_a, w_b, w_out)


def _mlp_kernel(x_ref, g_ref, wup_ref, wdn_ref, o_ref):
    x = x_ref[...]
    hn = _rms(x, g_ref[0:1, :]).astype(BF16)
    acc = jnp.zeros(x.shape, F32)
    for c in range(D_FF // D):
        t = _mm(hn, _unpack_rows(wup_ref[:, c * D:(c + 1) * D]))
        t = jnp.square(jnp.maximum(t, 0.0)).astype(BF16)
        acc = acc + _mm(t, _unpack_rows(wdn_ref[c * (D // 2):(c + 1) * (D // 2), :]))
    o_ref[...] = x + _rms(acc, g_ref[1:2, :])


def _mlp(x2d, gains, w_up, w_down, *, tile, name):
    n = x2d.shape[0]
    assert n % tile == 0
    return pl.pallas_call(
        _mlp_kernel,
        grid=(n // tile,),
        in_specs=[
            pl.BlockSpec((tile, D), lambda i: (i, 0)),
            pl.BlockSpec((2, D), lambda i: (0, 0), pipeline_mode=pl.Buffered(1)),
            pl.BlockSpec(w_up.shape, lambda i: (0, 0), pipeline_mode=pl.Buffered(1)),
            pl.BlockSpec(w_down.shape, lambda i: (0, 0), pipeline_mode=pl.Buffered(1)),
        ],
        out_specs=pl.BlockSpec((tile, D), lambda i: (i, 0)),
        out_shape=jax.ShapeDtypeStruct((n, D), F32),
        compiler_params=pltpu.CompilerParams(
            dimension_semantics=("arbitrary",),
            vmem_limit_bytes=VMEM_LIMIT_BYTES),
        name=name,
    )(x2d, gains, w_up, w_down)


def kernel(x_prompt, x_sample, state_conv, state_rglru, state_hgrn, meta_tokens, norm_gains, w_in,
           conv_w, conv_b, rg_w, rg_b, ig_w, ig_b, lru_lambda, hgrn_lb, hgrn_gnorm,
           w_branch_a, w_branch_b, w_out, w_up, w_down):
    bp, seq, _ = x_prompt.shape
    bs, dec_seq, _ = x_sample.shape
    layer = 0
    pvec = jnp.concatenate([
        norm_gains[layer], conv_w[layer], conv_b[layer][None], 0.5 * rg_b[layer][None], 0.5 * ig_b[layer][None],
        lru_lambda[layer][None], hgrn_lb[layer:layer + 2], hgrn_gnorm[layer][None],
        jnp.zeros((1, D), F32)], axis=0).astype(F32)
    halved = (C_GATE, C_Q, C_OG, C_MA, C_MB)
    scale_in = jnp.concatenate([jnp.full((1, D), 0.5 if g in halved else 1.0, F32) for g in range(8)], axis=1)

    def ones(n):
        return jnp.ones((1, n), F32)

    w_in_b = _pack_rows(w_in[layer], scale_in, name="pack_w_in")
    w_gates = jnp.concatenate([rg_w[layer], ig_w[layer]], axis=-1)
    w_ri = _pack_rows(w_gates.reshape(N_HEADS * HEAD, 2 * HEAD), 0.5 * ones(2 * HEAD), name="pack_w_gates")
    w_ri = w_ri.reshape(N_HEADS, HEAD // 2, 2 * HEAD)
    w_a = _pack_rows(w_branch_a[layer], ones(D), name="pack_w_a")
    w_b = _pack_rows(w_branch_b[layer], ones(D), name="pack_w_b")
    w_o = _pack_rows(w_out[layer], ones(D), name="pack_w_out")
    w_u = _pack_rows(w_up[layer], ones(D_FF), name="pack_w_up")
    w_d = _pack_rows(w_down[layer], ones(D), name="pack_w_down")
    weights = (pvec, w_in_b, w_ri, w_a, w_b, w_o)

    zeros_c = jnp.zeros((1, SUBLANES, D), F32)
    zeros_h = jnp.zeros((1, 1, D), F32)
    zeros_s = jnp.zeros((1, N_HEADS, HEAD, HEAD), F32)
    _, c_m, h_m, s_m = _mixer(meta_tokens[None].astype(F32), zeros_c, zeros_h, zeros_s, *weights,
                              seqs=1, tokens=N_META, reset_first=True, sub=N_META, name="mixer_meta")

    x1_p, c_p, h_p, s_p = _mixer(
        x_prompt,
        jnp.broadcast_to(c_m, (bp, SUBLANES, D)),
        jnp.broadcast_to(h_m, (bp, 1, D)),
        jnp.broadcast_to(s_m, (bp, N_HEADS, HEAD, HEAD)),
        *weights, seqs=1, tokens=256, reset_first=False, sub=64, name="mixer_prompt")

    cext_s = jnp.pad(state_conv[layer], ((0, 0), (SUBLANES - 3, 0), (0, 0)))
    x1_s, c_s, h_s, s_s = _mixer(
        x_sample, cext_s, state_rglru[layer][:, None, :], state_hgrn[layer],
        *weights, seqs=32, tokens=dec_seq, reset_first=False, sub=8, name="mixer_sample")

    gains_mlp = norm_gains[layer, 2:4]
    y_p = _mlp(x1_p.reshape(bp * seq, D), gains_mlp, w_u, w_d, tile=512, name="mlp_prompt")
    y_s = _mlp(x1_s.reshape(bs * dec_seq, D), gains_mlp, w_u, w_d, tile=512, name="mlp_sample")

    return (y_p.reshape(bp, seq, D), y_s.reshape(bs, dec_seq, D),
            c_p[:, SUBLANES - 3:, :][None], h_p[:, 0, :][None], s_p[None],
            c_s[:, SUBLANES - 3:, :][None], h_s[:, 0, :][None], s_s[None])
```

```python
import functools

import jax
import jax.numpy as jnp
from jax import lax
from jax.experimental import pallas as pl
from jax.experimental.pallas import tpu as pltpu

D = 1024
N_HEADS = 8
HEAD = 128
N_META = 16
LRU_C = 8.0
EPS = 1e-6
D_FF = 4096
SUBLANES = 8
F32 = jnp.float32
BF16 = jnp.bfloat16

P_GAIN, P_CONV_W, P_CONV_B, P_RG_B, P_IG_B, P_LAMBDA, P_LB, P_GNORM, P_ROWS = 0, 4, 8, 9, 10, 11, 12, 14, 16

C_U, C_GATE, C_Q, C_F, C_I, C_OG, C_MA, C_MB = range(8)

VMEM_LIMIT_BYTES = 60 * 1024 * 1024
PACK_BLOCK_ELEMS = 1 << 20
SAFE_DECAY_RANGE = 64.0


def _rms(x, g):
    ms = jnp.mean(x * x, axis=-1, keepdims=True)
    return x * lax.rsqrt(ms + EPS) * g


def _sigmoid_of_twice(h):
    return 0.5 * jnp.tanh(h) + 0.5


def _silu_of_twice(h):
    return h + h * jnp.tanh(h)


_GELU_C1 = 2.0 * 0.7978845608028654
_GELU_C2 = 8.0 * 0.7978845608028654 * 0.044715


def _gelu_tanh_of_twice(h):
    return h + h * jnp.tanh(h * (_GELU_C1 + _GELU_C2 * (h * h)))


def _mm(a, b):
    return jnp.dot(a, b, preferred_element_type=F32)


def _mm_nt(a, b):
    return lax.dot_general(a, b, (((1,), (1,)), ((), ())), preferred_element_type=F32)


def _mm_tn(a, b):
    return lax.dot_general(a, b, (((0,), (0,)), ((), ())), preferred_element_type=F32)


def _mask_matmul(mask_bf, x):
    hi = x.astype(BF16)
    lo = (x - hi.astype(F32)).astype(BF16)
    return _mm(mask_bf, hi) + _mm(mask_bf, lo)


def _pack_kernel(w_ref, scale_ref, o_ref):
    o_ref[...] = pltpu.bitcast((w_ref[...] * scale_ref[...]).astype(BF16), jnp.int32)


def _pack_rows(w, col_scale, *, name):
    k, n = w.shape
    tile = min(k, PACK_BLOCK_ELEMS // n)
    assert k % tile == 0 and tile % (2 * SUBLANES) == 0
    return pl.pallas_call(
        _pack_kernel,
        grid=(k // tile,),
        in_specs=[pl.BlockSpec((tile, n), lambda i: (i, 0)),
                  pl.BlockSpec((1, n), lambda i: (0, 0))],
        out_specs=pl.BlockSpec((tile // 2, n), lambda i: (i, 0)),
        out_shape=jax.ShapeDtypeStruct((k // 2, n), jnp.int32),
        compiler_params=pltpu.CompilerParams(dimension_semantics=("arbitrary",)),
        name=name,
    )(w, col_scale)


def _pack_many_kernel(*refs):
    n = len(refs) // 2
    for w_ref, o_ref in zip(refs[:n], refs[n:]):
        o_ref[...] = pltpu.bitcast(w_ref[...].astype(BF16), jnp.int32)


def _pack_rows_many(ws, *, name):
    k, n = ws[0].shape
    tile = k
    while tile * n * len(ws) > PACK_BLOCK_ELEMS:
        tile //= 2
    assert all(w.shape == (k, n) for w in ws) and k % tile == 0 and tile % (2 * SUBLANES) == 0
    return pl.pallas_call(
        _pack_many_kernel,
        grid=(k // tile,),
        in_specs=[pl.BlockSpec((tile, n), lambda i: (i, 0))] * len(ws),
        out_specs=[pl.BlockSpec((tile // 2, n), lambda i: (i, 0))] * len(ws),
        out_shape=[jax.ShapeDtypeStruct((k // 2, n), jnp.int32)] * len(ws),
        compiler_params=pltpu.CompilerParams(dimension_semantics=("arbitrary",)),
        name=name,
    )(*ws)


def _unpack_rows(words):
    return pltpu.bitcast(words, BF16)


def _exact_block(q_b, k_b, c_b, v_b):
    t_idx = lax.broadcasted_iota(jnp.int32, (SUBLANES, 1), 0)
    acc = jnp.zeros((SUBLANES, HEAD), F32)
    for s in range(SUBLANES):
        decay = jnp.exp(jnp.minimum(c_b - c_b[s:s + 1, :], 0.0))
        score = jnp.sum(q_b * k_b[s:s + 1, :] * decay, axis=-1, keepdims=True)
        acc = acc + jnp.where(t_idx >= s, score, 0.0) * v_b[s:s + 1, :]
    return acc


def _column_of(row):
    return jnp.transpose(jnp.broadcast_to(row, (HEAD, HEAD)))


def _mixer_kernel(x_ref, cext_ref, h0_ref, s0_ref, pvec_ref, win_ref, wri_ref, wa_ref, wb_ref, wout_ref,
                  x1_ref, cout_ref, hout_ref, sout_ref, *scratch,
                  seqs, tokens, n_tiles, reset_first, sub):
    j = pl.program_id(1)
    rows = seqs * tokens
    decode = seqs > 1
    if decode:
        ubuf, ga_s, sgb_s, sog_s, qf_h, kk_h, cum_h, v_h, tot_h, oin_h, oi_h = scratch
    else:
        ubuf, ga_s, sgb_s, sog_s, hbuf, qf_s, kk_s, cum_s, v_s, o_s, sold_s, xn_s = scratch

    pv = pvec_ref[...]

    def prow(r):
        return pv[r:r + 1, :]

    def wcol(g):
        return _unpack_rows(win_ref[:, g * D:(g + 1) * D])

    def head_cols(hd):
        return slice(hd * HEAD, (hd + 1) * HEAD)

    def front():
        x = x_ref[...].reshape(rows, D)
        xn = _rms(x, prow(P_GAIN + 0)).astype(BF16)
        row_id = lax.broadcasted_iota(jnp.int32, (rows, 1), 0)

        u = _mm(xn, wcol(C_U))
        ubuf[:, SUBLANES:SUBLANES + tokens, :] = u.reshape(seqs, tokens, D)
        if decode:
            merge_gates(xn)
        else:
            xn_s[...] = xn
        uc = prow(P_CONV_B) + prow(P_CONV_W + 3) * u
        for k in range(1, 4):
            shifted = ubuf[:, SUBLANES - k:SUBLANES - k + tokens, :].reshape(rows, D)
            uc = uc + prow(P_CONV_W + 3 - k) * shifted
        tail = ubuf[:, tokens:tokens + SUBLANES, :]
        ubuf[:, 0:SUBLANES, :] = tail
        cout_ref[...] = tail

        ucb = uc.astype(BF16)
        r_parts, i_parts = [], []
        for b in range(N_HEADS):
            ri = _mm(ucb[:, head_cols(b)], _unpack_rows(wri_ref[b]))
            r_parts.append(ri[:, :HEAD])
            i_parts.append(ri[:, HEAD:])
        r_pre = jnp.concatenate(r_parts, axis=1)
        i_pre = jnp.concatenate(i_parts, axis=1)
        if not decode:
            hbuf[...] = r_pre
            o_s[...] = i_pre
        hgrn_operands(xn)
        if not decode:
            r_pre = hbuf[...]
            i_pre = o_s[...]
        r_gate = _sigmoid_of_twice(r_pre + prow(P_RG_B))
        i_gate = _sigmoid_of_twice(i_pre + prow(P_IG_B))
        lam = prow(P_LAMBDA)
        softplus_neg_lam = jnp.maximum(-lam, 0.0) + jnp.log1p(jnp.exp(-jnp.abs(lam)))
        log_a = (-LRU_C) * r_gate * softplus_neg_lam
        a_cum = jnp.exp(log_a)
        th = jnp.tanh(log_a)
        sq = -2.0 * th / (1.0 - th)
        mult = jnp.where(sq > 0.0, sq * lax.rsqrt(sq), 0.0)
        if reset_first:
            mult = jnp.where(jnp.logical_and(row_id == 0, j == 0), 1.0, mult)
        b_cum = mult * i_gate * uc

        a_cum = a_cum.reshape(rows // SUBLANES, SUBLANES, D)
        b_cum = b_cum.reshape(rows // SUBLANES, SUBLANES, D)
        sublane = lax.broadcasted_iota(jnp.int32, (1, SUBLANES, 1), 1)
        for s in (1, 2, 4):
            keep = sublane >= s
            a_prev = jnp.where(keep, pltpu.roll(a_cum, s, 1), 1.0)
            b_prev = jnp.where(keep, pltpu.roll(b_cum, s, 1), 0.0)
            b_cum = a_cum * b_prev + b_cum
            a_cum = a_cum * a_prev
        a_cum = a_cum.reshape(rows, D)
        b_cum = b_cum.reshape(rows, D)

        if decode:
            h_in = jnp.broadcast_to(h0_ref[...], (seqs, SUBLANES, D)).reshape(rows, D)
            h_all = a_cum * h_in + b_cum
            hout_ref[...] = h_all.reshape(seqs, SUBLANES, D)[:, SUBLANES - 1:SUBLANES, :]
        else:
            h = hout_ref[0]
            for g in range(rows // SUBLANES):
                sl = slice(g * SUBLANES, (g + 1) * SUBLANES)
                hg = a_cum[sl] * h + b_cum[sl]
                hbuf[sl, :] = hg
                h = hg[SUBLANES - 1:SUBLANES, :]
            hout_ref[0] = h
            h_all = hbuf[...]

        ya = (h_all * _gelu_tanh_of_twice(_mm(xn, wcol(C_GATE)))).astype(BF16)
        ga_s[...] = _sigmoid_of_twice(_mm(xn, wcol(C_MA))) * _mm(ya, _unpack_rows(wa_ref[...]))

    def merge_gates(xn):
        sgb_s[...] = _sigmoid_of_twice(_mm(xn, wcol(C_MB)))
        sog_s[...] = _silu_of_twice(_mm(xn, wcol(C_OG)))

    def hgrn_operands(xn):
        qf = _silu_of_twice(_mm(xn, wcol(C_Q)))
        lb_raw = pv[P_LB:P_LB + 2, :]
        lb_exp = jnp.exp(lb_raw - jnp.max(lb_raw, axis=0, keepdims=True))
        lb = lb_exp[0:1, :] / jnp.sum(lb_exp, axis=0, keepdims=True)
        fg = lb + (1.0 - lb) / (1.0 + jnp.exp(-_mm(xn, wcol(C_F))))
        logf = jnp.log(fg)
        kk = 1.0 - fg
        v = _mm(xn, wcol(C_I))

        ri2 = lax.broadcasted_iota(jnp.int32, (rows, rows), 0)
        ci2 = lax.broadcasted_iota(jnp.int32, (rows, rows), 1)
        if decode:
            same_seq = (ri2 // tokens) == (ci2 // tokens)
            total = _mask_matmul(same_seq.astype(BF16), logf)
            cum = _mask_matmul(jnp.logical_and(ci2 <= ri2, same_seq).astype(BF16), logf)
            for hd in range(N_HEADS):
                hs = head_cols(hd)
                v_h[hd] = v[:, hs]
                tot_h[hd] = total[:, hs]
                qf_h[hd] = qf[:, hs]
                kk_h[hd] = kk[:, hs]
                cum_h[hd] = cum[:, hs]
        else:
            qf_s[...] = qf
            kk_s[...] = kk
            v_s[...] = v
            cum_s[...] = _mask_matmul((ci2 <= ri2).astype(BF16), logf)

    def head_norm(o_h):
        ms = jnp.mean(o_h * o_h, axis=-1, keepdims=True)
        return o_h * lax.rsqrt(ms + EPS)

    def finish(o_heads):
        o_n = jnp.concatenate([head_norm(o_h) for o_h in o_heads], axis=1) * prow(P_GNORM)
        yb = (o_n * sog_s[...]).astype(BF16)
        mixed = ga_s[...] + sgb_s[...] * _mm(yb, _unpack_rows(wb_ref[...]))
        z = _mm(mixed.astype(BF16), _unpack_rows(wout_ref[...]))
        x1 = x_ref[...].reshape(rows, D) + _rms(z, prow(P_GAIN + 1))
        x1_ref[...] = x1.reshape(seqs, tokens, D)

    def back_prompt():
        xnb = xn_s[...]
        qf = qf_s[...]
        kk = kk_s[...]
        cum = cum_s[...]
        vb = v_s[...].astype(BF16)
        n_sub = rows // sub
        last = cum[rows - 1:rows, :]
        qib = (qf * jnp.exp(cum)).astype(BF16)
        ksb = (kk * jnp.exp(last - cum)).astype(BF16)
        decay_row = jnp.exp(last)
        for hd in range(N_HEADS):
            hs = head_cols(hd)
            s_old = sout_ref[0, hd]
            parts = []
            for i in range(n_sub):
                r0 = i * sub
                r1 = r0 + sub
                width = min(rows, -(-r1 // HEAD) * HEAD)
                ref_row = cum[r0 - 1:r0, hs] if i > 0 else jnp.zeros((1, HEAD), F32)
                qd = (qf[r0:r1, hs] * jnp.exp(cum[r0:r1, hs] - ref_row)).astype(BF16)
                arg = ref_row - cum[0:width, hs]
                if width > r1:
                    arg = jnp.where(lax.broadcasted_iota(jnp.int32, (width, 1), 0) < r1, arg, 0.0)
                kdi = (kk[0:width, hs] * jnp.exp(arg)).astype(BF16)
                att = _mm_nt(qd, kdi)
                rr = lax.broadcasted_iota(jnp.int32, (sub, width), 0) + r0
                cc = lax.broadcasted_iota(jnp.int32, (sub, width), 1)
                att = jnp.where(cc <= rr, att, 0.0).astype(BF16)
                if width < rows:
                    att = jnp.concatenate([att, jnp.zeros((sub, rows - width), BF16)], axis=1)
                parts.append(att)
            att_full = jnp.concatenate(parts, axis=0) if n_sub > 1 else parts[0]
            if rows % HEAD == 0:
                o_h = _mm(jnp.concatenate([att_full, qib[:, hs]], axis=1),
                          jnp.concatenate([vb[:, hs], s_old.astype(BF16)], axis=0))
            else:
                o_h = _mm(att_full, vb[:, hs]) + _mm(qib[:, hs], s_old.astype(BF16))
            upd = _mm_tn(ksb[:, hs], vb[:, hs])
            sout_ref[0, hd] = _column_of(decay_row[:, hs]) * s_old + upd
            sold_s[hd] = s_old
            o_s[:, hs] = o_h
            quarter = D // 4
            grp, part = (C_MB, hd) if hd < 4 else (C_OG, hd - 4)
            cols = slice(part * quarter, (part + 1) * quarter)
            proj = _mm(xnb, _unpack_rows(win_ref[:, grp * D + part * quarter:grp * D + (part + 1) * quarter]))
            if hd < 4:
                sgb_s[:, cols] = _sigmoid_of_twice(proj)
            else:
                sog_s[:, cols] = _silu_of_twice(proj)

        worst = jnp.zeros((1, D), F32)
        for i in range(n_sub):
            start = cum[i * sub - 1:i * sub, :] if i > 0 else jnp.zeros((1, D), F32)
            worst = jnp.maximum(worst, start - cum[(i + 1) * sub - 1:(i + 1) * sub, :])
        out_of_range = jnp.max(worst) > SAFE_DECAY_RANGE

        @pl.when(out_of_range)
        def _exact_scores():
            row_i = lax.broadcasted_iota(jnp.int32, (rows, 1), 0)
            for hd in range(N_HEADS):
                hs = head_cols(hd)
                qib_h = (qf_s[:, hs] * jnp.exp(cum_s[:, hs])).astype(BF16)
                o_s[:, hs] = _mm(qib_h, sold_s[hd].astype(BF16))

                def block_body(b, carry, hs=hs):
                    r0 = pl.multiple_of(b * SUBLANES, SUBLANES)
                    rs = pl.ds(r0, SUBLANES)
                    q_b, k_b, c_b, v_b = qf_s[rs, hs], kk_s[rs, hs], cum_s[rs, hs], v_s[rs, hs]
                    prev_start = pl.multiple_of(jnp.maximum(r0 - SUBLANES, 0), SUBLANES)
                    before = cum_s[pl.ds(prev_start, SUBLANES), hs][SUBLANES - 1:SUBLANES, :]
                    ref_row = jnp.where(b > 0, before, 0.0)
                    qd = (q_b * jnp.exp(c_b - ref_row)).astype(BF16)
                    arg = jnp.minimum(ref_row - cum_s[:, hs], 0.0)
                    kd = jnp.where(row_i < r0, kk_s[:, hs] * jnp.exp(arg), 0.0).astype(BF16)
                    earlier = _mm(_mm_nt(qd, kd).astype(BF16), v_s[:, hs].astype(BF16))
                    o_s[rs, hs] = o_s[rs, hs] + earlier + _exact_block(q_b, k_b, c_b, v_b)
                    return carry

                lax.fori_loop(0, rows // SUBLANES, block_body, 0)

        finish([o_s[:, head_cols(hd)] for hd in range(N_HEADS)])

    def decode_state_step():
        def seq_body(g, carry):
            rsl = pl.ds(pl.multiple_of(g * SUBLANES, SUBLANES), SUBLANES)
            s_old = s0_ref[g, 0]
            c_b = cum_h[j, rsl, :]
            total_b = tot_h[j, rsl, :]
            qi = qf_h[j, rsl, :] * jnp.exp(c_b)
            ks = kk_h[j, rsl, :] * jnp.exp(total_b - c_b)
            oin_h[j, rsl, :] = _mm(qi.astype(BF16), s_old.astype(BF16))
            upd = _mm_tn(ks.astype(BF16), v_h[j, rsl, :].astype(BF16))
            decay = _column_of(jnp.exp(total_b[0:1, :]))
            sout_ref[g, 0] = decay * s_old + upd
            return carry

        lax.fori_loop(0, seqs, seq_body, 0, unroll=SUBLANES)

    def back_decode():
        ri = lax.broadcasted_iota(jnp.int32, (rows, rows), 0)
        ci = lax.broadcasted_iota(jnp.int32, (rows, rows), 1)
        causal = jnp.logical_and(ci <= ri, (ri // tokens) == (ci // tokens))
        for hd in range(N_HEADS):
            c_h = cum_h[hd]
            scores = _mm_nt((qf_h[hd] * jnp.exp(c_h)).astype(BF16), (kk_h[hd] * jnp.exp(-c_h)).astype(BF16))
            att = jnp.where(causal, scores, 0.0).astype(BF16)
            oi_h[hd] = _mm(att, v_h[hd].astype(BF16))

        worst = jnp.zeros((1, HEAD), F32)
        for hd in range(N_HEADS):
            worst = jnp.maximum(worst, jnp.max(-tot_h[hd], axis=0, keepdims=True))
        out_of_range = jnp.max(worst) > SAFE_DECAY_RANGE

        @pl.when(out_of_range)
        def _exact_scores():
            for hd in range(N_HEADS):
                def seq_body(g, carry, hd=hd):
                    rs = pl.ds(pl.multiple_of(g * SUBLANES, SUBLANES), SUBLANES)
                    oi_h[hd, rs, :] = _exact_block(qf_h[hd, rs, :], kk_h[hd, rs, :], cum_h[hd, rs, :], v_h[hd, rs, :])
                    return carry

                lax.fori_loop(0, seqs, seq_body, 0)

        finish([oin_h[hd] + oi_h[hd] for hd in range(N_HEADS)])

    if decode:
        @pl.when(j == 0)
        def _front_region():
            ubuf[:, 0:SUBLANES, :] = cext_ref[...]
            front()

        decode_state_step()

        @pl.when(j == N_HEADS - 1)
        def _back_region():
            back_decode()
    else:
        @pl.when(j == 0)
        def _init():
            ubuf[:, 0:SUBLANES, :] = cext_ref[...]
            hout_ref[...] = h0_ref[...]
            sout_ref[...] = s0_ref[...]

        front()

        @pl.when(j >= 0)
        def _back_region():
            back_prompt()


def _resident(shape):
    return pl.BlockSpec(shape, lambda i, j: (0,) * len(shape), pipeline_mode=pl.Buffered(1))


def _mixer(x, cext, h0, s0, pvec, w_in, w_ri, w_a, w_b, w_out, *, seqs, tokens, reset_first, sub, name):
    n_seq, length, _ = x.shape
    n_tiles = length // tokens
    decode = seqs > 1
    assert n_seq % seqs == 0 and length % tokens == 0 and tokens % SUBLANES == 0
    assert not decode or (n_tiles == 1 and tokens == SUBLANES)
    assert (seqs * tokens) % sub == 0
    rows = seqs * tokens
    kern = functools.partial(_mixer_kernel, seqs=seqs, tokens=tokens, n_tiles=n_tiles,
                             reset_first=reset_first, sub=sub)
    tile_f32 = pltpu.VMEM((rows, D), F32)
    by_head_f32 = pltpu.VMEM((N_HEADS, rows, HEAD), F32)
    common = [pltpu.VMEM((seqs, SUBLANES + tokens, D), F32),
              tile_f32, tile_f32, tile_f32]
    if decode:
        grid = (n_seq // seqs, N_HEADS)
        x_map = lambda i, j: (i, 0, 0)
        s_block = (seqs, 1, HEAD, HEAD)
        s_map = lambda i, j: (i, j, 0, 0)
        scratch = common + [by_head_f32] * 7
    else:
        grid = (n_seq, n_tiles)
        x_map = lambda i, j: (i, j, 0)
        s_block = (1, N_HEADS, HEAD, HEAD)
        s_map = lambda i, j: (i, 0, 0, 0)
        scratch = common + [tile_f32] * 6 + [pltpu.VMEM((N_HEADS, HEAD, HEAD), F32), pltpu.VMEM((rows, D), BF16)]
    seq_map = lambda i, j: (i, 0, 0)
    if cext.shape[0] == n_seq:
        in_seq_map, in_s_map = seq_map, s_map
    else:
        assert cext.shape[0] == h0.shape[0] == s0.shape[0] == seqs == 1
        in_seq_map = lambda i, j: (0, 0, 0)
        in_s_map = lambda i, j: (0, 0, 0, 0)
    return pl.pallas_call(
        kern,
        grid=grid,
        in_specs=[
            pl.BlockSpec((seqs, tokens, D), x_map),
            pl.BlockSpec((seqs, SUBLANES, D), in_seq_map),
            pl.BlockSpec((seqs, 1, D), in_seq_map),
            pl.BlockSpec(s_block, in_s_map),
            _resident((P_ROWS, D)),
            _resident(w_in.shape),
            _resident(w_ri.shape),
            _resident(w_a.shape),
            _resident(w_b.shape),
            _resident(w_out.shape),
        ],
        out_specs=[
            pl.BlockSpec((seqs, tokens, D), x_map),
            pl.BlockSpec((seqs, SUBLANES, D), seq_map),
            pl.BlockSpec((seqs, 1, D), seq_map),
            pl.BlockSpec(s_block, s_map),
        ],
        out_shape=[
            jax.ShapeDtypeStruct((n_seq, length, D), F32),
            jax.ShapeDtypeStruct((n_seq, SUBLANES, D), F32),
            jax.ShapeDtypeStruct((n_seq, 1, D), F32),
            jax.ShapeDtypeStruct((n_seq, N_HEADS, HEAD, HEAD), F32),
        ],
        scratch_shapes=scratch,
        compiler_params=pltpu.CompilerParams(
            dimension_semantics=("arbitrary", "arbitrary"),
            vmem_limit_bytes=VMEM_LIMIT_BYTES),
        name=name,
    )(x, cext, h0, s0, pvec, w_in, w_ri, w_a, w_b, w_out)


def _mlp_kernel(x_ref, g_ref, wup_ref, wdn_ref, o_ref):
    x = x_ref[...]
    hn = _rms(x, g_ref[0:1, :]).astype(BF16)
    acc = jnp.zeros(x.shape, F32)
    for c in range(D_FF // D):
        t = _mm(hn, _unpack_rows(wup_ref[:, c * D:(c + 1) * D]))
        t = jnp.square(jnp.maximum(t, 0.0)).astype(BF16)
        acc = acc + _mm(t, _unpack_rows(wdn_ref[c * (D // 2):(c + 1) * (D // 2), :]))
    o_ref[...] = x + _rms(acc, g_ref[1:2, :])


def _mlp(x2d, gains, w_up, w_down, *, tile, name):
    n = x2d.shape[0]
    assert n % tile == 0
    return pl.pallas_call(
        _mlp_kernel,
        grid=(n // tile,),
        in_specs=[
            pl.BlockSpec((tile, D), lambda i: (i, 0)),
            pl.BlockSpec((2, D), lambda i: (0, 0), pipeline_mode=pl.Buffered(1)),
            pl.BlockSpec(w_up.shape, lambda i: (0, 0), pipeline_mode=pl.Buffered(1)),
            pl.BlockSpec(w_down.shape, lambda i: (0, 0), pipeline_mode=pl.Buffered(1)),
        ],
        out_specs=pl.BlockSpec((tile, D), lambda i: (i, 0)),
        out_shape=jax.ShapeDtypeStruct((n, D), F32),
        compiler_params=pltpu.CompilerParams(
            dimension_semantics=("arbitrary",),
            vmem_limit_bytes=VMEM_LIMIT_BYTES),
        name=name,
    )(x2d, gains, w_up, w_down)


def kernel(x_prompt, x_sample, state_conv, state_rglru, state_hgrn, meta_tokens, norm_gains, w_in,
           conv_w, conv_b, rg_w, rg_b, ig_w, ig_b, lru_lambda, hgrn_lb, hgrn_gnorm,
           w_branch_a, w_branch_b, w_out, w_up, w_down):
    bp, seq, _ = x_prompt.shape
    bs, dec_seq, _ = x_sample.shape
    layer = 0
    pvec = jnp.concatenate([
        norm_gains[layer], conv_w[layer], conv_b[layer][None], 0.5 * rg_b[layer][None], 0.5 * ig_b[layer][None],
        lru_lambda[layer][None], hgrn_lb[layer:layer + 2], hgrn_gnorm[layer][None],
        jnp.zeros((1, D), F32)], axis=0).astype(F32)
    halved = (C_GATE, C_Q, C_OG, C_MA, C_MB)
    scale_in = jnp.concatenate([jnp.full((1, D), 0.5 if g in halved else 1.0, F32) for g in range(8)], axis=1)

    def ones(n):
        return jnp.ones((1, n), F32)

    w_in_b = _pack_rows(w_in[layer], scale_in, name="pack_w_in")
    w_gates = jnp.concatenate([rg_w[layer], ig_w[layer]], axis=-1)
    w_ri = _pack_rows(w_gates.reshape(N_HEADS * HEAD, 2 * HEAD), 0.5 * ones(2 * HEAD), name="pack_w_gates")
    w_ri = w_ri.reshape(N_HEADS, HEAD // 2, 2 * HEAD)
    w_a, w_b, w_o = _pack_rows_many([w_branch_a[layer], w_branch_b[layer], w_out[layer]], name="pack_w_square")
    w_u = _pack_rows(w_up[layer], ones(D_FF), name="pack_w_up")
    w_d = _pack_rows(w_down[layer], ones(D), name="pack_w_down")
    weights = (pvec, w_in_b, w_ri, w_a, w_b, w_o)

    zeros_c = jnp.zeros((1, SUBLANES, D), F32)
    zeros_h = jnp.zeros((1, 1, D), F32)
    zeros_s = jnp.zeros((1, N_HEADS, HEAD, HEAD), F32)
    _, c_m, h_m, s_m = _mixer(meta_tokens[None].astype(F32), zeros_c, zeros_h, zeros_s, *weights,
                              seqs=1, tokens=N_META, reset_first=True, sub=N_META, name="mixer_meta")

    x1_p, c_p, h_p, s_p = _mixer(
        x_prompt,
        c_m, h_m, s_m,
        *weights, seqs=1, tokens=256, reset_first=False, sub=64, name="mixer_prompt")

    cext_s = jnp.pad(state_conv[layer], ((0, 0), (SUBLANES - 3, 0), (0, 0)))
    x1_s, c_s, h_s, s_s = _mixer(
        x_sample, cext_s, state_rglru[layer][:, None, :], state_hgrn[layer],
        *weights, seqs=32, tokens=dec_seq, reset_first=False, sub=8, name="mixer_sample")

    gains_mlp = norm_gains[layer, 2:4]
    y_p = _mlp(x1_p.reshape(bp * seq, D), gains_mlp, w_u, w_d, tile=1024, name="mlp_prompt")
    y_s = _mlp(x1_s.reshape(bs * dec_seq, D), gains_mlp, w_u, w_d, tile=1024, name="mlp_sample")

    return (y_p.reshape(bp, seq, D), y_s.reshape(bs, dec_seq, D),
            c_p[:, SUBLANES - 3:, :][None], h_p[:, 0, :][None], s_p[None],
            c_s[:, SUBLANES - 3:, :][None], h_s[:, 0, :][None], s_s[None])
```

```python
import functools

import jax
import jax.numpy as jnp
from jax import lax
from jax.experimental import pallas as pl
from jax.experimental.pallas import tpu as pltpu

D = 1024
N_HEADS = 8
HEAD = 128
N_META = 16
LRU_C = 8.0
EPS = 1e-6
D_FF = 4096
SUBLANES = 8
F32 = jnp.float32
BF16 = jnp.bfloat16

P_GAIN, P_CONV_W, P_CONV_B, P_RG_B, P_IG_B, P_LAMBDA, P_LB, P_GNORM, P_ROWS = 0, 4, 8, 9, 10, 11, 12, 14, 16

C_U, C_GATE, C_Q, C_F, C_I, C_OG, C_MA, C_MB = range(8)

VMEM_LIMIT_BYTES = 60 * 1024 * 1024
PACK_BLOCK_ELEMS = 1 << 20
SAFE_DECAY_RANGE = 64.0


def _rms(x, g):
    ms = jnp.mean(x * x, axis=-1, keepdims=True)
    return x * lax.rsqrt(ms + EPS) * g


def _sigmoid_of_twice(h):
    return 0.5 * jnp.tanh(h) + 0.5


def _silu_of_twice(h):
    return h + h * jnp.tanh(h)


_GELU_C1 = 2.0 * 0.7978845608028654
_GELU_C2 = 8.0 * 0.7978845608028654 * 0.044715


def _gelu_tanh_of_twice(h):
    return h + h * jnp.tanh(h * (_GELU_C1 + _GELU_C2 * (h * h)))


def _mm(a, b):
    return jnp.dot(a, b, preferred_element_type=F32)


def _mm_nt(a, b):
    return lax.dot_general(a, b, (((1,), (1,)), ((), ())), preferred_element_type=F32)


def _mm_tn(a, b):
    return lax.dot_general(a, b, (((0,), (0,)), ((), ())), preferred_element_type=F32)


def _mask_matmul(mask_bf, x):
    hi = x.astype(BF16)
    lo = (x - hi.astype(F32)).astype(BF16)
    return _mm(mask_bf, hi) + _mm(mask_bf, lo)


def _pack_kernel(w_ref, scale_ref, o_ref):
    o_ref[...] = pltpu.bitcast((w_ref[...] * scale_ref[...]).astype(BF16), jnp.int32)


def _pack_rows(w, col_scale, *, name):
    k, n = w.shape
    tile = min(k, PACK_BLOCK_ELEMS // n)
    assert k % tile == 0 and tile % (2 * SUBLANES) == 0
    return pl.pallas_call(
        _pack_kernel,
        grid=(k // tile,),
        in_specs=[pl.BlockSpec((tile, n), lambda i: (i, 0)),
                  pl.BlockSpec((1, n), lambda i: (0, 0))],
        out_specs=pl.BlockSpec((tile // 2, n), lambda i: (i, 0)),
        out_shape=jax.ShapeDtypeStruct((k // 2, n), jnp.int32),
        compiler_params=pltpu.CompilerParams(dimension_semantics=("arbitrary",)),
        name=name,
    )(w, col_scale)


def _pack_many_kernel(*refs):
    n = len(refs) // 2
    for w_ref, o_ref in zip(refs[:n], refs[n:]):
        o_ref[...] = pltpu.bitcast(w_ref[...].astype(BF16), jnp.int32)


def _pack_rows_many(ws, *, name):
    k, n = ws[0].shape
    tile = k
    while tile * n * len(ws) > PACK_BLOCK_ELEMS:
        tile //= 2
    assert all(w.shape == (k, n) for w in ws) and k % tile == 0 and tile % (2 * SUBLANES) == 0
    return pl.pallas_call(
        _pack_many_kernel,
        grid=(k // tile,),
        in_specs=[pl.BlockSpec((tile, n), lambda i: (i, 0))] * len(ws),
        out_specs=[pl.BlockSpec((tile // 2, n), lambda i: (i, 0))] * len(ws),
        out_shape=[jax.ShapeDtypeStruct((k // 2, n), jnp.int32)] * len(ws),
        compiler_params=pltpu.CompilerParams(dimension_semantics=("arbitrary",)),
        name=name,
    )(*ws)


def _unpack_rows(words):
    return pltpu.bitcast(words, BF16)


def _exact_block(q_b, k_b, c_b, v_b):
    t_idx = lax.broadcasted_iota(jnp.int32, (SUBLANES, 1), 0)
    acc = jnp.zeros((SUBLANES, HEAD), F32)
    for s in range(SUBLANES):
        decay = jnp.exp(jnp.minimum(c_b - c_b[s:s + 1, :], 0.0))
        score = jnp.sum(q_b * k_b[s:s + 1, :] * decay, axis=-1, keepdims=True)
        acc = acc + jnp.where(t_idx >= s, score, 0.0) * v_b[s:s + 1, :]
    return acc


def _column_of(row):
    return jnp.transpose(jnp.broadcast_to(row, (HEAD, HEAD)))


def _mixer_kernel(x_ref, cext_ref, h0_ref, s0_ref, pvec_ref, win_ref, wri_ref, wa_ref, wb_ref, wout_ref,
                  x1_ref, cout_ref, hout_ref, sout_ref, *scratch,
                  seqs, tokens, n_tiles, reset_first, sub):
    j = pl.program_id(1)
    rows = seqs * tokens
    decode = seqs > 1
    if decode:
        ubuf, ga_s, sgb_s, sog_s, qf_h, kk_h, cum_h, v_h, tot_h, oin_h, oi_h = scratch
    else:
        ubuf, ga_s, sgb_s, sog_s, hbuf, qf_s, kk_s, cum_s, v_s, o_s, sold_s, xn_s = scratch

    pv = pvec_ref[...]

    def prow(r):
        return pv[r:r + 1, :]

    def wcol(g):
        return _unpack_rows(win_ref[:, g * D:(g + 1) * D])

    def head_cols(hd):
        return slice(hd * HEAD, (hd + 1) * HEAD)

    def front():
        x = x_ref[...].reshape(rows, D)
        xn = _rms(x, prow(P_GAIN + 0)).astype(BF16)
        row_id = lax.broadcasted_iota(jnp.int32, (rows, 1), 0)

        u = _mm(xn, wcol(C_U))
        ubuf[:, SUBLANES:SUBLANES + tokens, :] = u.reshape(seqs, tokens, D)
        if decode:
            merge_gates(xn)
        else:
            xn_s[...] = xn
        uc = prow(P_CONV_B) + prow(P_CONV_W + 3) * u
        for k in range(1, 4):
            shifted = ubuf[:, SUBLANES - k:SUBLANES - k + tokens, :].reshape(rows, D)
            uc = uc + prow(P_CONV_W + 3 - k) * shifted
        tail = ubuf[:, tokens:tokens + SUBLANES, :]
        ubuf[:, 0:SUBLANES, :] = tail
        cout_ref[...] = tail

        ucb = uc.astype(BF16)
        r_parts, i_parts = [], []
        for b in range(N_HEADS):
            ri = _mm(ucb[:, head_cols(b)], _unpack_rows(wri_ref[b]))
            r_parts.append(ri[:, :HEAD])
            i_parts.append(ri[:, HEAD:])
        r_pre = jnp.concatenate(r_parts, axis=1)
        i_pre = jnp.concatenate(i_parts, axis=1)
        if not decode:
            hbuf[...] = r_pre
            o_s[...] = i_pre
        hgrn_operands(xn)
        if not decode:
            r_pre = hbuf[...]
            i_pre = o_s[...]
        r_gate = _sigmoid_of_twice(r_pre + prow(P_RG_B))
        i_gate = _sigmoid_of_twice(i_pre + prow(P_IG_B))
        lam = prow(P_LAMBDA)
        softplus_neg_lam = jnp.maximum(-lam, 0.0) + jnp.log1p(jnp.exp(-jnp.abs(lam)))
        log_a = (-LRU_C) * r_gate * softplus_neg_lam
        a_cum = jnp.exp(log_a)
        th = jnp.tanh(log_a)
        sq = -2.0 * th / (1.0 - th)
        mult = jnp.where(sq > 0.0, sq * lax.rsqrt(sq), 0.0)
        if reset_first:
            mult = jnp.where(jnp.logical_and(row_id == 0, j == 0), 1.0, mult)
        b_cum = mult * i_gate * uc

        a_cum = a_cum.reshape(rows // SUBLANES, SUBLANES, D)
        b_cum = b_cum.reshape(rows // SUBLANES, SUBLANES, D)
        sublane = lax.broadcasted_iota(jnp.int32, (1, SUBLANES, 1), 1)
        for s in (1, 2, 4):
            keep = sublane >= s
            a_prev = jnp.where(keep, pltpu.roll(a_cum, s, 1), 1.0)
            b_prev = jnp.where(keep, pltpu.roll(b_cum, s, 1), 0.0)
            b_cum = a_cum * b_prev + b_cum
            a_cum = a_cum * a_prev
        a_cum = a_cum.reshape(rows, D)
        b_cum = b_cum.reshape(rows, D)

        if decode:
            h_in = jnp.broadcast_to(h0_ref[...], (seqs, SUBLANES, D)).reshape(rows, D)
            h_all = a_cum * h_in + b_cum
            hout_ref[...] = h_all.reshape(seqs, SUBLANES, D)[:, SUBLANES - 1:SUBLANES, :]
        else:
            h = hout_ref[0]
            for g in range(rows // SUBLANES):
                sl = slice(g * SUBLANES, (g + 1) * SUBLANES)
                hg = a_cum[sl] * h + b_cum[sl]
                hbuf[sl, :] = hg
                h = hg[SUBLANES - 1:SUBLANES, :]
            hout_ref[0] = h
            h_all = hbuf[...]

        ya = (h_all * _gelu_tanh_of_twice(_mm(xn, wcol(C_GATE)))).astype(BF16)
        ga_s[...] = _sigmoid_of_twice(_mm(xn, wcol(C_MA))) * _mm(ya, _unpack_rows(wa_ref[...]))

    def merge_gates(xn):
        sgb_s[...] = _sigmoid_of_twice(_mm(xn, wcol(C_MB)))
        sog_s[...] = _silu_of_twice(_mm(xn, wcol(C_OG)))

    def hgrn_operands(xn):
        qf = _silu_of_twice(_mm(xn, wcol(C_Q)))
        lb_raw = pv[P_LB:P_LB + 2, :]
        lb_exp = jnp.exp(lb_raw - jnp.max(lb_raw, axis=0, keepdims=True))
        lb = lb_exp[0:1, :] / jnp.sum(lb_exp, axis=0, keepdims=True)
        fg = lb + (1.0 - lb) / (1.0 + jnp.exp(-_mm(xn, wcol(C_F))))
        logf = jnp.log(fg)
        kk = 1.0 - fg
        v = _mm(xn, wcol(C_I))

        ri2 = lax.broadcasted_iota(jnp.int32, (rows, rows), 0)
        ci2 = lax.broadcasted_iota(jnp.int32, (rows, rows), 1)
        if decode:
            same_seq = (ri2 // tokens) == (ci2 // tokens)
            total = _mask_matmul(same_seq.astype(BF16), logf)
            cum = _mask_matmul(jnp.logical_and(ci2 <= ri2, same_seq).astype(BF16), logf)
            for hd in range(N_HEADS):
                hs = head_cols(hd)
                v_h[hd] = v[:, hs]
                tot_h[hd] = total[:, hs]
                qf_h[hd] = qf[:, hs]
                kk_h[hd] = kk[:, hs]
                cum_h[hd] = cum[:, hs]
        else:
            qf_s[...] = qf
            kk_s[...] = kk
            v_s[...] = v
            cum_s[...] = _mask_matmul((ci2 <= ri2).astype(BF16), logf)

    def head_norm(o_h):
        ms = jnp.mean(o_h * o_h, axis=-1, keepdims=True)
        return o_h * lax.rsqrt(ms + EPS)

    def finish(o_heads):
        o_n = jnp.concatenate([head_norm(o_h) for o_h in o_heads], axis=1) * prow(P_GNORM)
        yb = (o_n * sog_s[...]).astype(BF16)
        mixed = ga_s[...] + sgb_s[...] * _mm(yb, _unpack_rows(wb_ref[...]))
        z = _mm(mixed.astype(BF16), _unpack_rows(wout_ref[...]))
        x1 = x_ref[...].reshape(rows, D) + _rms(z, prow(P_GAIN + 1))
        x1_ref[...] = x1.reshape(seqs, tokens, D)

    def back_prompt():
        xnb = xn_s[...]
        qf = qf_s[...]
        kk = kk_s[...]
        cum = cum_s[...]
        vb = v_s[...].astype(BF16)
        n_sub = rows // sub
        last = cum[rows - 1:rows, :]
        qib = (qf * jnp.exp(cum)).astype(BF16)
        ksb = (kk * jnp.exp(last - cum)).astype(BF16)
        decay_row = jnp.exp(last)
        for hd in range(N_HEADS):
            hs = head_cols(hd)
            s_old = sout_ref[0, hd]
            parts = []
            for i in range(n_sub):
                r0 = i * sub
                r1 = r0 + sub
                width = min(rows, -(-r1 // HEAD) * HEAD)
                ref_row = cum[r0 - 1:r0, hs] if i > 0 else jnp.zeros((1, HEAD), F32)
                qd = (qf[r0:r1, hs] * jnp.exp(cum[r0:r1, hs] - ref_row)).astype(BF16)
                arg = ref_row - cum[0:width, hs]
                if width > r1:
                    arg = jnp.where(lax.broadcasted_iota(jnp.int32, (width, 1), 0) < r1, arg, 0.0)
                kdi = (kk[0:width, hs] * jnp.exp(arg)).astype(BF16)
                att = _mm_nt(qd, kdi)
                rr = lax.broadcasted_iota(jnp.int32, (sub, width), 0) + r0
                cc = lax.broadcasted_iota(jnp.int32, (sub, width), 1)
                att = jnp.where(cc <= rr, att, 0.0).astype(BF16)
                if width < rows:
                    att = jnp.concatenate([att, jnp.zeros((sub, rows - width), BF16)], axis=1)
                parts.append(att)
            att_full = jnp.concatenate(parts, axis=0) if n_sub > 1 else parts[0]
            if rows % HEAD == 0:
                o_h = _mm(jnp.concatenate([att_full, qib[:, hs]], axis=1),
                          jnp.concatenate([vb[:, hs], s_old.astype(BF16)], axis=0))
            else:
                o_h = _mm(att_full, vb[:, hs]) + _mm(qib[:, hs], s_old.astype(BF16))
            upd = _mm_tn(ksb[:, hs], vb[:, hs])
            sout_ref[0, hd] = _column_of(decay_row[:, hs]) * s_old + upd
            sold_s[hd] = s_old
            o_s[:, hs] = o_h
            quarter = D // 4
            grp, part = (C_MB, hd) if hd < 4 else (C_OG, hd - 4)
            cols = slice(part * quarter, (part + 1) * quarter)
            proj = _mm(xnb, _unpack_rows(win_ref[:, grp * D + part * quarter:grp * D + (part + 1) * quarter]))
            if hd < 4:
                sgb_s[:, cols] = _sigmoid_of_twice(proj)
            else:
                sog_s[:, cols] = _silu_of_twice(proj)

        worst = jnp.zeros((1, D), F32)
        for i in range(n_sub):
            start = cum[i * sub - 1:i * sub, :] if i > 0 else jnp.zeros((1, D), F32)
            worst = jnp.maximum(worst, start - cum[(i + 1) * sub - 1:(i + 1) * sub, :])
        out_of_range = jnp.max(worst) > SAFE_DECAY_RANGE

        @pl.when(out_of_range)
        def _exact_scores():
            row_i = lax.broadcasted_iota(jnp.int32, (rows, 1), 0)
            for hd in range(N_HEADS):
                hs = head_cols(hd)
                qib_h = (qf_s[:, hs] * jnp.exp(cum_s[:, hs])).astype(BF16)
                o_s[:, hs] = _mm(qib_h, sold_s[hd].astype(BF16))

                def block_body(b, carry, hs=hs):
                    r0 = pl.multiple_of(b * SUBLANES, SUBLANES)
                    rs = pl.ds(r0, SUBLANES)
                    q_b, k_b, c_b, v_b = qf_s[rs, hs], kk_s[rs, hs], cum_s[rs, hs], v_s[rs, hs]
                    prev_start = pl.multiple_of(jnp.maximum(r0 - SUBLANES, 0), SUBLANES)
                    before = cum_s[pl.ds(prev_start, SUBLANES), hs][SUBLANES - 1:SUBLANES, :]
                    ref_row = jnp.where(b > 0, before, 0.0)
                    qd = (q_b * jnp.exp(c_b - ref_row)).astype(BF16)
                    arg = jnp.minimum(ref_row - cum_s[:, hs], 0.0)
                    kd = jnp.where(row_i < r0, kk_s[:, hs] * jnp.exp(arg), 0.0).astype(BF16)
                    earlier = _mm(_mm_nt(qd, kd).astype(BF16), v_s[:, hs].astype(BF16))
                    o_s[rs, hs] = o_s[rs, hs] + earlier + _exact_block(q_b, k_b, c_b, v_b)
                    return carry

                lax.fori_loop(0, rows // SUBLANES, block_body, 0)

        finish([o_s[:, head_cols(hd)] for hd in range(N_HEADS)])

    def decode_state_step():
        def seq_body(g, carry):
            rsl = pl.ds(pl.multiple_of(g * SUBLANES, SUBLANES), SUBLANES)
            s_old = s0_ref[g, 0]
            c_b = cum_h[j, rsl, :]
            total_b = tot_h[j, rsl, :]
            qi = qf_h[j, rsl, :] * jnp.exp(c_b)
            ks = kk_h[j, rsl, :] * jnp.exp(total_b - c_b)
            oin_h[j, rsl, :] = _mm(qi.astype(BF16), s_old.astype(BF16))
            upd = _mm_tn(ks.astype(BF16), v_h[j, rsl, :].astype(BF16))
            decay = _column_of(jnp.exp(total_b[0:1, :]))
            sout_ref[g, 0] = decay * s_old + upd
            return carry

        lax.fori_loop(0, seqs, seq_body, 0, unroll=SUBLANES)

    def back_decode():
        ri = lax.broadcasted_iota(jnp.int32, (rows, rows), 0)
        ci = lax.broadcasted_iota(jnp.int32, (rows, rows), 1)
        causal = jnp.logical_and(ci <= ri, (ri // tokens) == (ci // tokens))
        for hd in range(N_HEADS):
            c_h = cum_h[hd]
            scores = _mm_nt((qf_h[hd] * jnp.exp(c_h)).astype(BF16), (kk_h[hd] * jnp.exp(-c_h)).astype(BF16))
            att = jnp.where(causal, scores, 0.0).astype(BF16)
            oi_h[hd] = _mm(att, v_h[hd].astype(BF16))

        worst = jnp.zeros((1, HEAD), F32)
        for hd in range(N_HEADS):
            worst = jnp.maximum(worst, jnp.max(-tot_h[hd], axis=0, keepdims=True))
        out_of_range = jnp.max(worst) > SAFE_DECAY_RANGE

        @pl.when(out_of_range)
        def _exact_scores():
            for hd in range(N_HEADS):
                def seq_body(g, carry, hd=hd):
                    rs = pl.ds(pl.multiple_of(g * SUBLANES, SUBLANES), SUBLANES)
                    oi_h[hd, rs, :] = _exact_block(qf_h[hd, rs, :], kk_h[hd, rs, :], cum_h[hd, rs, :], v_h[hd, rs, :])
                    return carry

                lax.fori_loop(0, seqs, seq_body, 0)

        finish([oin_h[hd] + oi_h[hd] for hd in range(N_HEADS)])

    if decode:
        @pl.when(j == 0)
        def _front_region():
            ubuf[:, 0:SUBLANES, :] = cext_ref[...]
            front()

        decode_state_step()

        @pl.when(j == N_HEADS - 1)
        def _back_region():
            back_decode()
    else:
        @pl.when(j == 0)
        def _init():
            ubuf[:, 0:SUBLANES, :] = cext_ref[...]
            hout_ref[...] = h0_ref[...]
            sout_ref[...] = s0_ref[...]

        front()

        @pl.when(j >= 0)
        def _back_region():
            back_prompt()


def _resident(shape):
    return pl.BlockSpec(shape, lambda i, j: (0,) * len(shape), pipeline_mode=pl.Buffered(1))


def _mixer(x, cext, h0, s0, pvec, w_in, w_ri, w_a, w_b, w_out, *, seqs, tokens, reset_first, sub, name):
    n_seq, length, _ = x.shape
    n_tiles = length // tokens
    decode = seqs > 1
    assert n_seq % seqs == 0 and length % tokens == 0 and tokens % SUBLANES == 0
    assert not decode or (n_tiles == 1 and tokens == SUBLANES)
    assert (seqs * tokens) % sub == 0
    rows = seqs * tokens
    kern = functools.partial(_mixer_kernel, seqs=seqs, tokens=tokens, n_tiles=n_tiles,
                             reset_first=reset_first, sub=sub)
    tile_f32 = pltpu.VMEM((rows, D), F32)
    by_head_f32 = pltpu.VMEM((N_HEADS, rows, HEAD), F32)
    common = [pltpu.VMEM((seqs, SUBLANES + tokens, D), F32),
              tile_f32, tile_f32, tile_f32]
    if decode:
        grid = (n_seq // seqs, N_HEADS)
        x_map = lambda i, j: (i, 0, 0)
        s_block = (seqs, 1, HEAD, HEAD)
        s_map = lambda i, j: (i, j, 0, 0)
        scratch = common + [by_head_f32] * 7
    else:
        grid = (n_seq, n_tiles)
        x_map = lambda i, j: (i, j, 0)
        s_block = (1, N_HEADS, HEAD, HEAD)
        s_map = lambda i, j: (i, 0, 0, 0)
        scratch = common + [tile_f32] * 6 + [pltpu.VMEM((N_HEADS, HEAD, HEAD), F32), pltpu.VMEM((rows, D), BF16)]
    seq_map = lambda i, j: (i, 0, 0)
    if cext.shape[0] == n_seq:
        in_seq_map, in_s_map = seq_map, s_map
    else:
        assert cext.shape[0] == h0.shape[0] == s0.shape[0] == seqs == 1
        in_seq_map = lambda i, j: (0, 0, 0)
        in_s_map = lambda i, j: (0, 0, 0, 0)
    return pl.pallas_call(
        kern,
        grid=grid,
        in_specs=[
            pl.BlockSpec((seqs, tokens, D), x_map),
            pl.BlockSpec((seqs, SUBLANES, D), in_seq_map),
            pl.BlockSpec((seqs, 1, D), in_seq_map),
            pl.BlockSpec(s_block, in_s_map),
            _resident((P_ROWS, D)),
            _resident(w_in.shape),
            _resident(w_ri.shape),
            _resident(w_a.shape),
            _resident(w_b.shape),
            _resident(w_out.shape),
        ],
        out_specs=[
            pl.BlockSpec((seqs, tokens, D), x_map),
            pl.BlockSpec((seqs, SUBLANES, D), seq_map),
            pl.BlockSpec((seqs, 1, D), seq_map),
            pl.BlockSpec(s_block, s_map),
        ],
        out_shape=[
            jax.ShapeDtypeStruct((n_seq, length, D), F32),
            jax.ShapeDtypeStruct((n_seq, SUBLANES, D), F32),
            jax.ShapeDtypeStruct((n_seq, 1, D), F32),
            jax.ShapeDtypeStruct((n_seq, N_HEADS, HEAD, HEAD), F32),
        ],
        scratch_shapes=scratch,
        compiler_params=pltpu.CompilerParams(
            dimension_semantics=("arbitrary", "arbitrary"),
            vmem_limit_bytes=VMEM_LIMIT_BYTES),
        name=name,
    )(x, cext, h0, s0, pvec, w_in, w_ri, w_a, w_b, w_out)


def _mlp_kernel(xa_ref, xb_ref, g_ref, wup_ref, wdn_ref, oa_ref, ob_ref, *, steps_a):
    def tile_mlp(x_ref, o_ref):
        x = x_ref[...]
        hn = _rms(x, g_ref[0:1, :]).astype(BF16)
        acc = jnp.zeros(x.shape, F32)
        for c in range(D_FF // D):
            t = _mm(hn, _unpack_rows(wup_ref[:, c * D:(c + 1) * D]))
            t = jnp.square(jnp.maximum(t, 0.0)).astype(BF16)
            acc = acc + _mm(t, _unpack_rows(wdn_ref[c * (D // 2):(c + 1) * (D // 2), :]))
        o_ref[...] = x + _rms(acc, g_ref[1:2, :])

    i = pl.program_id(0)

    @pl.when(i < steps_a)
    def _first():
        tile_mlp(xa_ref, oa_ref)

    @pl.when(i >= steps_a)
    def _second():
        tile_mlp(xb_ref, ob_ref)


def _mlp(xa, xb, gains, w_up, w_down, *, tile, name):
    na, nb = xa.shape[0], xb.shape[0]
    assert na % tile == 0 and nb % tile == 0
    steps_a, steps_b = na // tile, nb // tile
    a_map = lambda i: (jnp.minimum(i, steps_a - 1), 0)
    b_map = lambda i: (jnp.maximum(i - steps_a, 0), 0)
    return pl.pallas_call(
        functools.partial(_mlp_kernel, steps_a=steps_a),
        grid=(steps_a + steps_b,),
        in_specs=[
            pl.BlockSpec((tile, D), a_map),
            pl.BlockSpec((tile, D), b_map),
            pl.BlockSpec((2, D), lambda i: (0, 0), pipeline_mode=pl.Buffered(1)),
            pl.BlockSpec(w_up.shape, lambda i: (0, 0), pipeline_mode=pl.Buffered(1)),
            pl.BlockSpec(w_down.shape, lambda i: (0, 0), pipeline_mode=pl.Buffered(1)),
        ],
        out_specs=[pl.BlockSpec((tile, D), a_map), pl.BlockSpec((tile, D), b_map)],
        out_shape=[jax.ShapeDtypeStruct((na, D), F32), jax.ShapeDtypeStruct((nb, D), F32)],
        compiler_params=pltpu.CompilerParams(
            dimension_semantics=("arbitrary",),
            vmem_limit_bytes=VMEM_LIMIT_BYTES),
        name=name,
    )(xa, xb, gains, w_up, w_down)


def kernel(x_prompt, x_sample, state_conv, state_rglru, state_hgrn, meta_tokens, norm_gains, w_in,
           conv_w, conv_b, rg_w, rg_b, ig_w, ig_b, lru_lambda, hgrn_lb, hgrn_gnorm,
           w_branch_a, w_branch_b, w_out, w_up, w_down):
    bp, seq, _ = x_prompt.shape
    bs, dec_seq, _ = x_sample.shape
    layer = 0
    pvec = jnp.concatenate([
        norm_gains[layer], conv_w[layer], conv_b[layer][None], 0.5 * rg_b[layer][None], 0.5 * ig_b[layer][None],
        lru_lambda[layer][None], hgrn_lb[layer:layer + 2], hgrn_gnorm[layer][None],
        jnp.zeros((1, D), F32)], axis=0).astype(F32)
    halved = (C_GATE, C_Q, C_OG, C_MA, C_MB)
    scale_in = jnp.concatenate([jnp.full((1, D), 0.5 if g in halved else 1.0, F32) for g in range(8)], axis=1)

    def ones(n):
        return jnp.ones((1, n), F32)

    w_in_b = _pack_rows(w_in[layer], scale_in, name="pack_w_in")
    w_gates = jnp.concatenate([rg_w[layer], ig_w[layer]], axis=-1)
    w_ri = _pack_rows(w_gates.reshape(N_HEADS * HEAD, 2 * HEAD), 0.5 * ones(2 * HEAD), name="pack_w_gates")
    w_ri = w_ri.reshape(N_HEADS, HEAD // 2, 2 * HEAD)
    w_a, w_b, w_o = _pack_rows_many([w_branch_a[layer], w_branch_b[layer], w_out[layer]], name="pack_w_square")
    w_u = _pack_rows(w_up[layer], ones(D_FF), name="pack_w_up")
    w_d = _pack_rows(w_down[layer], ones(D), name="pack_w_down")
    weights = (pvec, w_in_b, w_ri, w_a, w_b, w_o)

    zeros_c = jnp.zeros((1, SUBLANES, D), F32)
    zeros_h = jnp.zeros((1, 1, D), F32)
    zeros_s = jnp.zeros((1, N_HEADS, HEAD, HEAD), F32)
    _, c_m, h_m, s_m = _mixer(meta_tokens[None].astype(F32), zeros_c, zeros_h, zeros_s, *weights,
                              seqs=1, tokens=N_META, reset_first=True, sub=N_META, name="mixer_meta")

    x1_p, c_p, h_p, s_p = _mixer(
        x_prompt,
        c_m, h_m, s_m,
        *weights, seqs=1, tokens=256, reset_first=False, sub=64, name="mixer_prompt")

    cext_s = jnp.pad(state_conv[layer], ((0, 0), (SUBLANES - 3, 0), (0, 0)))
    x1_s, c_s, h_s, s_s = _mixer(
        x_sample, cext_s, state_rglru[layer][:, None, :], state_hgrn[layer],
        *weights, seqs=32, tokens=dec_seq, reset_first=False, sub=8, name="mixer_sample")

    gains_mlp = norm_gains[layer, 2:4]
    y_p, y_s = _mlp(x1_p.reshape(bp * seq, D), x1_s.reshape(bs * dec_seq, D), gains_mlp, w_u, w_d,
                    tile=1024, name="mlp")

    return (y_p.reshape(bp, seq, D), y_s.reshape(bs, dec_seq, D),
            c_p[:, SUBLANES - 3:, :][None], h_p[:, 0, :][None], s_p[None],
            c_s[:, SUBLANES - 3:, :][None], h_s[:, 0, :][None], s_s[None])
```

```python
import functools

import jax
import jax.numpy as jnp
from jax import lax
from jax.experimental import pallas as pl
from jax.experimental.pallas import tpu as pltpu

D = 1024
N_HEADS = 8
HEAD = 128
N_META = 16
LRU_C = 8.0
EPS = 1e-6
D_FF = 4096
SUBLANES = 8
F32 = jnp.float32
BF16 = jnp.bfloat16

P_GAIN, P_CONV_W, P_CONV_B, P_RG_B, P_IG_B, P_LAMBDA, P_LB, P_GNORM, P_ROWS = 0, 4, 8, 9, 10, 11, 12, 14, 16

C_U, C_GATE, C_Q, C_F, C_I, C_OG, C_MA, C_MB = range(8)

VMEM_LIMIT_BYTES = 60 * 1024 * 1024
PACK_STEPS = 8
PACK_VMEM_LIMIT_BYTES = 40 * 1024 * 1024
SAFE_DECAY_RANGE = 64.0


def _rms(x, g):
    ms = jnp.mean(x * x, axis=-1, keepdims=True)
    return x * lax.rsqrt(ms + EPS) * g


def _sigmoid_of_twice(h):
    return 0.5 * jnp.tanh(h) + 0.5


def _silu_of_twice(h):
    return h + h * jnp.tanh(h)


_GELU_C1 = 2.0 * 0.7978845608028654
_GELU_C2 = 8.0 * 0.7978845608028654 * 0.044715


def _gelu_tanh_of_twice(h):
    return h + h * jnp.tanh(h * (_GELU_C1 + _GELU_C2 * (h * h)))


def _mm(a, b):
    return jnp.dot(a, b, preferred_element_type=F32)


def _mm_nt(a, b):
    return lax.dot_general(a, b, (((1,), (1,)), ((), ())), preferred_element_type=F32)


def _mm_tn(a, b):
    return lax.dot_general(a, b, (((0,), (0,)), ((), ())), preferred_element_type=F32)


def _mask_matmul(mask_bf, x):
    hi = x.astype(BF16)
    lo = (x - hi.astype(F32)).astype(BF16)
    return _mm(mask_bf, hi) + _mm(mask_bf, lo)


def _pack_kernel(*refs, groups):
    out_refs = refs[len(refs) - len(groups):]
    pos = 0
    for (n_parts, scale_kind, scalar), o_ref in zip(groups, out_refs):
        parts = [refs[pos + p][...] for p in range(n_parts)]
        pos += n_parts
        w = parts[0] if n_parts == 1 else jnp.concatenate(parts, axis=1)
        if scale_kind == "row":
            w = w * refs[pos][...]
            pos += 1
        elif scale_kind == "scalar":
            w = w * scalar
        o_ref[...] = pltpu.bitcast(w.astype(BF16), jnp.int32)


def _pack_weights(groups, *, name):
    operands, in_specs, out_specs, out_shapes, kinds = [], [], [], [], []
    for mats, scale in groups:
        k = mats[0].shape[0]
        tile = k // PACK_STEPS
        assert all(m.shape[0] == k for m in mats) and k % PACK_STEPS == 0 and tile % (2 * SUBLANES) == 0
        n = sum(m.shape[1] for m in mats)
        for m in mats:
            operands.append(m)
            in_specs.append(pl.BlockSpec((tile, m.shape[1]), lambda i: (i, 0)))
        if scale is None:
            kinds.append((len(mats), "none", None))
        elif isinstance(scale, float):
            kinds.append((len(mats), "scalar", scale))
        else:
            kinds.append((len(mats), "row", None))
            operands.append(scale)
            in_specs.append(pl.BlockSpec((1, n), lambda i: (0, 0)))
        out_specs.append(pl.BlockSpec((tile // 2, n), lambda i: (i, 0)))
        out_shapes.append(jax.ShapeDtypeStruct((k // 2, n), jnp.int32))
    return pl.pallas_call(
        functools.partial(_pack_kernel, groups=tuple(kinds)),
        grid=(PACK_STEPS,),
        in_specs=in_specs,
        out_specs=out_specs,
        out_shape=out_shapes,
        compiler_params=pltpu.CompilerParams(
            dimension_semantics=("arbitrary",), vmem_limit_bytes=PACK_VMEM_LIMIT_BYTES),
        name=name,
    )(*operands)


def _unpack_rows(words):
    return pltpu.bitcast(words, BF16)


def _exact_block(q_b, k_b, c_b, v_b):
    t_idx = lax.broadcasted_iota(jnp.int32, (SUBLANES, 1), 0)
    acc = jnp.zeros((SUBLANES, HEAD), F32)
    for s in range(SUBLANES):
        decay = jnp.exp(jnp.minimum(c_b - c_b[s:s + 1, :], 0.0))
        score = jnp.sum(q_b * k_b[s:s + 1, :] * decay, axis=-1, keepdims=True)
        acc = acc + jnp.where(t_idx >= s, score, 0.0) * v_b[s:s + 1, :]
    return acc


def _column_of(row):
    return jnp.transpose(jnp.broadcast_to(row, (HEAD, HEAD)))


def _mixer_kernel(x_ref, cext_ref, h0_ref, s0_ref, pvec_ref, win_ref, wri_ref, wa_ref, wb_ref, wout_ref,
                  x1_ref, cout_ref, hout_ref, sout_ref, *scratch,
                  seqs, tokens, n_tiles, reset_first, sub):
    j = pl.program_id(1)
    rows = seqs * tokens
    decode = seqs > 1
    if decode:
        ubuf, ga_s, sgb_s, sog_s, qf_h, kk_h, cum_h, v_h, tot_h, oin_h, oi_h = scratch
    else:
        ubuf, ga_s, sgb_s, sog_s, hbuf, qf_s, kk_s, cum_s, v_s, o_s, sold_s, xn_s = scratch

    pv = pvec_ref[...]

    def prow(r):
        return pv[r:r + 1, :]

    def wcol(g):
        return _unpack_rows(win_ref[:, g * D:(g + 1) * D])

    def head_cols(hd):
        return slice(hd * HEAD, (hd + 1) * HEAD)

    def front():
        x = x_ref[...].reshape(rows, D)
        xn = _rms(x, prow(P_GAIN + 0)).astype(BF16)
        row_id = lax.broadcasted_iota(jnp.int32, (rows, 1), 0)

        u = _mm(xn, wcol(C_U))
        ubuf[:, SUBLANES:SUBLANES + tokens, :] = u.reshape(seqs, tokens, D)
        if decode:
            merge_gates(xn)
        else:
            xn_s[...] = xn
        uc = prow(P_CONV_B) + prow(P_CONV_W + 3) * u
        for k in range(1, 4):
            shifted = ubuf[:, SUBLANES - k:SUBLANES - k + tokens, :].reshape(rows, D)
            uc = uc + prow(P_CONV_W + 3 - k) * shifted
        tail = ubuf[:, tokens:tokens + SUBLANES, :]
        ubuf[:, 0:SUBLANES, :] = tail
        cout_ref[...] = tail

        ucb = uc.astype(BF16)
        r_parts, i_parts = [], []
        for b in range(N_HEADS):
            ri = _mm(ucb[:, head_cols(b)], _unpack_rows(wri_ref[b]))
            r_parts.append(ri[:, :HEAD])
            i_parts.append(ri[:, HEAD:])
        r_pre = jnp.concatenate(r_parts, axis=1)
        i_pre = jnp.concatenate(i_parts, axis=1)
        if not decode:
            hbuf[...] = r_pre
            o_s[...] = i_pre
        hgrn_operands(xn)
        if not decode:
            r_pre = hbuf[...]
            i_pre = o_s[...]
        r_gate = _sigmoid_of_twice(r_pre + prow(P_RG_B))
        i_gate = _sigmoid_of_twice(i_pre + prow(P_IG_B))
        lam = prow(P_LAMBDA)
        softplus_neg_lam = jnp.maximum(-lam, 0.0) + jnp.log1p(jnp.exp(-jnp.abs(lam)))
        log_a = (-LRU_C) * r_gate * softplus_neg_lam
        a_cum = jnp.exp(log_a)
        th = jnp.tanh(log_a)
        sq = -2.0 * th / (1.0 - th)
        mult = jnp.where(sq > 0.0, sq * lax.rsqrt(sq), 0.0)
        if reset_first:
            mult = jnp.where(jnp.logical_and(row_id == 0, j == 0), 1.0, mult)
        b_cum = mult * i_gate * uc

        a_cum = a_cum.reshape(rows // SUBLANES, SUBLANES, D)
        b_cum = b_cum.reshape(rows // SUBLANES, SUBLANES, D)
        sublane = lax.broadcasted_iota(jnp.int32, (1, SUBLANES, 1), 1)
        for s in (1, 2, 4):
            keep = sublane >= s
            a_prev = jnp.where(keep, pltpu.roll(a_cum, s, 1), 1.0)
            b_prev = jnp.where(keep, pltpu.roll(b_cum, s, 1), 0.0)
            b_cum = a_cum * b_prev + b_cum
            a_cum = a_cum * a_prev
        a_cum = a_cum.reshape(rows, D)
        b_cum = b_cum.reshape(rows, D)

        if decode:
            h_in = jnp.broadcast_to(h0_ref[...], (seqs, SUBLANES, D)).reshape(rows, D)
            h_all = a_cum * h_in + b_cum
            hout_ref[...] = h_all.reshape(seqs, SUBLANES, D)[:, SUBLANES - 1:SUBLANES, :]
        else:
            h = hout_ref[0]
            for g in range(rows // SUBLANES):
                sl = slice(g * SUBLANES, (g + 1) * SUBLANES)
                hg = a_cum[sl] * h + b_cum[sl]
                hbuf[sl, :] = hg
                h = hg[SUBLANES - 1:SUBLANES, :]
            hout_ref[0] = h
            h_all = hbuf[...]

        ya = (h_all * _gelu_tanh_of_twice(_mm(xn, wcol(C_GATE)))).astype(BF16)
        ga_s[...] = _sigmoid_of_twice(_mm(xn, wcol(C_MA))) * _mm(ya, _unpack_rows(wa_ref[...]))

    def merge_gates(xn):
        sgb_s[...] = _sigmoid_of_twice(_mm(xn, wcol(C_MB)))
        sog_s[...] = _silu_of_twice(_mm(xn, wcol(C_OG)))

    def hgrn_operands(xn):
        qf = _silu_of_twice(_mm(xn, wcol(C_Q)))
        lb_raw = pv[P_LB:P_LB + 2, :]
        lb_exp = jnp.exp(lb_raw - jnp.max(lb_raw, axis=0, keepdims=True))
        lb = lb_exp[0:1, :] / jnp.sum(lb_exp, axis=0, keepdims=True)
        fg = lb + (1.0 - lb) / (1.0 + jnp.exp(-_mm(xn, wcol(C_F))))
        logf = jnp.log(fg)
        kk = 1.0 - fg
        v = _mm(xn, wcol(C_I))

        ri2 = lax.broadcasted_iota(jnp.int32, (rows, rows), 0)
        ci2 = lax.broadcasted_iota(jnp.int32, (rows, rows), 1)
        if decode:
            same_seq = (ri2 // tokens) == (ci2 // tokens)
            total = _mask_matmul(same_seq.astype(BF16), logf)
            cum = _mask_matmul(jnp.logical_and(ci2 <= ri2, same_seq).astype(BF16), logf)
            for hd in range(N_HEADS):
                hs = head_cols(hd)
                v_h[hd] = v[:, hs]
                tot_h[hd] = total[:, hs]
                qf_h[hd] = qf[:, hs]
                kk_h[hd] = kk[:, hs]
                cum_h[hd] = cum[:, hs]
        else:
            qf_s[...] = qf
            kk_s[...] = kk
            v_s[...] = v
            cum_s[...] = _mask_matmul((ci2 <= ri2).astype(BF16), logf)

    def head_norm(o_h):
        ms = jnp.mean(o_h * o_h, axis=-1, keepdims=True)
        return o_h * lax.rsqrt(ms + EPS)

    def finish(o_heads):
        o_n = jnp.concatenate([head_norm(o_h) for o_h in o_heads], axis=1) * prow(P_GNORM)
        yb = (o_n * sog_s[...]).astype(BF16)
        mixed = ga_s[...] + sgb_s[...] * _mm(yb, _unpack_rows(wb_ref[...]))
        z = _mm(mixed.astype(BF16), _unpack_rows(wout_ref[...]))
        x1 = x_ref[...].reshape(rows, D) + _rms(z, prow(P_GAIN + 1))
        x1_ref[...] = x1.reshape(seqs, tokens, D)

    def back_prompt():
        xnb = xn_s[...]
        qf = qf_s[...]
        kk = kk_s[...]
        cum = cum_s[...]
        vb = v_s[...].astype(BF16)
        n_sub = rows // sub
        last = cum[rows - 1:rows, :]
        qib = (qf * jnp.exp(cum)).astype(BF16)
        ksb = (kk * jnp.exp(last - cum)).astype(BF16)
        decay_row = jnp.exp(last)
        for hd in range(N_HEADS):
            hs = head_cols(hd)
            s_old = sout_ref[0, hd]
            parts = []
            for i in range(n_sub):
                r0 = i * sub
                r1 = r0 + sub
                width = min(rows, -(-r1 // HEAD) * HEAD)
                ref_row = cum[r0 - 1:r0, hs] if i > 0 else jnp.zeros((1, HEAD), F32)
                qd = (qf[r0:r1, hs] * jnp.exp(cum[r0:r1, hs] - ref_row)).astype(BF16)
                arg = ref_row - cum[0:width, hs]
                if width > r1:
                    arg = jnp.where(lax.broadcasted_iota(jnp.int32, (width, 1), 0) < r1, arg, 0.0)
                kdi = (kk[0:width, hs] * jnp.exp(arg)).astype(BF16)
                att = _mm_nt(qd, kdi)
                rr = lax.broadcasted_iota(jnp.int32, (sub, width), 0) + r0
                cc = lax.broadcasted_iota(jnp.int32, (sub, width), 1)
                att = jnp.where(cc <= rr, att, 0.0).astype(BF16)
                if width < rows:
                    att = jnp.concatenate([att, jnp.zeros((sub, rows - width), BF16)], axis=1)
                parts.append(att)
            att_full = jnp.concatenate(parts, axis=0) if n_sub > 1 else parts[0]
            if rows % HEAD == 0:
                o_h = _mm(jnp.concatenate([att_full, qib[:, hs]], axis=1),
                          jnp.concatenate([vb[:, hs], s_old.astype(BF16)], axis=0))
            else:
                o_h = _mm(att_full, vb[:, hs]) + _mm(qib[:, hs], s_old.astype(BF16))
            upd = _mm_tn(ksb[:, hs], vb[:, hs])
            sout_ref[0, hd] = _column_of(decay_row[:, hs]) * s_old + upd
            sold_s[hd] = s_old
            o_s[:, hs] = o_h
            quarter = D // 4
            grp, part = (C_MB, hd) if hd < 4 else (C_OG, hd - 4)
            cols = slice(part * quarter, (part + 1) * quarter)
            proj = _mm(xnb, _unpack_rows(win_ref[:, grp * D + part * quarter:grp * D + (part + 1) * quarter]))
            if hd < 4:
                sgb_s[:, cols] = _sigmoid_of_twice(proj)
            else:
                sog_s[:, cols] = _silu_of_twice(proj)

        worst = jnp.zeros((1, D), F32)
        for i in range(n_sub):
            start = cum[i * sub - 1:i * sub, :] if i > 0 else jnp.zeros((1, D), F32)
            worst = jnp.maximum(worst, start - cum[(i + 1) * sub - 1:(i + 1) * sub, :])
        out_of_range = jnp.max(worst) > SAFE_DECAY_RANGE

        @pl.when(out_of_range)
        def _exact_scores():
            row_i = lax.broadcasted_iota(jnp.int32, (rows, 1), 0)
            for hd in range(N_HEADS):
                hs = head_cols(hd)
                qib_h = (qf_s[:, hs] * jnp.exp(cum_s[:, hs])).astype(BF16)
                o_s[:, hs] = _mm(qib_h, sold_s[hd].astype(BF16))

                def block_body(b, carry, hs=hs):
                    r0 = pl.multiple_of(b * SUBLANES, SUBLANES)
                    rs = pl.ds(r0, SUBLANES)
                    q_b, k_b, c_b, v_b = qf_s[rs, hs], kk_s[rs, hs], cum_s[rs, hs], v_s[rs, hs]
                    prev_start = pl.multiple_of(jnp.maximum(r0 - SUBLANES, 0), SUBLANES)
                    before = cum_s[pl.ds(prev_start, SUBLANES), hs][SUBLANES - 1:SUBLANES, :]
                    ref_row = jnp.where(b > 0, before, 0.0)
                    qd = (q_b * jnp.exp(c_b - ref_row)).astype(BF16)
                    arg = jnp.minimum(ref_row - cum_s[:, hs], 0.0)
                    kd = jnp.where(row_i < r0, kk_s[:, hs] * jnp.exp(arg), 0.0).astype(BF16)
                    earlier = _mm(_mm_nt(qd, kd).astype(BF16), v_s[:, hs].astype(BF16))
                    o_s[rs, hs] = o_s[rs, hs] + earlier + _exact_block(q_b, k_b, c_b, v_b)
                    return carry

                lax.fori_loop(0, rows // SUBLANES, block_body, 0)

        finish([o_s[:, head_cols(hd)] for hd in range(N_HEADS)])

    def decode_state_step():
        def seq_body(g, carry):
            rsl = pl.ds(pl.multiple_of(g * SUBLANES, SUBLANES), SUBLANES)
            s_old = s0_ref[g, 0]
            c_b = cum_h[j, rsl, :]
            total_b = tot_h[j, rsl, :]
            qi = qf_h[j, rsl, :] * jnp.exp(c_b)
            ks = kk_h[j, rsl, :] * jnp.exp(total_b - c_b)
            oin_h[j, rsl, :] = _mm(qi.astype(BF16), s_old.astype(BF16))
            upd = _mm_tn(ks.astype(BF16), v_h[j, rsl, :].astype(BF16))
            decay = _column_of(jnp.exp(total_b[0:1, :]))
            sout_ref[g, 0] = decay * s_old + upd
            return carry

        lax.fori_loop(0, seqs, seq_body, 0, unroll=SUBLANES)

    def back_decode():
        ri = lax.broadcasted_iota(jnp.int32, (rows, rows), 0)
        ci = lax.broadcasted_iota(jnp.int32, (rows, rows), 1)
        causal = jnp.logical_and(ci <= ri, (ri // tokens) == (ci // tokens))
        for hd in range(N_HEADS):
            c_h = cum_h[hd]
            scores = _mm_nt((qf_h[hd] * jnp.exp(c_h)).astype(BF16), (kk_h[hd] * jnp.exp(-c_h)).astype(BF16))
            att = jnp.where(causal, scores, 0.0).astype(BF16)
            oi_h[hd] = _mm(att, v_h[hd].astype(BF16))

        worst = jnp.zeros((1, HEAD), F32)
        for hd in range(N_HEADS):
            worst = jnp.maximum(worst, jnp.max(-tot_h[hd], axis=0, keepdims=True))
        out_of_range = jnp.max(worst) > SAFE_DECAY_RANGE

        @pl.when(out_of_range)
        def _exact_scores():
            for hd in range(N_HEADS):
                def seq_body(g, carry, hd=hd):
                    rs = pl.ds(pl.multiple_of(g * SUBLANES, SUBLANES), SUBLANES)
                    oi_h[hd, rs, :] = _exact_block(qf_h[hd, rs, :], kk_h[hd, rs, :], cum_h[hd, rs, :], v_h[hd, rs, :])
                    return carry

                lax.fori_loop(0, seqs, seq_body, 0)

        finish([oin_h[hd] + oi_h[hd] for hd in range(N_HEADS)])

    if decode:
        @pl.when(j == 0)
        def _front_region():
            ubuf[:, 0:SUBLANES, :] = cext_ref[...]
            front()

        decode_state_step()

        @pl.when(j == N_HEADS - 1)
        def _back_region():
            back_decode()
    else:
        @pl.when(j == 0)
        def _init():
            ubuf[:, 0:SUBLANES, :] = cext_ref[...]
            hout_ref[...] = h0_ref[...]
            sout_ref[...] = s0_ref[...]

        front()

        @pl.when(j >= 0)
        def _back_region():
            back_prompt()


def _resident(shape):
    return pl.BlockSpec(shape, lambda i, j: (0,) * len(shape), pipeline_mode=pl.Buffered(1))


def _mixer(x, cext, h0, s0, pvec, w_in, w_ri, w_a, w_b, w_out, *, seqs, tokens, reset_first, sub, name):
    n_seq, length, _ = x.shape
    n_tiles = length // tokens
    decode = seqs > 1
    assert n_seq % seqs == 0 and length % tokens == 0 and tokens % SUBLANES == 0
    assert not decode or (n_tiles == 1 and tokens == SUBLANES)
    assert (seqs * tokens) % sub == 0
    rows = seqs * tokens
    kern = functools.partial(_mixer_kernel, seqs=seqs, tokens=tokens, n_tiles=n_tiles,
                             reset_first=reset_first, sub=sub)
    tile_f32 = pltpu.VMEM((rows, D), F32)
    by_head_f32 = pltpu.VMEM((N_HEADS, rows, HEAD), F32)
    common = [pltpu.VMEM((seqs, SUBLANES + tokens, D), F32),
              tile_f32, tile_f32, tile_f32]
    if decode:
        grid = (n_seq // seqs, N_HEADS)
        x_map = lambda i, j: (i, 0, 0)
        s_block = (seqs, 1, HEAD, HEAD)
        s_map = lambda i, j: (i, j, 0, 0)
        scratch = common + [by_head_f32] * 7
    else:
        grid = (n_seq, n_tiles)
        x_map = lambda i, j: (i, j, 0)
        s_block = (1, N_HEADS, HEAD, HEAD)
        s_map = lambda i, j: (i, 0, 0, 0)
        scratch = common + [tile_f32] * 6 + [pltpu.VMEM((N_HEADS, HEAD, HEAD), F32), pltpu.VMEM((rows, D), BF16)]
    seq_map = lambda i, j: (i, 0, 0)
    if cext.shape[0] == n_seq:
        in_seq_map, in_s_map = seq_map, s_map
    else:
        assert cext.shape[0] == h0.shape[0] == s0.shape[0] == seqs == 1
        in_seq_map = lambda i, j: (0, 0, 0)
        in_s_map = lambda i, j: (0, 0, 0, 0)
    return pl.pallas_call(
        kern,
        grid=grid,
        in_specs=[
            pl.BlockSpec((seqs, tokens, D), x_map),
            pl.BlockSpec((seqs, SUBLANES, D), in_seq_map),
            pl.BlockSpec((seqs, 1, D), in_seq_map),
            pl.BlockSpec(s_block, in_s_map),
            _resident((P_ROWS, D)),
            _resident(w_in.shape),
            _resident(w_ri.shape),
            _resident(w_a.shape),
            _resident(w_b.shape),
            _resident(w_out.shape),
        ],
        out_specs=[
            pl.BlockSpec((seqs, tokens, D), x_map),
            pl.BlockSpec((seqs, SUBLANES, D), seq_map),
            pl.BlockSpec((seqs, 1, D), seq_map),
            pl.BlockSpec(s_block, s_map),
        ],
        out_shape=[
            jax.ShapeDtypeStruct((n_seq, length, D), F32),
            jax.ShapeDtypeStruct((n_seq, SUBLANES, D), F32),
            jax.ShapeDtypeStruct((n_seq, 1, D), F32),
            jax.ShapeDtypeStruct((n_seq, N_HEADS, HEAD, HEAD), F32),
        ],
        scratch_shapes=scratch,
        compiler_params=pltpu.CompilerParams(
            dimension_semantics=("arbitrary", "arbitrary"),
            vmem_limit_bytes=VMEM_LIMIT_BYTES),
        name=name,
    )(x, cext, h0, s0, pvec, w_in, w_ri, w_a, w_b, w_out)


def _mlp_kernel(xa_ref, xb_ref, g_ref, wup_ref, wdn_ref, oa_ref, ob_ref, *, steps_a):
    def tile_mlp(x_ref, o_ref):
        x = x_ref[...]
        hn = _rms(x, g_ref[0:1, :]).astype(BF16)
        acc = jnp.zeros(x.shape, F32)
        for c in range(D_FF // D):
            t = _mm(hn, _unpack_rows(wup_ref[:, c * D:(c + 1) * D]))
            t = jnp.square(jnp.maximum(t, 0.0)).astype(BF16)
            acc = acc + _mm(t, _unpack_rows(wdn_ref[c * (D // 2):(c + 1) * (D // 2), :]))
        o_ref[...] = x + _rms(acc, g_ref[1:2, :])

    i = pl.program_id(0)

    @pl.when(i < steps_a)
    def _first():
        tile_mlp(xa_ref, oa_ref)

    @pl.when(i >= steps_a)
    def _second():
        tile_mlp(xb_ref, ob_ref)


def _mlp(xa, xb, gains, w_up, w_down, *, tile, name):
    na, nb = xa.shape[0], xb.shape[0]
    assert na % tile == 0 and nb % tile == 0
    steps_a, steps_b = na // tile, nb // tile
    a_map = lambda i: (jnp.minimum(i, steps_a - 1), 0)
    b_map = lambda i: (jnp.maximum(i - steps_a, 0), 0)
    return pl.pallas_call(
        functools.partial(_mlp_kernel, steps_a=steps_a),
        grid=(steps_a + steps_b,),
        in_specs=[
            pl.BlockSpec((tile, D), a_map),
            pl.BlockSpec((tile, D), b_map),
            pl.BlockSpec((2, D), lambda i: (0, 0), pipeline_mode=pl.Buffered(1)),
            pl.BlockSpec(w_up.shape, lambda i: (0, 0), pipeline_mode=pl.Buffered(1)),
            pl.BlockSpec(w_down.shape, lambda i: (0, 0), pipeline_mode=pl.Buffered(1)),
        ],
        out_specs=[pl.BlockSpec((tile, D), a_map), pl.BlockSpec((tile, D), b_map)],
        out_shape=[jax.ShapeDtypeStruct((na, D), F32), jax.ShapeDtypeStruct((nb, D), F32)],
        compiler_params=pltpu.CompilerParams(
            dimension_semantics=("arbitrary",),
            vmem_limit_bytes=VMEM_LIMIT_BYTES),
        name=name,
    )(xa, xb, gains, w_up, w_down)


def kernel(x_prompt, x_sample, state_conv, state_rglru, state_hgrn, meta_tokens, norm_gains, w_in,
           conv_w, conv_b, rg_w, rg_b, ig_w, ig_b, lru_lambda, hgrn_lb, hgrn_gnorm,
           w_branch_a, w_branch_b, w_out, w_up, w_down):
    bp, seq, _ = x_prompt.shape
    bs, dec_seq, _ = x_sample.shape
    layer = 0
    pvec = jnp.concatenate([
        norm_gains[layer], conv_w[layer], conv_b[layer][None], 0.5 * rg_b[layer][None], 0.5 * ig_b[layer][None],
        lru_lambda[layer][None], hgrn_lb[layer:layer + 2], hgrn_gnorm[layer][None],
        jnp.zeros((1, D), F32)], axis=0).astype(F32)
    halved = (C_GATE, C_Q, C_OG, C_MA, C_MB)
    scale_in = jnp.concatenate([jnp.full((1, D), 0.5 if g in halved else 1.0, F32) for g in range(8)], axis=1)

    gate_blocks = [w.reshape(N_HEADS * HEAD, HEAD) for w in (rg_w[layer], ig_w[layer])]
    w_in_b, w_ri, w_a, w_b, w_o, w_u, w_d = _pack_weights([
        ([w_in[layer]], scale_in),
        (gate_blocks, 0.5),
        ([w_branch_a[layer]], None),
        ([w_branch_b[layer]], None),
        ([w_out[layer]], None),
        ([w_up[layer]], None),
        ([w_down[layer]], None),
    ], name="pack_weights")
    w_ri = w_ri.reshape(N_HEADS, HEAD // 2, 2 * HEAD)
    weights = (pvec, w_in_b, w_ri, w_a, w_b, w_o)

    zeros_c = jnp.zeros((1, SUBLANES, D), F32)
    zeros_h = jnp.zeros((1, 1, D), F32)
    zeros_s = jnp.zeros((1, N_HEADS, HEAD, HEAD), F32)
    _, c_m, h_m, s_m = _mixer(meta_tokens[None].astype(F32), zeros_c, zeros_h, zeros_s, *weights,
                              seqs=1, tokens=N_META, reset_first=True, sub=N_META, name="mixer_meta")

    x1_p, c_p, h_p, s_p = _mixer(
        x_prompt,
        c_m, h_m, s_m,
        *weights, seqs=1, tokens=256, reset_first=False, sub=64, name="mixer_prompt")

    cext_s = jnp.pad(state_conv[layer], ((0, 0), (SUBLANES - 3, 0), (0, 0)))
    x1_s, c_s, h_s, s_s = _mixer(
        x_sample, cext_s, state_rglru[layer][:, None, :], state_hgrn[layer],
        *weights, seqs=32, tokens=dec_seq, reset_first=False, sub=8, name="mixer_sample")

    gains_mlp = norm_gains[layer, 2:4]
    y_p, y_s = _mlp(x1_p.reshape(bp * seq, D), x1_s.reshape(bs * dec_seq, D), gains_mlp, w_u, w_d,
                    tile=1024, name="mlp")

    return (y_p.reshape(bp, seq, D), y_s.reshape(bs, dec_seq, D),
            c_p[:, SUBLANES - 3:, :][None], h_p[:, 0, :][None], s_p[None],
            c_s[:, SUBLANES - 3:, :][None], h_s[:, 0, :][None], s_s[None])
```

```python
import functools

import jax
import jax.numpy as jnp
from jax import lax
from jax.experimental import pallas as pl
from jax.experimental.pallas import tpu as pltpu

D = 1024
N_HEADS = 8
HEAD = 128
N_META = 16
LRU_C = 8.0
EPS = 1e-6
D_FF = 4096
SUBLANES = 8
F32 = jnp.float32
BF16 = jnp.bfloat16

P_GAIN, P_CONV_W, P_CONV_B, P_RG_B, P_IG_B, P_LAMBDA, P_LB, P_GNORM, P_ROWS = 0, 4, 8, 9, 10, 11, 12, 14, 16

C_U, C_GATE, C_Q, C_F, C_I, C_OG, C_MA, C_MB = range(8)

VMEM_LIMIT_BYTES = 60 * 1024 * 1024
PACK_STEPS = 8
PACK_VMEM_LIMIT_BYTES = 40 * 1024 * 1024
SAFE_DECAY_RANGE = 64.0


def _rms(x, g):
    ms = jnp.mean(x * x, axis=-1, keepdims=True)
    return x * lax.rsqrt(ms + EPS) * g


def _sigmoid_of_twice(h):
    return 0.5 * jnp.tanh(h) + 0.5


def _silu_of_twice(h):
    return h + h * jnp.tanh(h)


_GELU_C1 = 2.0 * 0.7978845608028654
_GELU_C2 = 8.0 * 0.7978845608028654 * 0.044715


def _gelu_tanh_of_twice(h):
    return h + h * jnp.tanh(h * (_GELU_C1 + _GELU_C2 * (h * h)))


def _mm(a, b):
    return jnp.dot(a, b, preferred_element_type=F32)


def _mm_nt(a, b):
    return lax.dot_general(a, b, (((1,), (1,)), ((), ())), preferred_element_type=F32)


def _mm_tn(a, b):
    return lax.dot_general(a, b, (((0,), (0,)), ((), ())), preferred_element_type=F32)


def _mask_matmul(mask_bf, x):
    hi = x.astype(BF16)
    lo = (x - hi.astype(F32)).astype(BF16)
    return _mm(mask_bf, hi) + _mm(mask_bf, lo)


def _pack_kernel(*refs, groups):
    out_refs = refs[len(refs) - len(groups):]
    pos = 0
    for (n_parts, scale_kind, scalar), o_ref in zip(groups, out_refs):
        parts = [refs[pos + p][...] for p in range(n_parts)]
        pos += n_parts
        w = parts[0] if n_parts == 1 else jnp.concatenate(parts, axis=1)
        if scale_kind == "row":
            w = w * refs[pos][...]
            pos += 1
        elif scale_kind == "scalar":
            w = w * scalar
        o_ref[...] = pltpu.bitcast(w.astype(BF16), jnp.int32)


def _pack_weights(groups, *, name):
    operands, in_specs, out_specs, out_shapes, kinds = [], [], [], [], []
    for mats, scale in groups:
        k = mats[0].shape[0]
        tile = k // PACK_STEPS
        assert all(m.shape[0] == k for m in mats) and k % PACK_STEPS == 0 and tile % (2 * SUBLANES) == 0
        n = sum(m.shape[1] for m in mats)
        for m in mats:
            operands.append(m)
            in_specs.append(pl.BlockSpec((tile, m.shape[1]), lambda i: (i, 0)))
        if scale is None:
            kinds.append((len(mats), "none", None))
        elif isinstance(scale, float):
            kinds.append((len(mats), "scalar", scale))
        else:
            kinds.append((len(mats), "row", None))
            operands.append(scale)
            in_specs.append(pl.BlockSpec((1, n), lambda i: (0, 0)))
        out_specs.append(pl.BlockSpec((tile // 2, n), lambda i: (i, 0)))
        out_shapes.append(jax.ShapeDtypeStruct((k // 2, n), jnp.int32))
    return pl.pallas_call(
        functools.partial(_pack_kernel, groups=tuple(kinds)),
        grid=(PACK_STEPS,),
        in_specs=in_specs,
        out_specs=out_specs,
        out_shape=out_shapes,
        compiler_params=pltpu.CompilerParams(
            dimension_semantics=("arbitrary",), vmem_limit_bytes=PACK_VMEM_LIMIT_BYTES),
        name=name,
    )(*operands)


def _unpack_rows(words):
    return pltpu.bitcast(words, BF16)


def _exact_block(q_b, k_b, c_b, v_b):
    t_idx = lax.broadcasted_iota(jnp.int32, (SUBLANES, 1), 0)
    acc = jnp.zeros((SUBLANES, HEAD), F32)
    for s in range(SUBLANES):
        decay = jnp.exp(jnp.minimum(c_b - c_b[s:s + 1, :], 0.0))
        score = jnp.sum(q_b * k_b[s:s + 1, :] * decay, axis=-1, keepdims=True)
        acc = acc + jnp.where(t_idx >= s, score, 0.0) * v_b[s:s + 1, :]
    return acc


def _column_of(row):
    return jnp.transpose(jnp.broadcast_to(row, (HEAD, HEAD)))


def _mixer_kernel(x_ref, cext_ref, h0_ref, s0_ref, pvec_ref, win_ref, wri_ref, wa_ref, wb_ref, wout_ref,
                  x1_ref, cout_ref, hout_ref, sout_ref, *scratch,
                  seqs, tokens, n_tiles, reset_first, sub):
    j = pl.program_id(1)
    rows = seqs * tokens
    decode = seqs > 1
    if decode:
        ubuf, ga_s, sgb_s, sog_s, qf_h, kk_h, cum_h, v_h, tot_h, oin_h, oi_h = scratch
    else:
        ubuf, ga_s, sgb_s, sog_s, hbuf, qf_s, kk_s, cum_s, v_s, o_s, sold_s, xn_s = scratch

    pv = pvec_ref[...]

    def prow(r):
        return pv[r:r + 1, :]

    def wcol(g):
        return _unpack_rows(win_ref[:, g * D:(g + 1) * D])

    def head_cols(hd):
        return slice(hd * HEAD, (hd + 1) * HEAD)

    def front():
        x = x_ref[...].reshape(rows, D)
        xn = _rms(x, prow(P_GAIN + 0)).astype(BF16)
        row_id = lax.broadcasted_iota(jnp.int32, (rows, 1), 0)

        u = _mm(xn, wcol(C_U))
        ubuf[:, SUBLANES:SUBLANES + tokens, :] = u.reshape(seqs, tokens, D)
        if decode:
            merge_gates(xn)
        else:
            xn_s[...] = xn
        uc = prow(P_CONV_B) + prow(P_CONV_W + 3) * u
        for k in range(1, 4):
            shifted = ubuf[:, SUBLANES - k:SUBLANES - k + tokens, :].reshape(rows, D)
            uc = uc + prow(P_CONV_W + 3 - k) * shifted
        tail = ubuf[:, tokens:tokens + SUBLANES, :]
        ubuf[:, 0:SUBLANES, :] = tail
        cout_ref[...] = tail

        ucb = uc.astype(BF16)
        r_parts, i_parts = [], []
        for b in range(N_HEADS):
            ri = _mm(ucb[:, head_cols(b)], _unpack_rows(wri_ref[b]))
            r_parts.append(ri[:, :HEAD])
            i_parts.append(ri[:, HEAD:])
        r_pre = jnp.concatenate(r_parts, axis=1)
        i_pre = jnp.concatenate(i_parts, axis=1)
        if not decode:
            hbuf[...] = r_pre
            o_s[...] = i_pre
        hgrn_operands(xn)
        if not decode:
            r_pre = hbuf[...]
            i_pre = o_s[...]
        r_gate = _sigmoid_of_twice(r_pre + prow(P_RG_B))
        i_gate = _sigmoid_of_twice(i_pre + prow(P_IG_B))
        lam = prow(P_LAMBDA)
        softplus_neg_lam = jnp.maximum(-lam, 0.0) + jnp.log1p(jnp.exp(-jnp.abs(lam)))
        log_a = (-LRU_C) * r_gate * softplus_neg_lam
        a_cum = jnp.exp(log_a)
        th = jnp.tanh(log_a)
        sq = -2.0 * th / (1.0 - th)
        mult = jnp.where(sq > 0.0, sq * lax.rsqrt(sq), 0.0)
        if reset_first:
            mult = jnp.where(jnp.logical_and(row_id == 0, j == 0), 1.0, mult)
        b_cum = mult * i_gate * uc

        a_cum = a_cum.reshape(rows // SUBLANES, SUBLANES, D)
        b_cum = b_cum.reshape(rows // SUBLANES, SUBLANES, D)
        sublane = lax.broadcasted_iota(jnp.int32, (1, SUBLANES, 1), 1)
        for s in (1, 2, 4):
            keep = sublane >= s
            a_prev = jnp.where(keep, pltpu.roll(a_cum, s, 1), 1.0)
            b_prev = jnp.where(keep, pltpu.roll(b_cum, s, 1), 0.0)
            b_cum = a_cum * b_prev + b_cum
            a_cum = a_cum * a_prev
        a_cum = a_cum.reshape(rows, D)
        b_cum = b_cum.reshape(rows, D)

        if decode:
            h_in = jnp.broadcast_to(h0_ref[...], (seqs, SUBLANES, D)).reshape(rows, D)
            h_all = a_cum * h_in + b_cum
            hout_ref[...] = h_all.reshape(seqs, SUBLANES, D)[:, SUBLANES - 1:SUBLANES, :]
        else:
            h = hout_ref[0]
            for g in range(rows // SUBLANES):
                sl = slice(g * SUBLANES, (g + 1) * SUBLANES)
                hg = a_cum[sl] * h + b_cum[sl]
                hbuf[sl, :] = hg
                h = hg[SUBLANES - 1:SUBLANES, :]
            hout_ref[0] = h
            h_all = hbuf[...]

        ya = (h_all * _gelu_tanh_of_twice(_mm(xn, wcol(C_GATE)))).astype(BF16)
        ga_s[...] = _sigmoid_of_twice(_mm(xn, wcol(C_MA))) * _mm(ya, _unpack_rows(wa_ref[...]))

    def merge_gates(xn):
        sgb_s[...] = _sigmoid_of_twice(_mm(xn, wcol(C_MB)))
        sog_s[...] = _silu_of_twice(_mm(xn, wcol(C_OG)))

    def hgrn_operands(xn):
        qf = _silu_of_twice(_mm(xn, wcol(C_Q)))
        lb_raw = pv[P_LB:P_LB + 2, :]
        lb_exp = jnp.exp(lb_raw - jnp.max(lb_raw, axis=0, keepdims=True))
        lb = lb_exp[0:1, :] / jnp.sum(lb_exp, axis=0, keepdims=True)
        fg = lb + (1.0 - lb) / (1.0 + jnp.exp(-_mm(xn, wcol(C_F))))
        logf = jnp.log(fg)
        kk = 1.0 - fg
        v = _mm(xn, wcol(C_I))

        ri2 = lax.broadcasted_iota(jnp.int32, (rows, rows), 0)
        ci2 = lax.broadcasted_iota(jnp.int32, (rows, rows), 1)
        if decode:
            same_seq = (ri2 // tokens) == (ci2 // tokens)
            total = _mask_matmul(same_seq.astype(BF16), logf)
            cum = _mask_matmul(jnp.logical_and(ci2 <= ri2, same_seq).astype(BF16), logf)
            for hd in range(N_HEADS):
                hs = head_cols(hd)
                v_h[hd] = v[:, hs]
                tot_h[hd] = total[:, hs]
                qf_h[hd] = qf[:, hs]
                kk_h[hd] = kk[:, hs]
                cum_h[hd] = cum[:, hs]
        else:
            qf_s[...] = qf
            kk_s[...] = kk
            v_s[...] = v
            cum_s[...] = _mask_matmul((ci2 <= ri2).astype(BF16), logf)

    def head_norm(o_h):
        ms = jnp.mean(o_h * o_h, axis=-1, keepdims=True)
        return o_h * lax.rsqrt(ms + EPS)

    def finish(o_heads):
        o_n = jnp.concatenate([head_norm(o_h) for o_h in o_heads], axis=1) * prow(P_GNORM)
        yb = (o_n * sog_s[...]).astype(BF16)
        mixed = ga_s[...] + sgb_s[...] * _mm(yb, _unpack_rows(wb_ref[...]))
        z = _mm(mixed.astype(BF16), _unpack_rows(wout_ref[...]))
        x1 = x_ref[...].reshape(rows, D) + _rms(z, prow(P_GAIN + 1))
        x1_ref[...] = x1.reshape(seqs, tokens, D)

    def back_prompt():
        xnb = xn_s[...]
        qf = qf_s[...]
        kk = kk_s[...]
        cum = cum_s[...]
        vb = v_s[...].astype(BF16)
        n_sub = rows // sub
        last = cum[rows - 1:rows, :]
        qib = (qf * jnp.exp(cum)).astype(BF16)
        ksb = (kk * jnp.exp(last - cum)).astype(BF16)
        decay_row = jnp.exp(last)
        def score_blocks(hd):
            hs = head_cols(hd)
            parts = []
            for i in range(n_sub):
                r0 = i * sub
                r1 = r0 + sub
                width = min(rows, -(-r1 // HEAD) * HEAD)
                ref_row = cum[r0 - 1:r0, hs] if i > 0 else jnp.zeros((1, HEAD), F32)
                qd = (qf[r0:r1, hs] * jnp.exp(cum[r0:r1, hs] - ref_row)).astype(BF16)
                arg = ref_row - cum[0:width, hs]
                if width > r1:
                    arg = jnp.where(lax.broadcasted_iota(jnp.int32, (width, 1), 0) < r1, arg, 0.0)
                kdi = (kk[0:width, hs] * jnp.exp(arg)).astype(BF16)
                att = _mm_nt(qd, kdi)
                rr = lax.broadcasted_iota(jnp.int32, (sub, width), 0) + r0
                cc = lax.broadcasted_iota(jnp.int32, (sub, width), 1)
                att = jnp.where(cc <= rr, att, 0.0).astype(BF16)
                if width < rows:
                    att = jnp.concatenate([att, jnp.zeros((sub, rows - width), BF16)], axis=1)
                parts.append(att)
            return jnp.concatenate(parts, axis=0) if n_sub > 1 else parts[0]

        def state_and_gate_quarter(hd, s_old):
            hs = head_cols(hd)
            upd = _mm_tn(ksb[:, hs], vb[:, hs])
            sout_ref[0, hd] = _column_of(decay_row[:, hs]) * s_old + upd
            sold_s[hd] = s_old
            quarter = D // 4
            grp, part = (C_MB, hd) if hd < 4 else (C_OG, hd - 4)
            cols = slice(part * quarter, (part + 1) * quarter)
            proj = _mm(xnb, _unpack_rows(win_ref[:, grp * D + part * quarter:grp * D + (part + 1) * quarter]))
            if hd < 4:
                sgb_s[:, cols] = _sigmoid_of_twice(proj)
            else:
                sog_s[:, cols] = _silu_of_twice(proj)

        att_next = score_blocks(0)
        for hd in range(N_HEADS):
            hs = head_cols(hd)
            att_full = att_next
            s_old = sout_ref[0, hd]
            state_and_gate_quarter(hd, s_old)
            if hd + 1 < N_HEADS:
                att_next = score_blocks(hd + 1)
            if rows % HEAD == 0:
                o_h = _mm(jnp.concatenate([att_full, qib[:, hs]], axis=1),
                          jnp.concatenate([vb[:, hs], s_old.astype(BF16)], axis=0))
            else:
                o_h = _mm(att_full, vb[:, hs]) + _mm(qib[:, hs], s_old.astype(BF16))
            o_s[:, hs] = o_h

        worst = jnp.zeros((1, D), F32)
        for i in range(n_sub):
            start = cum[i * sub - 1:i * sub, :] if i > 0 else jnp.zeros((1, D), F32)
            worst = jnp.maximum(worst, start - cum[(i + 1) * sub - 1:(i + 1) * sub, :])
        out_of_range = jnp.max(worst) > SAFE_DECAY_RANGE

        @pl.when(out_of_range)
        def _exact_scores():
            row_i = lax.broadcasted_iota(jnp.int32, (rows, 1), 0)
            for hd in range(N_HEADS):
                hs = head_cols(hd)
                qib_h = (qf_s[:, hs] * jnp.exp(cum_s[:, hs])).astype(BF16)
                o_s[:, hs] = _mm(qib_h, sold_s[hd].astype(BF16))

                def block_body(b, carry, hs=hs):
                    r0 = pl.multiple_of(b * SUBLANES, SUBLANES)
                    rs = pl.ds(r0, SUBLANES)
                    q_b, k_b, c_b, v_b = qf_s[rs, hs], kk_s[rs, hs], cum_s[rs, hs], v_s[rs, hs]
                    prev_start = pl.multiple_of(jnp.maximum(r0 - SUBLANES, 0), SUBLANES)
                    before = cum_s[pl.ds(prev_start, SUBLANES), hs][SUBLANES - 1:SUBLANES, :]
                    ref_row = jnp.where(b > 0, before, 0.0)
                    qd = (q_b * jnp.exp(c_b - ref_row)).astype(BF16)
                    arg = jnp.minimum(ref_row - cum_s[:, hs], 0.0)
                    kd = jnp.where(row_i < r0, kk_s[:, hs] * jnp.exp(arg), 0.0).astype(BF16)
                    earlier = _mm(_mm_nt(qd, kd).astype(BF16), v_s[:, hs].astype(BF16))
                    o_s[rs, hs] = o_s[rs, hs] + earlier + _exact_block(q_b, k_b, c_b, v_b)
                    return carry

                lax.fori_loop(0, rows // SUBLANES, block_body, 0)

        finish([o_s[:, head_cols(hd)] for hd in range(N_HEADS)])

    def decode_state_step():
        def seq_body(g, carry):
            rsl = pl.ds(pl.multiple_of(g * SUBLANES, SUBLANES), SUBLANES)
            s_old = s0_ref[g, 0]
            c_b = cum_h[j, rsl, :]
            total_b = tot_h[j, rsl, :]
            qi = qf_h[j, rsl, :] * jnp.exp(c_b)
            ks = kk_h[j, rsl, :] * jnp.exp(total_b - c_b)
            oin_h[j, rsl, :] = _mm(qi.astype(BF16), s_old.astype(BF16))
            upd = _mm_tn(ks.astype(BF16), v_h[j, rsl, :].astype(BF16))
            decay = _column_of(jnp.exp(total_b[0:1, :]))
            sout_ref[g, 0] = decay * s_old + upd
            return carry

        lax.fori_loop(0, seqs, seq_body, 0, unroll=SUBLANES)

    def back_decode():
        ri = lax.broadcasted_iota(jnp.int32, (rows, rows), 0)
        ci = lax.broadcasted_iota(jnp.int32, (rows, rows), 1)
        causal = jnp.logical_and(ci <= ri, (ri // tokens) == (ci // tokens))
        for hd in range(N_HEADS):
            c_h = cum_h[hd]
            scores = _mm_nt((qf_h[hd] * jnp.exp(c_h)).astype(BF16), (kk_h[hd] * jnp.exp(-c_h)).astype(BF16))
            att = jnp.where(causal, scores, 0.0).astype(BF16)
            oi_h[hd] = _mm(att, v_h[hd].astype(BF16))

        worst = jnp.zeros((1, HEAD), F32)
        for hd in range(N_HEADS):
            worst = jnp.maximum(worst, jnp.max(-tot_h[hd], axis=0, keepdims=True))
        out_of_range = jnp.max(worst) > SAFE_DECAY_RANGE

        @pl.when(out_of_range)
        def _exact_scores():
            for hd in range(N_HEADS):
                def seq_body(g, carry, hd=hd):
                    rs = pl.ds(pl.multiple_of(g * SUBLANES, SUBLANES), SUBLANES)
                    oi_h[hd, rs, :] = _exact_block(qf_h[hd, rs, :], kk_h[hd, rs, :], cum_h[hd, rs, :], v_h[hd, rs, :])
                    return carry

                lax.fori_loop(0, seqs, seq_body, 0)

        finish([oin_h[hd] + oi_h[hd] for hd in range(N_HEADS)])

    if decode:
        @pl.when(j == 0)
        def _front_region():
            ubuf[:, 0:SUBLANES, :] = cext_ref[...]
            front()

        decode_state_step()

        @pl.when(j == N_HEADS - 1)
        def _back_region():
            back_decode()
    else:
        @pl.when(j == 0)
        def _init():
            ubuf[:, 0:SUBLANES, :] = cext_ref[...]
            hout_ref[...] = h0_ref[...]
            sout_ref[...] = s0_ref[...]

        front()

        @pl.when(j >= 0)
        def _back_region():
            back_prompt()


def _resident(shape):
    return pl.BlockSpec(shape, lambda i, j: (0,) * len(shape), pipeline_mode=pl.Buffered(1))


def _mixer(x, cext, h0, s0, pvec, w_in, w_ri, w_a, w_b, w_out, *, seqs, tokens, reset_first, sub, name):
    n_seq, length, _ = x.shape
    n_tiles = length // tokens
    decode = seqs > 1
    assert n_seq % seqs == 0 and length % tokens == 0 and tokens % SUBLANES == 0
    assert not decode or (n_tiles == 1 and tokens == SUBLANES)
    assert (seqs * tokens) % sub == 0
    rows = seqs * tokens
    kern = functools.partial(_mixer_kernel, seqs=seqs, tokens=tokens, n_tiles=n_tiles,
                             reset_first=reset_first, sub=sub)
    tile_f32 = pltpu.VMEM((rows, D), F32)
    by_head_f32 = pltpu.VMEM((N_HEADS, rows, HEAD), F32)
    common = [pltpu.VMEM((seqs, SUBLANES + tokens, D), F32),
              tile_f32, tile_f32, tile_f32]
    if decode:
        grid = (n_seq // seqs, N_HEADS)
        x_map = lambda i, j: (i, 0, 0)
        s_block = (seqs, 1, HEAD, HEAD)
        s_map = lambda i, j: (i, j, 0, 0)
        scratch = common + [by_head_f32] * 7
    else:
        grid = (n_seq, n_tiles)
        x_map = lambda i, j: (i, j, 0)
        s_block = (1, N_HEADS, HEAD, HEAD)
        s_map = lambda i, j: (i, 0, 0, 0)
        scratch = common + [tile_f32] * 6 + [pltpu.VMEM((N_HEADS, HEAD, HEAD), F32), pltpu.VMEM((rows, D), BF16)]
    seq_map = lambda i, j: (i, 0, 0)
    if cext.shape[0] == n_seq:
        in_seq_map, in_s_map = seq_map, s_map
    else:
        assert cext.shape[0] == h0.shape[0] == s0.shape[0] == seqs == 1
        in_seq_map = lambda i, j: (0, 0, 0)
        in_s_map = lambda i, j: (0, 0, 0, 0)
    return pl.pallas_call(
        kern,
        grid=grid,
        in_specs=[
            pl.BlockSpec((seqs, tokens, D), x_map),
            pl.BlockSpec((seqs, SUBLANES, D), in_seq_map),
            pl.BlockSpec((seqs, 1, D), in_seq_map),
            pl.BlockSpec(s_block, in_s_map),
            _resident((P_ROWS, D)),
            _resident(w_in.shape),
            _resident(w_ri.shape),
            _resident(w_a.shape),
            _resident(w_b.shape),
            _resident(w_out.shape),
        ],
        out_specs=[
            pl.BlockSpec((seqs, tokens, D), x_map),
            pl.BlockSpec((seqs, SUBLANES, D), seq_map),
            pl.BlockSpec((seqs, 1, D), seq_map),
            pl.BlockSpec(s_block, s_map),
        ],
        out_shape=[
            jax.ShapeDtypeStruct((n_seq, length, D), F32),
            jax.ShapeDtypeStruct((n_seq, SUBLANES, D), F32),
            jax.ShapeDtypeStruct((n_seq, 1, D), F32),
            jax.ShapeDtypeStruct((n_seq, N_HEADS, HEAD, HEAD), F32),
        ],
        scratch_shapes=scratch,
        compiler_params=pltpu.CompilerParams(
            dimension_semantics=("arbitrary", "arbitrary"),
            vmem_limit_bytes=VMEM_LIMIT_BYTES),
        name=name,
    )(x, cext, h0, s0, pvec, w_in, w_ri, w_a, w_b, w_out)


def _mlp_kernel(xa_ref, xb_ref, g_ref, wup_ref, wdn_ref, oa_ref, ob_ref, *, steps_a):
    def tile_mlp(x_ref, o_ref):
        x = x_ref[...]
        hn = _rms(x, g_ref[0:1, :]).astype(BF16)
        acc = jnp.zeros(x.shape, F32)
        for c in range(D_FF // D):
            t = _mm(hn, _unpack_rows(wup_ref[:, c * D:(c + 1) * D]))
            t = jnp.square(jnp.maximum(t, 0.0)).astype(BF16)
            acc = acc + _mm(t, _unpack_rows(wdn_ref[c * (D // 2):(c + 1) * (D // 2), :]))
        o_ref[...] = x + _rms(acc, g_ref[1:2, :])

    i = pl.program_id(0)

    @pl.when(i < steps_a)
    def _first():
        tile_mlp(xa_ref, oa_ref)

    @pl.when(i >= steps_a)
    def _second():
        tile_mlp(xb_ref, ob_ref)


def _mlp(xa, xb, gains, w_up, w_down, *, tile, name):
    na, nb = xa.shape[0], xb.shape[0]
    assert na % tile == 0 and nb % tile == 0
    steps_a, steps_b = na // tile, nb // tile
    a_map = lambda i: (jnp.minimum(i, steps_a - 1), 0)
    b_map = lambda i: (jnp.maximum(i - steps_a, 0), 0)
    return pl.pallas_call(
        functools.partial(_mlp_kernel, steps_a=steps_a),
        grid=(steps_a + steps_b,),
        in_specs=[
            pl.BlockSpec((tile, D), a_map),
            pl.BlockSpec((tile, D), b_map),
            pl.BlockSpec((2, D), lambda i: (0, 0), pipeline_mode=pl.Buffered(1)),
            pl.BlockSpec(w_up.shape, lambda i: (0, 0), pipeline_mode=pl.Buffered(1)),
            pl.BlockSpec(w_down.shape, lambda i: (0, 0), pipeline_mode=pl.Buffered(1)),
        ],
        out_specs=[pl.BlockSpec((tile, D), a_map), pl.BlockSpec((tile, D), b_map)],
        out_shape=[jax.ShapeDtypeStruct((na, D), F32), jax.ShapeDtypeStruct((nb, D), F32)],
        compiler_params=pltpu.CompilerParams(
            dimension_semantics=("arbitrary",),
            vmem_limit_bytes=VMEM_LIMIT_BYTES),
        name=name,
    )(xa, xb, gains, w_up, w_down)


def kernel(x_prompt, x_sample, state_conv, state_rglru, state_hgrn, meta_tokens, norm_gains, w_in,
           conv_w, conv_b, rg_w, rg_b, ig_w, ig_b, lru_lambda, hgrn_lb, hgrn_gnorm,
           w_branch_a, w_branch_b, w_out, w_up, w_down):
    bp, seq, _ = x_prompt.shape
    bs, dec_seq, _ = x_sample.shape
    layer = 0
    pvec = jnp.concatenate([
        norm_gains[layer], conv_w[layer], conv_b[layer][None], 0.5 * rg_b[layer][None], 0.5 * ig_b[layer][None],
        lru_lambda[layer][None], hgrn_lb[layer:layer + 2], hgrn_gnorm[layer][None],
        jnp.zeros((1, D), F32)], axis=0).astype(F32)
    halved = (C_GATE, C_Q, C_OG, C_MA, C_MB)
    scale_in = jnp.concatenate([jnp.full((1, D), 0.5 if g in halved else 1.0, F32) for g in range(8)], axis=1)

    gate_blocks = [w.reshape(N_HEADS * HEAD, HEAD) for w in (rg_w[layer], ig_w[layer])]
    w_in_b, w_ri, w_a, w_b, w_o, w_u, w_d = _pack_weights([
        ([w_in[layer]], scale_in),
        (gate_blocks, 0.5),
        ([w_branch_a[layer]], None),
        ([w_branch_b[layer]], None),
        ([w_out[layer]], None),
        ([w_up[layer]], None),
        ([w_down[layer]], None),
    ], name="pack_weights")
    w_ri = w_ri.reshape(N_HEADS, HEAD // 2, 2 * HEAD)
    weights = (pvec, w_in_b, w_ri, w_a, w_b, w_o)

    zeros_c = jnp.zeros((1, SUBLANES, D), F32)
    zeros_h = jnp.zeros((1, 1, D), F32)
    zeros_s = jnp.zeros((1, N_HEADS, HEAD, HEAD), F32)
    _, c_m, h_m, s_m = _mixer(meta_tokens[None].astype(F32), zeros_c, zeros_h, zeros_s, *weights,
                              seqs=1, tokens=N_META, reset_first=True, sub=N_META, name="mixer_meta")

    x1_p, c_p, h_p, s_p = _mixer(
        x_prompt,
        c_m, h_m, s_m,
        *weights, seqs=1, tokens=256, reset_first=False, sub=64, name="mixer_prompt")

    cext_s = jnp.pad(state_conv[layer], ((0, 0), (SUBLANES - 3, 0), (0, 0)))
    x1_s, c_s, h_s, s_s = _mixer(
        x_sample, cext_s, state_rglru[layer][:, None, :], state_hgrn[layer],
        *weights, seqs=32, tokens=dec_seq, reset_first=False, sub=8, name="mixer_sample")

    gains_mlp = norm_gains[layer, 2:4]
    y_p, y_s = _mlp(x1_p.reshape(bp * seq, D), x1_s.reshape(bs * dec_seq, D), gains_mlp, w_u, w_d,
                    tile=1024, name="mlp")

    return (y_p.reshape(bp, seq, D), y_s.reshape(bs, dec_seq, D),
            c_p[:, SUBLANES - 3:, :][None], h_p[:, 0, :][None], s_p[None],
            c_s[:, SUBLANES - 3:, :][None], h_s[:, 0, :][None], s_s[None])
```

```python
import functools

import jax
import jax.numpy as jnp
from jax import lax
from jax.experimental import pallas as pl
from jax.experimental.pallas import tpu as pltpu

D = 1024
N_HEADS = 8
HEAD = 128
N_META = 16
LRU_C = 8.0
EPS = 1e-6
D_FF = 4096
SUBLANES = 8
F32 = jnp.float32
BF16 = jnp.bfloat16

P_GAIN, P_CONV_W, P_CONV_B, P_RG_B, P_IG_B, P_LAMBDA, P_LB, P_GNORM, P_ROWS = 0, 4, 8, 9, 10, 11, 12, 14, 16

C_U, C_GATE, C_Q, C_F, C_I, C_OG, C_MA, C_MB = range(8)

VMEM_LIMIT_BYTES = 60 * 1024 * 1024
PACK_STEPS = 8
PACK_VMEM_LIMIT_BYTES = 40 * 1024 * 1024
SAFE_DECAY_RANGE = 64.0


def _rms(x, g):
    ms = jnp.mean(x * x, axis=-1, keepdims=True)
    return x * lax.rsqrt(ms + EPS) * g


def _sigmoid_of_twice(h):
    return 0.5 * jnp.tanh(h) + 0.5


def _silu_of_twice(h):
    return h + h * jnp.tanh(h)


_GELU_C1 = 2.0 * 0.7978845608028654
_GELU_C2 = 8.0 * 0.7978845608028654 * 0.044715


def _gelu_tanh_of_twice(h):
    return h + h * jnp.tanh(h * (_GELU_C1 + _GELU_C2 * (h * h)))


def _mm(a, b):
    return jnp.dot(a, b, preferred_element_type=F32)


def _mm_nt(a, b):
    return lax.dot_general(a, b, (((1,), (1,)), ((), ())), preferred_element_type=F32)


def _mm_tn(a, b):
    return lax.dot_general(a, b, (((0,), (0,)), ((), ())), preferred_element_type=F32)


def _mask_matmul(mask_bf, x):
    hi = x.astype(BF16)
    lo = (x - hi.astype(F32)).astype(BF16)
    return _mm(mask_bf, hi) + _mm(mask_bf, lo)


def _pack_kernel(*refs, groups):
    out_refs = refs[len(refs) - len(groups):]
    pos = 0
    for (n_parts, scale_kind, scalar), o_ref in zip(groups, out_refs):
        parts = [refs[pos + p][...] for p in range(n_parts)]
        pos += n_parts
        w = parts[0] if n_parts == 1 else jnp.concatenate(parts, axis=1)
        if scale_kind == "row":
            w = w * refs[pos][...]
            pos += 1
        elif scale_kind == "scalar":
            w = w * scalar
        o_ref[...] = pltpu.bitcast(w.astype(BF16), jnp.int32)


def _pack_weights(groups, *, name):
    operands, in_specs, out_specs, out_shapes, kinds = [], [], [], [], []
    for mats, scale in groups:
        k = mats[0].shape[0]
        tile = k // PACK_STEPS
        assert all(m.shape[0] == k for m in mats) and k % PACK_STEPS == 0 and tile % (2 * SUBLANES) == 0
        n = sum(m.shape[1] for m in mats)
        for m in mats:
            operands.append(m)
            in_specs.append(pl.BlockSpec((tile, m.shape[1]), lambda i: (i, 0)))
        if scale is None:
            kinds.append((len(mats), "none", None))
        elif isinstance(scale, float):
            kinds.append((len(mats), "scalar", scale))
        else:
            kinds.append((len(mats), "row", None))
            operands.append(scale)
            in_specs.append(pl.BlockSpec((1, n), lambda i: (0, 0)))
        out_specs.append(pl.BlockSpec((tile // 2, n), lambda i: (i, 0)))
        out_shapes.append(jax.ShapeDtypeStruct((k // 2, n), jnp.int32))
    return pl.pallas_call(
        functools.partial(_pack_kernel, groups=tuple(kinds)),
        grid=(PACK_STEPS,),
        in_specs=in_specs,
        out_specs=out_specs,
        out_shape=out_shapes,
        compiler_params=pltpu.CompilerParams(
            dimension_semantics=("arbitrary",), vmem_limit_bytes=PACK_VMEM_LIMIT_BYTES),
        name=name,
    )(*operands)


def _unpack_rows(words):
    return pltpu.bitcast(words, BF16)


def _exact_block(q_b, k_b, c_b, v_b):
    t_idx = lax.broadcasted_iota(jnp.int32, (SUBLANES, 1), 0)
    acc = jnp.zeros((SUBLANES, HEAD), F32)
    for s in range(SUBLANES):
        decay = jnp.exp(jnp.minimum(c_b - c_b[s:s + 1, :], 0.0))
        score = jnp.sum(q_b * k_b[s:s + 1, :] * decay, axis=-1, keepdims=True)
        acc = acc + jnp.where(t_idx >= s, score, 0.0) * v_b[s:s + 1, :]
    return acc


def _column_of(row):
    return jnp.transpose(jnp.broadcast_to(row, (HEAD, HEAD)))


def _mixer_kernel(x_ref, cext_ref, h0_ref, s0_ref, pvec_ref, win_ref, wri_ref, wa_ref, wb_ref, wout_ref,
                  x1_ref, cout_ref, hout_ref, sout_ref, *scratch,
                  seqs, tokens, n_tiles, reset_first, sub):
    j = pl.program_id(1)
    rows = seqs * tokens
    decode = seqs > 1
    if decode:
        ubuf, ga_s, sgb_s, sog_s, qf_h, kk_h, cum_h, v_h, tot_h, oin_h, oi_h = scratch
    else:
        ubuf, ga_s, sgb_s, sog_s, hbuf, qf_s, kk_s, cum_s, v_s, o_s, sold_s, xn_s = scratch

    pv = pvec_ref[...]

    def prow(r):
        return pv[r:r + 1, :]

    def wcol(g):
        return _unpack_rows(win_ref[:, g * D:(g + 1) * D])

    def head_cols(hd):
        return slice(hd * HEAD, (hd + 1) * HEAD)

    def front():
        x = x_ref[...].reshape(rows, D)
        xn = _rms(x, prow(P_GAIN + 0)).astype(BF16)
        row_id = lax.broadcasted_iota(jnp.int32, (rows, 1), 0)

        u = _mm(xn, wcol(C_U))
        ubuf[:, SUBLANES:SUBLANES + tokens, :] = u.reshape(seqs, tokens, D)
        if decode:
            merge_gates(xn)
        else:
            xn_s[...] = xn
        uc = prow(P_CONV_B) + prow(P_CONV_W + 3) * u
        for k in range(1, 4):
            shifted = ubuf[:, SUBLANES - k:SUBLANES - k + tokens, :].reshape(rows, D)
            uc = uc + prow(P_CONV_W + 3 - k) * shifted
        tail = ubuf[:, tokens:tokens + SUBLANES, :]
        ubuf[:, 0:SUBLANES, :] = tail
        cout_ref[...] = tail

        ucb = uc.astype(BF16)
        r_parts, i_parts = [], []
        for b in range(N_HEADS):
            ri = _mm(ucb[:, head_cols(b)], _unpack_rows(wri_ref[b]))
            r_parts.append(ri[:, :HEAD])
            i_parts.append(ri[:, HEAD:])
        r_pre = jnp.concatenate(r_parts, axis=1)
        i_pre = jnp.concatenate(i_parts, axis=1)
        if not decode:
            hbuf[...] = r_pre
            o_s[...] = i_pre
        hgrn_operands(xn)
        if not decode:
            r_pre = hbuf[...]
            i_pre = o_s[...]
        r_gate = _sigmoid_of_twice(r_pre + prow(P_RG_B))
        i_gate = _sigmoid_of_twice(i_pre + prow(P_IG_B))
        lam = prow(P_LAMBDA)
        softplus_neg_lam = jnp.maximum(-lam, 0.0) + jnp.log1p(jnp.exp(-jnp.abs(lam)))
        log_a = (-LRU_C) * r_gate * softplus_neg_lam
        a_cum = jnp.exp(log_a)
        th = jnp.tanh(log_a)
        sq = -2.0 * th / (1.0 - th)
        mult = jnp.where(sq > 0.0, sq * lax.rsqrt(sq), 0.0)
        if reset_first:
            mult = jnp.where(jnp.logical_and(row_id == 0, j == 0), 1.0, mult)
        b_cum = mult * i_gate * uc

        a_cum = a_cum.reshape(rows // SUBLANES, SUBLANES, D)
        b_cum = b_cum.reshape(rows // SUBLANES, SUBLANES, D)
        sublane = lax.broadcasted_iota(jnp.int32, (1, SUBLANES, 1), 1)
        for s in (1, 2, 4):
            keep = sublane >= s
            a_prev = jnp.where(keep, pltpu.roll(a_cum, s, 1), 1.0)
            b_prev = jnp.where(keep, pltpu.roll(b_cum, s, 1), 0.0)
            b_cum = a_cum * b_prev + b_cum
            a_cum = a_cum * a_prev
        a_cum = a_cum.reshape(rows, D)
        b_cum = b_cum.reshape(rows, D)

        if decode:
            h_in = jnp.broadcast_to(h0_ref[...], (seqs, SUBLANES, D)).reshape(rows, D)
            h_all = a_cum * h_in + b_cum
            hout_ref[...] = h_all.reshape(seqs, SUBLANES, D)[:, SUBLANES - 1:SUBLANES, :]
        else:
            h = hout_ref[0]
            for g in range(rows // SUBLANES):
                sl = slice(g * SUBLANES, (g + 1) * SUBLANES)
                hg = a_cum[sl] * h + b_cum[sl]
                hbuf[sl, :] = hg
                h = hg[SUBLANES - 1:SUBLANES, :]
            hout_ref[0] = h
            h_all = hbuf[...]

        ya = (h_all * _gelu_tanh_of_twice(_mm(xn, wcol(C_GATE)))).astype(BF16)
        ga_s[...] = _sigmoid_of_twice(_mm(xn, wcol(C_MA))) * _mm(ya, _unpack_rows(wa_ref[...]))

    def merge_gates(xn):
        sgb_s[...] = _sigmoid_of_twice(_mm(xn, wcol(C_MB)))
        sog_s[...] = _silu_of_twice(_mm(xn, wcol(C_OG)))

    def hgrn_operands(xn):
        qf = _silu_of_twice(_mm(xn, wcol(C_Q)))
        lb_raw = pv[P_LB:P_LB + 2, :]
        lb_exp = jnp.exp(lb_raw - jnp.max(lb_raw, axis=0, keepdims=True))
        lb = lb_exp[0:1, :] / jnp.sum(lb_exp, axis=0, keepdims=True)
        fg = lb + (1.0 - lb) / (1.0 + jnp.exp(-_mm(xn, wcol(C_F))))
        logf = jnp.log(fg)
        kk = 1.0 - fg
        v = _mm(xn, wcol(C_I))

        ri2 = lax.broadcasted_iota(jnp.int32, (rows, rows), 0)
        ci2 = lax.broadcasted_iota(jnp.int32, (rows, rows), 1)
        if decode:
            same_seq = (ri2 // tokens) == (ci2 // tokens)
            total = _mask_matmul(same_seq.astype(BF16), logf)
            cum = _mask_matmul(jnp.logical_and(ci2 <= ri2, same_seq).astype(BF16), logf)
            for hd in range(N_HEADS):
                hs = head_cols(hd)
                v_h[hd] = v[:, hs]
                tot_h[hd] = total[:, hs]
                qf_h[hd] = qf[:, hs]
                kk_h[hd] = kk[:, hs]
                cum_h[hd] = cum[:, hs]
        else:
            qf_s[...] = qf
            kk_s[...] = kk
            v_s[...] = v
            cum_s[...] = _mask_matmul((ci2 <= ri2).astype(BF16), logf)

    def head_norm(o_h):
        ms = jnp.mean(o_h * o_h, axis=-1, keepdims=True)
        return o_h * lax.rsqrt(ms + EPS)

    def finish(o_heads):
        o_n = jnp.concatenate([head_norm(o_h) for o_h in o_heads], axis=1) * prow(P_GNORM)
        yb = (o_n * sog_s[...]).astype(BF16)
        mixed = ga_s[...] + sgb_s[...] * _mm(yb, _unpack_rows(wb_ref[...]))
        z = _mm(mixed.astype(BF16), _unpack_rows(wout_ref[...]))
        x1 = x_ref[...].reshape(rows, D) + _rms(z, prow(P_GAIN + 1))
        x1_ref[...] = x1.reshape(seqs, tokens, D)

    def back_prompt():
        xnb = xn_s[...]
        qf = qf_s[...]
        kk = kk_s[...]
        cum = cum_s[...]
        vb = v_s[...].astype(BF16)
        n_sub = rows // sub
        last = cum[rows - 1:rows, :]
        qib = (qf * jnp.exp(cum)).astype(BF16)
        ksb = (kk * jnp.exp(last - cum)).astype(BF16)
        decay_row = jnp.exp(last)
        def score_blocks(hd):
            hs = head_cols(hd)
            parts = []
            for i in range(n_sub):
                r0 = i * sub
                r1 = r0 + sub
                width = min(rows, -(-r1 // HEAD) * HEAD)
                ref_row = cum[r0 - 1:r0, hs] if i > 0 else jnp.zeros((1, HEAD), F32)
                qd = (qf[r0:r1, hs] * jnp.exp(cum[r0:r1, hs] - ref_row)).astype(BF16)
                arg = ref_row - cum[0:width, hs]
                if width > r1:
                    arg = jnp.where(lax.broadcasted_iota(jnp.int32, (width, 1), 0) < r1, arg, 0.0)
                kdi = (kk[0:width, hs] * jnp.exp(arg)).astype(BF16)
                att = _mm_nt(qd, kdi)
                rr = lax.broadcasted_iota(jnp.int32, (sub, width), 0) + r0
                cc = lax.broadcasted_iota(jnp.int32, (sub, width), 1)
                att = jnp.where(cc <= rr, att, 0.0).astype(BF16)
                if width < rows:
                    att = jnp.concatenate([att, jnp.zeros((sub, rows - width), BF16)], axis=1)
                parts.append(att)
            return jnp.concatenate(parts, axis=0) if n_sub > 1 else parts[0]

        def state_and_gate_quarter(hd, s_old):
            hs = head_cols(hd)
            upd = _mm_tn(ksb[:, hs], vb[:, hs])
            sout_ref[0, hd] = _column_of(decay_row[:, hs]) * s_old + upd
            sold_s[hd] = s_old
            quarter = D // 4
            grp, part = (C_MB, hd) if hd < 4 else (C_OG, hd - 4)
            cols = slice(part * quarter, (part + 1) * quarter)
            proj = _mm(xnb, _unpack_rows(win_ref[:, grp * D + part * quarter:grp * D + (part + 1) * quarter]))
            if hd < 4:
                sgb_s[:, cols] = _sigmoid_of_twice(proj)
            else:
                sog_s[:, cols] = _silu_of_twice(proj)

        att_next = score_blocks(0)
        for hd in range(N_HEADS):
            hs = head_cols(hd)
            att_full = att_next
            s_old = sout_ref[0, hd]
            state_and_gate_quarter(hd, s_old)
            if hd + 1 < N_HEADS:
                att_next = score_blocks(hd + 1)
            if rows % HEAD == 0:
                o_h = _mm(jnp.concatenate([att_full, qib[:, hs]], axis=1),
                          jnp.concatenate([vb[:, hs], s_old.astype(BF16)], axis=0))
            else:
                o_h = _mm(att_full, vb[:, hs]) + _mm(qib[:, hs], s_old.astype(BF16))
            o_s[:, hs] = o_h

        worst = jnp.zeros((1, D), F32)
        for i in range(n_sub):
            start = cum[i * sub - 1:i * sub, :] if i > 0 else jnp.zeros((1, D), F32)
            worst = jnp.maximum(worst, start - cum[(i + 1) * sub - 1:(i + 1) * sub, :])
        out_of_range = jnp.max(worst) > SAFE_DECAY_RANGE

        @pl.when(out_of_range)
        def _exact_scores():
            row_i = lax.broadcasted_iota(jnp.int32, (rows, 1), 0)
            for hd in range(N_HEADS):
                hs = head_cols(hd)
                qib_h = (qf_s[:, hs] * jnp.exp(cum_s[:, hs])).astype(BF16)
                o_s[:, hs] = _mm(qib_h, sold_s[hd].astype(BF16))

                def block_body(b, carry, hs=hs):
                    r0 = pl.multiple_of(b * SUBLANES, SUBLANES)
                    rs = pl.ds(r0, SUBLANES)
                    q_b, k_b, c_b, v_b = qf_s[rs, hs], kk_s[rs, hs], cum_s[rs, hs], v_s[rs, hs]
                    prev_start = pl.multiple_of(jnp.maximum(r0 - SUBLANES, 0), SUBLANES)
                    before = cum_s[pl.ds(prev_start, SUBLANES), hs][SUBLANES - 1:SUBLANES, :]
                    ref_row = jnp.where(b > 0, before, 0.0)
                    qd = (q_b * jnp.exp(c_b - ref_row)).astype(BF16)
                    arg = jnp.minimum(ref_row - cum_s[:, hs], 0.0)
                    kd = jnp.where(row_i < r0, kk_s[:, hs] * jnp.exp(arg), 0.0).astype(BF16)
                    earlier = _mm(_mm_nt(qd, kd).astype(BF16), v_s[:, hs].astype(BF16))
                    o_s[rs, hs] = o_s[rs, hs] + earlier + _exact_block(q_b, k_b, c_b, v_b)
                    return carry

                lax.fori_loop(0, rows // SUBLANES, block_body, 0)

        finish([o_s[:, head_cols(hd)] for hd in range(N_HEADS)])

    def decode_state_step():
        def seq_body(g, carry):
            rsl = pl.ds(pl.multiple_of(g * SUBLANES, SUBLANES), SUBLANES)
            s_old = s0_ref[g, 0]
            c_b = cum_h[j, rsl, :]
            total_b = tot_h[j, rsl, :]
            qi = qf_h[j, rsl, :] * jnp.exp(c_b)
            ks = kk_h[j, rsl, :] * jnp.exp(total_b - c_b)
            oin_h[j, rsl, :] = _mm(qi.astype(BF16), s_old.astype(BF16))
            upd = _mm_tn(ks.astype(BF16), v_h[j, rsl, :].astype(BF16))
            decay = _column_of(jnp.exp(total_b[0:1, :]))
            sout_ref[g, 0] = decay * s_old + upd
            return carry

        lax.fori_loop(0, seqs, seq_body, 0, unroll=SUBLANES)

    def back_decode():
        ri = lax.broadcasted_iota(jnp.int32, (rows, rows), 0)
        ci = lax.broadcasted_iota(jnp.int32, (rows, rows), 1)
        causal = jnp.logical_and(ci <= ri, (ri // tokens) == (ci // tokens))
        for hd in range(N_HEADS):
            c_h = cum_h[hd]
            scores = _mm_nt((qf_h[hd] * jnp.exp(c_h)).astype(BF16), (kk_h[hd] * jnp.exp(-c_h)).astype(BF16))
            att = jnp.where(causal, scores, 0.0).astype(BF16)
            oi_h[hd] = _mm(att, v_h[hd].astype(BF16))

        worst = jnp.zeros((1, HEAD), F32)
        for hd in range(N_HEADS):
            worst = jnp.maximum(worst, jnp.max(-tot_h[hd], axis=0, keepdims=True))
        out_of_range = jnp.max(worst) > SAFE_DECAY_RANGE

        @pl.when(out_of_range)
        def _exact_scores():
            for hd in range(N_HEADS):
                def seq_body(g, carry, hd=hd):
                    rs = pl.ds(pl.multiple_of(g * SUBLANES, SUBLANES), SUBLANES)
                    oi_h[hd, rs, :] = _exact_block(qf_h[hd, rs, :], kk_h[hd, rs, :], cum_h[hd, rs, :], v_h[hd, rs, :])
                    return carry

                lax.fori_loop(0, seqs, seq_body, 0)

        finish([oin_h[hd] + oi_h[hd] for hd in range(N_HEADS)])

    if decode:
        @pl.when(j == 0)
        def _front_region():
            ubuf[:, 0:SUBLANES, :] = cext_ref[...]
            front()

        decode_state_step()

        @pl.when(j == N_HEADS - 1)
        def _back_region():
            back_decode()
    else:
        @pl.when(j == 0)
        def _init():
            ubuf[:, 0:SUBLANES, :] = cext_ref[...]
            hout_ref[...] = h0_ref[...]
            sout_ref[...] = s0_ref[...]

        front()

        back_prompt()


def _resident(shape):
    return pl.BlockSpec(shape, lambda i, j: (0,) * len(shape), pipeline_mode=pl.Buffered(1))


def _mixer(x, cext, h0, s0, pvec, w_in, w_ri, w_a, w_b, w_out, *, seqs, tokens, reset_first, sub, name):
    n_seq, length, _ = x.shape
    n_tiles = length // tokens
    decode = seqs > 1
    assert n_seq % seqs == 0 and length % tokens == 0 and tokens % SUBLANES == 0
    assert not decode or (n_tiles == 1 and tokens == SUBLANES)
    assert (seqs * tokens) % sub == 0
    rows = seqs * tokens
    kern = functools.partial(_mixer_kernel, seqs=seqs, tokens=tokens, n_tiles=n_tiles,
                             reset_first=reset_first, sub=sub)
    tile_f32 = pltpu.VMEM((rows, D), F32)
    by_head_f32 = pltpu.VMEM((N_HEADS, rows, HEAD), F32)
    common = [pltpu.VMEM((seqs, SUBLANES + tokens, D), F32),
              tile_f32, tile_f32, tile_f32]
    if decode:
        grid = (n_seq // seqs, N_HEADS)
        x_map = lambda i, j: (i, 0, 0)
        s_block = (seqs, 1, HEAD, HEAD)
        s_map = lambda i, j: (i, j, 0, 0)
        scratch = common + [by_head_f32] * 7
    else:
        grid = (n_seq, n_tiles)
        x_map = lambda i, j: (i, j, 0)
        s_block = (1, N_HEADS, HEAD, HEAD)
        s_map = lambda i, j: (i, 0, 0, 0)
        scratch = common + [tile_f32] * 6 + [pltpu.VMEM((N_HEADS, HEAD, HEAD), F32), pltpu.VMEM((rows, D), BF16)]
    seq_map = lambda i, j: (i, 0, 0)
    if cext.shape[0] == n_seq:
        in_seq_map, in_s_map = seq_map, s_map
    else:
        assert cext.shape[0] == h0.shape[0] == s0.shape[0] == seqs == 1
        in_seq_map = lambda i, j: (0, 0, 0)
        in_s_map = lambda i, j: (0, 0, 0, 0)
    return pl.pallas_call(
        kern,
        grid=grid,
        in_specs=[
            pl.BlockSpec((seqs, tokens, D), x_map),
            pl.BlockSpec((seqs, SUBLANES, D), in_seq_map),
            pl.BlockSpec((seqs, 1, D), in_seq_map),
            pl.BlockSpec(s_block, in_s_map),
            _resident((P_ROWS, D)),
            _resident(w_in.shape),
            _resident(w_ri.shape),
            _resident(w_a.shape),
            _resident(w_b.shape),
            _resident(w_out.shape),
        ],
        out_specs=[
            pl.BlockSpec((seqs, tokens, D), x_map),
            pl.BlockSpec((seqs, SUBLANES, D), seq_map),
            pl.BlockSpec((seqs, 1, D), seq_map),
            pl.BlockSpec(s_block, s_map),
        ],
        out_shape=[
            jax.ShapeDtypeStruct((n_seq, length, D), F32),
            jax.ShapeDtypeStruct((n_seq, SUBLANES, D), F32),
            jax.ShapeDtypeStruct((n_seq, 1, D), F32),
            jax.ShapeDtypeStruct((n_seq, N_HEADS, HEAD, HEAD), F32),
        ],
        scratch_shapes=scratch,
        compiler_params=pltpu.CompilerParams(
            dimension_semantics=("arbitrary", "arbitrary"),
            vmem_limit_bytes=VMEM_LIMIT_BYTES),
        name=name,
    )(x, cext, h0, s0, pvec, w_in, w_ri, w_a, w_b, w_out)


def _mlp_kernel(xa_ref, xb_ref, g_ref, wup_ref, wdn_ref, oa_ref, ob_ref, *, steps_a):
    def tile_mlp(x_ref, o_ref):
        x = x_ref[...]
        hn = _rms(x, g_ref[0:1, :]).astype(BF16)
        acc = jnp.zeros(x.shape, F32)
        for c in range(D_FF // D):
            t = _mm(hn, _unpack_rows(wup_ref[:, c * D:(c + 1) * D]))
            t = jnp.square(jnp.maximum(t, 0.0)).astype(BF16)
            acc = acc + _mm(t, _unpack_rows(wdn_ref[c * (D // 2):(c + 1) * (D // 2), :]))
        o_ref[...] = x + _rms(acc, g_ref[1:2, :])

    i = pl.program_id(0)

    @pl.when(i < steps_a)
    def _first():
        tile_mlp(xa_ref, oa_ref)

    @pl.when(i >= steps_a)
    def _second():
        tile_mlp(xb_ref, ob_ref)


def _mlp(xa, xb, gains, w_up, w_down, *, tile, name):
    na, nb = xa.shape[0], xb.shape[0]
    assert na % tile == 0 and nb % tile == 0
    steps_a, steps_b = na // tile, nb // tile
    a_map = lambda i: (jnp.minimum(i, steps_a - 1), 0)
    b_map = lambda i: (jnp.maximum(i - steps_a, 0), 0)
    return pl.pallas_call(
        functools.partial(_mlp_kernel, steps_a=steps_a),
        grid=(steps_a + steps_b,),
        in_specs=[
            pl.BlockSpec((tile, D), a_map),
            pl.BlockSpec((tile, D), b_map),
            pl.BlockSpec((2, D), lambda i: (0, 0), pipeline_mode=pl.Buffered(1)),
            pl.BlockSpec(w_up.shape, lambda i: (0, 0), pipeline_mode=pl.Buffered(1)),
            pl.BlockSpec(w_down.shape, lambda i: (0, 0), pipeline_mode=pl.Buffered(1)),
        ],
        out_specs=[pl.BlockSpec((tile, D), a_map), pl.BlockSpec((tile, D), b_map)],
        out_shape=[jax.ShapeDtypeStruct((na, D), F32), jax.ShapeDtypeStruct((nb, D), F32)],
        compiler_params=pltpu.CompilerParams(
            dimension_semantics=("arbitrary",),
            vmem_limit_bytes=VMEM_LIMIT_BYTES),
        name=name,
    )(xa, xb, gains, w_up, w_down)


def kernel(x_prompt, x_sample, state_conv, state_rglru, state_hgrn, meta_tokens, norm_gains, w_in,
           conv_w, conv_b, rg_w, rg_b, ig_w, ig_b, lru_lambda, hgrn_lb, hgrn_gnorm,
           w_branch_a, w_branch_b, w_out, w_up, w_down):
    bp, seq, _ = x_prompt.shape
    bs, dec_seq, _ = x_sample.shape
    layer = 0
    pvec = jnp.concatenate([
        norm_gains[layer], conv_w[layer], conv_b[layer][None], 0.5 * rg_b[layer][None], 0.5 * ig_b[layer][None],
        lru_lambda[layer][None], hgrn_lb[layer:layer + 2], hgrn_gnorm[layer][None],
        jnp.zeros((1, D), F32)], axis=0).astype(F32)
    halved = (C_GATE, C_Q, C_OG, C_MA, C_MB)
    scale_in = jnp.concatenate([jnp.full((1, D), 0.5 if g in halved else 1.0, F32) for g in range(8)], axis=1)

    gate_blocks = [w.reshape(N_HEADS * HEAD, HEAD) for w in (rg_w[layer], ig_w[layer])]
    w_in_b, w_ri, w_a, w_b, w_o, w_u, w_d = _pack_weights([
        ([w_in[layer]], scale_in),
        (gate_blocks, 0.5),
        ([w_branch_a[layer]], None),
        ([w_branch_b[layer]], None),
        ([w_out[layer]], None),
        ([w_up[layer]], None),
        ([w_down[layer]], None),
    ], name="pack_weights")
    w_ri = w_ri.reshape(N_HEADS, HEAD // 2, 2 * HEAD)
    weights = (pvec, w_in_b, w_ri, w_a, w_b, w_o)

    zeros_c = jnp.zeros((1, SUBLANES, D), F32)
    zeros_h = jnp.zeros((1, 1, D), F32)
    zeros_s = jnp.zeros((1, N_HEADS, HEAD, HEAD), F32)
    _, c_m, h_m, s_m = _mixer(meta_tokens[None].astype(F32), zeros_c, zeros_h, zeros_s, *weights,
                              seqs=1, tokens=N_META, reset_first=True, sub=N_META, name="mixer_meta")

    x1_p, c_p, h_p, s_p = _mixer(
        x_prompt,
        c_m, h_m, s_m,
        *weights, seqs=1, tokens=256, reset_first=False, sub=64, name="mixer_prompt")

    cext_s = jnp.pad(state_conv[layer], ((0, 0), (SUBLANES - 3, 0), (0, 0)))
    x1_s, c_s, h_s, s_s = _mixer(
        x_sample, cext_s, state_rglru[layer][:, None, :], state_hgrn[layer],
        *weights, seqs=32, tokens=dec_seq, reset_first=False, sub=8, name="mixer_sample")

    gains_mlp = norm_gains[layer, 2:4]
    y_p, y_s = _mlp(x1_p.reshape(bp * seq, D), x1_s.reshape(bs * dec_seq, D), gains_mlp, w_u, w_d,
                    tile=1024, name="mlp")

    return (y_p.reshape(bp, seq, D), y_s.reshape(bs, dec_seq, D),
            c_p[:, SUBLANES - 3:, :][None], h_p[:, 0, :][None], s_p[None],
            c_s[:, SUBLANES - 3:, :][None], h_s[:, 0, :][None], s_s[None])
```

```python
import functools

import jax
import jax.numpy as jnp
from jax import lax
from jax.experimental import pallas as pl
from jax.experimental.pallas import tpu as pltpu

D = 1024
N_HEADS = 8
HEAD = 128
N_META = 16
LRU_C = 8.0
EPS = 1e-6
D_FF = 4096
SUBLANES = 8
F32 = jnp.float32
BF16 = jnp.bfloat16

P_GAIN, P_CONV_W, P_CONV_B, P_RG_B, P_IG_B, P_LAMBDA, P_LB, P_GNORM, P_ROWS = 0, 4, 8, 9, 10, 11, 12, 14, 16

C_U, C_GATE, C_Q, C_F, C_I, C_OG, C_MA, C_MB = range(8)

VMEM_LIMIT_BYTES = 60 * 1024 * 1024
PACK_STEPS = 8
PACK_VMEM_LIMIT_BYTES = 40 * 1024 * 1024
SAFE_DECAY_RANGE = 64.0


def _rms(x, g):
    ms = jnp.mean(x * x, axis=-1, keepdims=True)
    return x * lax.rsqrt(ms + EPS) * g


def _sigmoid_of_twice(h):
    return 0.5 * jnp.tanh(h) + 0.5


def _silu_of_twice(h):
    return h + h * jnp.tanh(h)


_GELU_C1 = 2.0 * 0.7978845608028654
_GELU_C2 = 8.0 * 0.7978845608028654 * 0.044715


def _gelu_tanh_of_twice(h):
    return h + h * jnp.tanh(h * (_GELU_C1 + _GELU_C2 * (h * h)))


def _mm(a, b):
    return jnp.dot(a, b, preferred_element_type=F32)


def _mm_nt(a, b):
    return lax.dot_general(a, b, (((1,), (1,)), ((), ())), preferred_element_type=F32)


def _mm_tn(a, b):
    return lax.dot_general(a, b, (((0,), (0,)), ((), ())), preferred_element_type=F32)


def _mask_matmul(mask_bf, x):
    hi = x.astype(BF16)
    lo = (x - hi.astype(F32)).astype(BF16)
    return _mm(mask_bf, hi) + _mm(mask_bf, lo)


def _pack_kernel(*refs, groups):
    out_refs = refs[len(refs) - len(groups):]
    pos = 0
    for (n_parts, scale_kind, scalar), o_ref in zip(groups, out_refs):
        parts = [refs[pos + p][...] for p in range(n_parts)]
        pos += n_parts
        w = parts[0] if n_parts == 1 else jnp.concatenate(parts, axis=1)
        if scale_kind == "row":
            w = w * refs[pos][...]
            pos += 1
        elif scale_kind == "scalar":
            w = w * scalar
        o_ref[...] = pltpu.bitcast(w.astype(BF16), jnp.int32)


def _pack_weights(groups, *, name):
    operands, in_specs, out_specs, out_shapes, kinds = [], [], [], [], []
    for mats, scale in groups:
        k = mats[0].shape[0]
        tile = k // PACK_STEPS
        assert all(m.shape[0] == k for m in mats) and k % PACK_STEPS == 0 and tile % (2 * SUBLANES) == 0
        n = sum(m.shape[1] for m in mats)
        for m in mats:
            operands.append(m)
            in_specs.append(pl.BlockSpec((tile, m.shape[1]), lambda i: (i, 0)))
        if scale is None:
            kinds.append((len(mats), "none", None))
        elif isinstance(scale, float):
            kinds.append((len(mats), "scalar", scale))
        else:
            kinds.append((len(mats), "row", None))
            operands.append(scale)
            in_specs.append(pl.BlockSpec((1, n), lambda i: (0, 0)))
        out_specs.append(pl.BlockSpec((tile // 2, n), lambda i: (i, 0)))
        out_shapes.append(jax.ShapeDtypeStruct((k // 2, n), jnp.int32))
    return pl.pallas_call(
        functools.partial(_pack_kernel, groups=tuple(kinds)),
        grid=(PACK_STEPS,),
        in_specs=in_specs,
        out_specs=out_specs,
        out_shape=out_shapes,
        compiler_params=pltpu.CompilerParams(
            dimension_semantics=("arbitrary",), vmem_limit_bytes=PACK_VMEM_LIMIT_BYTES),
        name=name,
    )(*operands)


def _unpack_rows(words):
    return pltpu.bitcast(words, BF16)


def _exact_block(q_b, k_b, c_b, v_b):
    t_idx = lax.broadcasted_iota(jnp.int32, (SUBLANES, 1), 0)
    acc = jnp.zeros((SUBLANES, HEAD), F32)
    for s in range(SUBLANES):
        decay = jnp.exp(jnp.minimum(c_b - c_b[s:s + 1, :], 0.0))
        score = jnp.sum(q_b * k_b[s:s + 1, :] * decay, axis=-1, keepdims=True)
        acc = acc + jnp.where(t_idx >= s, score, 0.0) * v_b[s:s + 1, :]
    return acc


def _column_of(row):
    return jnp.transpose(jnp.broadcast_to(row, (HEAD, HEAD)))


def _mixer_kernel(x_ref, cext_ref, h0_ref, s0_ref, pvec_ref, win_ref, wri_ref, wa_ref, wb_ref, wout_ref,
                  x1_ref, cout_ref, hout_ref, sout_ref, *scratch,
                  seqs, tokens, n_tiles, reset_first, sub):
    j = pl.program_id(1)
    rows = seqs * tokens
    decode = seqs > 1
    if decode:
        ubuf, ga_s, sgb_s, sog_s, qf_h, kk_h, cum_h, v_h, tot_h, oin_h, oi_h = scratch
    else:
        ubuf, ga_s, sgb_s, sog_s, hbuf, qf_s, kk_s, cum_s, v_s, o_s, sold_s, xn_s = scratch

    pv = pvec_ref[...]

    def prow(r):
        return pv[r:r + 1, :]

    def wcol(g):
        return _unpack_rows(win_ref[:, g * D:(g + 1) * D])

    def head_cols(hd):
        return slice(hd * HEAD, (hd + 1) * HEAD)

    def front():
        x = x_ref[...].reshape(rows, D)
        xn = _rms(x, prow(P_GAIN + 0)).astype(BF16)
        row_id = lax.broadcasted_iota(jnp.int32, (rows, 1), 0)

        u = _mm(xn, wcol(C_U))
        ubuf[:, SUBLANES:SUBLANES + tokens, :] = u.reshape(seqs, tokens, D)
        if decode:
            merge_gates(xn)
        else:
            xn_s[...] = xn
        uc = prow(P_CONV_B) + prow(P_CONV_W + 3) * u
        for k in range(1, 4):
            shifted = ubuf[:, SUBLANES - k:SUBLANES - k + tokens, :].reshape(rows, D)
            uc = uc + prow(P_CONV_W + 3 - k) * shifted
        tail = ubuf[:, tokens:tokens + SUBLANES, :]
        ubuf[:, 0:SUBLANES, :] = tail
        cout_ref[...] = tail

        ucb = uc.astype(BF16)
        r_parts, i_parts = [], []
        for b in range(N_HEADS):
            ri = _mm(ucb[:, head_cols(b)], _unpack_rows(wri_ref[b]))
            r_parts.append(ri[:, :HEAD])
            i_parts.append(ri[:, HEAD:])
        r_pre = jnp.concatenate(r_parts, axis=1)
        i_pre = jnp.concatenate(i_parts, axis=1)
        if not decode:
            hbuf[...] = r_pre
            o_s[...] = i_pre
        hgrn_operands(xn)
        if not decode:
            r_pre = hbuf[...]
            i_pre = o_s[...]
        r_gate = _sigmoid_of_twice(r_pre + prow(P_RG_B))
        i_gate = _sigmoid_of_twice(i_pre + prow(P_IG_B))
        lam = prow(P_LAMBDA)
        softplus_neg_lam = jnp.maximum(-lam, 0.0) + jnp.log1p(jnp.exp(-jnp.abs(lam)))
        log_a = (-LRU_C) * r_gate * softplus_neg_lam
        a_cum = jnp.exp(log_a)
        th = jnp.tanh(log_a)
        sq = -2.0 * th / (1.0 - th)
        mult = jnp.where(sq > 0.0, sq * lax.rsqrt(sq), 0.0)
        if reset_first:
            mult = jnp.where(jnp.logical_and(row_id == 0, j == 0), 1.0, mult)
        b_cum = mult * i_gate * uc

        a_cum = a_cum.reshape(rows // SUBLANES, SUBLANES, D)
        b_cum = b_cum.reshape(rows // SUBLANES, SUBLANES, D)
        sublane = lax.broadcasted_iota(jnp.int32, (1, SUBLANES, 1), 1)
        for s in (1, 2, 4):
            keep = sublane >= s
            a_prev = jnp.where(keep, pltpu.roll(a_cum, s, 1), 1.0)
            b_prev = jnp.where(keep, pltpu.roll(b_cum, s, 1), 0.0)
            b_cum = a_cum * b_prev + b_cum
            a_cum = a_cum * a_prev
        a_cum = a_cum.reshape(rows, D)
        b_cum = b_cum.reshape(rows, D)

        if decode:
            h_in = jnp.broadcast_to(h0_ref[...], (seqs, SUBLANES, D)).reshape(rows, D)
            h_all = a_cum * h_in + b_cum
            hout_ref[...] = h_all.reshape(seqs, SUBLANES, D)[:, SUBLANES - 1:SUBLANES, :]
        else:
            h = hout_ref[0]
            for g in range(rows // SUBLANES):
                sl = slice(g * SUBLANES, (g + 1) * SUBLANES)
                hg = a_cum[sl] * h + b_cum[sl]
                hbuf[sl, :] = hg
                h = hg[SUBLANES - 1:SUBLANES, :]
            hout_ref[0] = h
            h_all = hbuf[...]

        ya = (h_all * _gelu_tanh_of_twice(_mm(xn, wcol(C_GATE)))).astype(BF16)
        ga_s[...] = _sigmoid_of_twice(_mm(xn, wcol(C_MA))) * _mm(ya, _unpack_rows(wa_ref[...]))

    def merge_gates(xn):
        sgb_s[...] = _sigmoid_of_twice(_mm(xn, wcol(C_MB)))
        sog_s[...] = _silu_of_twice(_mm(xn, wcol(C_OG)))

    def hgrn_operands(xn):
        qf = _silu_of_twice(_mm(xn, wcol(C_Q)))
        lb_raw = pv[P_LB:P_LB + 2, :]
        lb_exp = jnp.exp(lb_raw - jnp.max(lb_raw, axis=0, keepdims=True))
        lb = lb_exp[0:1, :] / jnp.sum(lb_exp, axis=0, keepdims=True)
        fg = lb + (1.0 - lb) / (1.0 + jnp.exp(-_mm(xn, wcol(C_F))))
        logf = jnp.log(fg)
        kk = 1.0 - fg
        v = _mm(xn, wcol(C_I))

        ri2 = lax.broadcasted_iota(jnp.int32, (rows, rows), 0)
        ci2 = lax.broadcasted_iota(jnp.int32, (rows, rows), 1)
        if decode:
            same_seq = (ri2 // tokens) == (ci2 // tokens)
            total = _mask_matmul(same_seq.astype(BF16), logf)
            cum = _mask_matmul(jnp.logical_and(ci2 <= ri2, same_seq).astype(BF16), logf)
            for hd in range(N_HEADS):
                hs = head_cols(hd)
                v_h[hd] = v[:, hs]
                tot_h[hd] = total[:, hs]
                qf_h[hd] = qf[:, hs]
                kk_h[hd] = kk[:, hs]
                cum_h[hd] = cum[:, hs]
        else:
            qf_s[...] = qf
            kk_s[...] = kk
            v_s[...] = v
            cum_s[...] = _mask_matmul((ci2 <= ri2).astype(BF16), logf)

    def head_norm(o_h):
        ms = jnp.mean(o_h * o_h, axis=-1, keepdims=True)
        return o_h * lax.rsqrt(ms + EPS)

    def finish(o_heads):
        o_n = jnp.concatenate([head_norm(o_h) for o_h in o_heads], axis=1) * prow(P_GNORM)
        yb = (o_n * sog_s[...]).astype(BF16)
        mixed = ga_s[...] + sgb_s[...] * _mm(yb, _unpack_rows(wb_ref[...]))
        z = _mm(mixed.astype(BF16), _unpack_rows(wout_ref[...]))
        x1 = x_ref[...].reshape(rows, D) + _rms(z, prow(P_GAIN + 1))
        x1_ref[...] = x1.reshape(seqs, tokens, D)

    def back_prompt():
        xnb = xn_s[...]
        qf = qf_s[...]
        kk = kk_s[...]
        cum = cum_s[...]
        vb = v_s[...].astype(BF16)
        n_sub = rows // sub
        last = cum[rows - 1:rows, :]
        qib = (qf * jnp.exp(cum)).astype(BF16)
        ksb = (kk * jnp.exp(last - cum)).astype(BF16)
        decay_row = jnp.exp(last)
        def score_blocks(hd):
            hs = head_cols(hd)
            parts = []
            for i in range(n_sub):
                r0 = i * sub
                r1 = r0 + sub
                width = min(rows, -(-r1 // HEAD) * HEAD)
                ref_row = cum[r0 - 1:r0, hs] if i > 0 else jnp.zeros((1, HEAD), F32)
                qd = (qf[r0:r1, hs] * jnp.exp(cum[r0:r1, hs] - ref_row)).astype(BF16)
                arg = ref_row - cum[0:width, hs]
                if width > r1:
                    arg = jnp.where(lax.broadcasted_iota(jnp.int32, (width, 1), 0) < r1, arg, 0.0)
                kdi = (kk[0:width, hs] * jnp.exp(arg)).astype(BF16)
                att = _mm_nt(qd, kdi)
                rr = lax.broadcasted_iota(jnp.int32, (sub, width), 0) + r0
                cc = lax.broadcasted_iota(jnp.int32, (sub, width), 1)
                att = jnp.where(cc <= rr, att, 0.0).astype(BF16)
                if width < rows:
                    att = jnp.concatenate([att, jnp.zeros((sub, rows - width), BF16)], axis=1)
                parts.append(att)
            return jnp.concatenate(parts, axis=0) if n_sub > 1 else parts[0]

        def state_and_gate_quarter(hd, s_old):
            hs = head_cols(hd)
            upd = _mm_tn(ksb[:, hs], vb[:, hs])
            sout_ref[0, hd] = _column_of(decay_row[:, hs]) * s_old + upd
            sold_s[hd] = s_old
            quarter = D // 4
            grp, part = (C_MB, hd) if hd < 4 else (C_OG, hd - 4)
            cols = slice(part * quarter, (part + 1) * quarter)
            proj = _mm(xnb, _unpack_rows(win_ref[:, grp * D + part * quarter:grp * D + (part + 1) * quarter]))
            if hd < 4:
                sgb_s[:, cols] = _sigmoid_of_twice(proj)
            else:
                sog_s[:, cols] = _silu_of_twice(proj)

        att_next = score_blocks(0)
        o_heads = []
        for hd in range(N_HEADS):
            hs = head_cols(hd)
            att_full = att_next
            s_old = sout_ref[0, hd]
            state_and_gate_quarter(hd, s_old)
            if hd + 1 < N_HEADS:
                att_next = score_blocks(hd + 1)
            if rows % HEAD == 0:
                o_h = _mm(jnp.concatenate([att_full, qib[:, hs]], axis=1),
                          jnp.concatenate([vb[:, hs], s_old.astype(BF16)], axis=0))
            else:
                o_h = _mm(att_full, vb[:, hs]) + _mm(qib[:, hs], s_old.astype(BF16))
            o_heads.append(o_h)

        finish(o_heads)

        worst = jnp.zeros((1, D), F32)
        for i in range(n_sub):
            start = cum[i * sub - 1:i * sub, :] if i > 0 else jnp.zeros((1, D), F32)
            worst = jnp.maximum(worst, start - cum[(i + 1) * sub - 1:(i + 1) * sub, :])
        out_of_range = jnp.max(worst) > SAFE_DECAY_RANGE

        @pl.when(out_of_range)
        def _exact_scores():
            row_i = lax.broadcasted_iota(jnp.int32, (rows, 1), 0)
            for hd in range(N_HEADS):
                hs = head_cols(hd)
                qib_h = (qf_s[:, hs] * jnp.exp(cum_s[:, hs])).astype(BF16)
                o_s[:, hs] = _mm(qib_h, sold_s[hd].astype(BF16))

                def block_body(b, carry, hs=hs):
                    r0 = pl.multiple_of(b * SUBLANES, SUBLANES)
                    rs = pl.ds(r0, SUBLANES)
                    q_b, k_b, c_b, v_b = qf_s[rs, hs], kk_s[rs, hs], cum_s[rs, hs], v_s[rs, hs]
                    prev_start = pl.multiple_of(jnp.maximum(r0 - SUBLANES, 0), SUBLANES)
                    before = cum_s[pl.ds(prev_start, SUBLANES), hs][SUBLANES - 1:SUBLANES, :]
                    ref_row = jnp.where(b > 0, before, 0.0)
                    qd = (q_b * jnp.exp(c_b - ref_row)).astype(BF16)
                    arg = jnp.minimum(ref_row - cum_s[:, hs], 0.0)
                    kd = jnp.where(row_i < r0, kk_s[:, hs] * jnp.exp(arg), 0.0).astype(BF16)
                    earlier = _mm(_mm_nt(qd, kd).astype(BF16), v_s[:, hs].astype(BF16))
                    o_s[rs, hs] = o_s[rs, hs] + earlier + _exact_block(q_b, k_b, c_b, v_b)
                    return carry

                lax.fori_loop(0, rows // SUBLANES, block_body, 0)

            finish([o_s[:, head_cols(hd)] for hd in range(N_HEADS)])

    def decode_state_step():
        def seq_body(g, carry):
            rsl = pl.ds(pl.multiple_of(g * SUBLANES, SUBLANES), SUBLANES)
            s_old = s0_ref[g, 0]
            c_b = cum_h[j, rsl, :]
            total_b = tot_h[j, rsl, :]
            qi = qf_h[j, rsl, :] * jnp.exp(c_b)
            ks = kk_h[j, rsl, :] * jnp.exp(total_b - c_b)
            oin_h[j, rsl, :] = _mm(qi.astype(BF16), s_old.astype(BF16))
            upd = _mm_tn(ks.astype(BF16), v_h[j, rsl, :].astype(BF16))
            decay = _column_of(jnp.exp(total_b[0:1, :]))
            sout_ref[g, 0] = decay * s_old + upd
            return carry

        lax.fori_loop(0, seqs, seq_body, 0, unroll=SUBLANES)

    def back_decode():
        ri = lax.broadcasted_iota(jnp.int32, (rows, rows), 0)
        ci = lax.broadcasted_iota(jnp.int32, (rows, rows), 1)
        causal = jnp.logical_and(ci <= ri, (ri // tokens) == (ci // tokens))
        o_heads = []
        for hd in range(N_HEADS):
            c_h = cum_h[hd]
            scores = _mm_nt((qf_h[hd] * jnp.exp(c_h)).astype(BF16), (kk_h[hd] * jnp.exp(-c_h)).astype(BF16))
            att = jnp.where(causal, scores, 0.0).astype(BF16)
            o_heads.append(oin_h[hd] + _mm(att, v_h[hd].astype(BF16)))
        finish(o_heads)

        worst = jnp.zeros((1, HEAD), F32)
        for hd in range(N_HEADS):
            worst = jnp.maximum(worst, jnp.max(-tot_h[hd], axis=0, keepdims=True))
        out_of_range = jnp.max(worst) > SAFE_DECAY_RANGE

        @pl.when(out_of_range)
        def _exact_scores():
            for hd in range(N_HEADS):
                def seq_body(g, carry, hd=hd):
                    rs = pl.ds(pl.multiple_of(g * SUBLANES, SUBLANES), SUBLANES)
                    oi_h[hd, rs, :] = _exact_block(qf_h[hd, rs, :], kk_h[hd, rs, :], cum_h[hd, rs, :], v_h[hd, rs, :])
                    return carry

                lax.fori_loop(0, seqs, seq_body, 0)

            finish([oin_h[hd] + oi_h[hd] for hd in range(N_HEADS)])

    if decode:
        @pl.when(j == 0)
        def _front_region():
            ubuf[:, 0:SUBLANES, :] = cext_ref[...]
            front()

        decode_state_step()

        @pl.when(j == N_HEADS - 1)
        def _back_region():
            back_decode()
    else:
        @pl.when(j == 0)
        def _init():
            ubuf[:, 0:SUBLANES, :] = cext_ref[...]
            hout_ref[...] = h0_ref[...]
            sout_ref[...] = s0_ref[...]

        front()

        back_prompt()


def _resident(shape):
    return pl.BlockSpec(shape, lambda i, j: (0,) * len(shape), pipeline_mode=pl.Buffered(1))


def _mixer(x, cext, h0, s0, pvec, w_in, w_ri, w_a, w_b, w_out, *, seqs, tokens, reset_first, sub, name):
    n_seq, length, _ = x.shape
    n_tiles = length // tokens
    decode = seqs > 1
    assert n_seq % seqs == 0 and length % tokens == 0 and tokens % SUBLANES == 0
    assert not decode or (n_tiles == 1 and tokens == SUBLANES)
    assert (seqs * tokens) % sub == 0
    rows = seqs * tokens
    kern = functools.partial(_mixer_kernel, seqs=seqs, tokens=tokens, n_tiles=n_tiles,
                             reset_first=reset_first, sub=sub)
    tile_f32 = pltpu.VMEM((rows, D), F32)
    by_head_f32 = pltpu.VMEM((N_HEADS, rows, HEAD), F32)
    common = [pltpu.VMEM((seqs, SUBLANES + tokens, D), F32),
              tile_f32, tile_f32, tile_f32]
    if decode:
        grid = (n_seq // seqs, N_HEADS)
        x_map = lambda i, j: (i, 0, 0)
        s_block = (seqs, 1, HEAD, HEAD)
        s_map = lambda i, j: (i, j, 0, 0)
        scratch = common + [by_head_f32] * 7
    else:
        grid = (n_seq, n_tiles)
        x_map = lambda i, j: (i, j, 0)
        s_block = (1, N_HEADS, HEAD, HEAD)
        s_map = lambda i, j: (i, 0, 0, 0)
        scratch = common + [tile_f32] * 6 + [pltpu.VMEM((N_HEADS, HEAD, HEAD), F32), pltpu.VMEM((rows, D), BF16)]
    seq_map = lambda i, j: (i, 0, 0)
    if cext.shape[0] == n_seq:
        in_seq_map, in_s_map = seq_map, s_map
    else:
        assert cext.shape[0] == h0.shape[0] == s0.shape[0] == seqs == 1
        in_seq_map = lambda i, j: (0, 0, 0)
        in_s_map = lambda i, j: (0, 0, 0, 0)
    return pl.pallas_call(
        kern,
        grid=grid,
        in_specs=[
            pl.BlockSpec((seqs, tokens, D), x_map),
            pl.BlockSpec((seqs, SUBLANES, D), in_seq_map),
            pl.BlockSpec((seqs, 1, D), in_seq_map),
            pl.BlockSpec(s_block, in_s_map),
            _resident((P_ROWS, D)),
            _resident(w_in.shape),
            _resident(w_ri.shape),
            _resident(w_a.shape),
            _resident(w_b.shape),
            _resident(w_out.shape),
        ],
        out_specs=[
            pl.BlockSpec((seqs, tokens, D), x_map),
            pl.BlockSpec((seqs, SUBLANES, D), seq_map),
            pl.BlockSpec((seqs, 1, D), seq_map),
            pl.BlockSpec(s_block, s_map),
        ],
        out_shape=[
            jax.ShapeDtypeStruct((n_seq, length, D), F32),
            jax.ShapeDtypeStruct((n_seq, SUBLANES, D), F32),
            jax.ShapeDtypeStruct((n_seq, 1, D), F32),
            jax.ShapeDtypeStruct((n_seq, N_HEADS, HEAD, HEAD), F32),
        ],
        scratch_shapes=scratch,
        compiler_params=pltpu.CompilerParams(
            dimension_semantics=("arbitrary", "arbitrary"),
            vmem_limit_bytes=VMEM_LIMIT_BYTES),
        name=name,
    )(x, cext, h0, s0, pvec, w_in, w_ri, w_a, w_b, w_out)


def _mlp_kernel(xa_ref, xb_ref, g_ref, wup_ref, wdn_ref, oa_ref, ob_ref, *, steps_a):
    def tile_mlp(x_ref, o_ref):
        x = x_ref[...]
        hn = _rms(x, g_ref[0:1, :]).astype(BF16)
        acc = jnp.zeros(x.shape, F32)
        for c in range(D_FF // D):
            t = _mm(hn, _unpack_rows(wup_ref[:, c * D:(c + 1) * D]))
            t = jnp.square(jnp.maximum(t, 0.0)).astype(BF16)
            acc = acc + _mm(t, _unpack_rows(wdn_ref[c * (D // 2):(c + 1) * (D // 2), :]))
        o_ref[...] = x + _rms(acc, g_ref[1:2, :])

    i = pl.program_id(0)

    @pl.when(i < steps_a)
    def _first():
        tile_mlp(xa_ref, oa_ref)

    @pl.when(i >= steps_a)
    def _second():
        tile_mlp(xb_ref, ob_ref)


def _mlp(xa, xb, gains, w_up, w_down, *, tile, name):
    na, nb = xa.shape[0], xb.shape[0]
    assert na % tile == 0 and nb % tile == 0
    steps_a, steps_b = na // tile, nb // tile
    a_map = lambda i: (jnp.minimum(i, steps_a - 1), 0)
    b_map = lambda i: (jnp.maximum(i - steps_a, 0), 0)
    return pl.pallas_call(
        functools.partial(_mlp_kernel, steps_a=steps_a),
        grid=(steps_a + steps_b,),
        in_specs=[
            pl.BlockSpec((tile, D), a_map),
            pl.BlockSpec((tile, D), b_map),
            pl.BlockSpec((2, D), lambda i: (0, 0), pipeline_mode=pl.Buffered(1)),
            pl.BlockSpec(w_up.shape, lambda i: (0, 0), pipeline_mode=pl.Buffered(1)),
            pl.BlockSpec(w_down.shape, lambda i: (0, 0), pipeline_mode=pl.Buffered(1)),
        ],
        out_specs=[pl.BlockSpec((tile, D), a_map), pl.BlockSpec((tile, D), b_map)],
        out_shape=[jax.ShapeDtypeStruct((na, D), F32), jax.ShapeDtypeStruct((nb, D), F32)],
        compiler_params=pltpu.CompilerParams(
            dimension_semantics=("arbitrary",),
            vmem_limit_bytes=VMEM_LIMIT_BYTES),
        name=name,
    )(xa, xb, gains, w_up, w_down)


def kernel(x_prompt, x_sample, state_conv, state_rglru, state_hgrn, meta_tokens, norm_gains, w_in,
           conv_w, conv_b, rg_w, rg_b, ig_w, ig_b, lru_lambda, hgrn_lb, hgrn_gnorm,
           w_branch_a, w_branch_b, w_out, w_up, w_down):
    bp, seq, _ = x_prompt.shape
    bs, dec_seq, _ = x_sample.shape
    layer = 0
    pvec = jnp.concatenate([
        norm_gains[layer], conv_w[layer], conv_b[layer][None], 0.5 * rg_b[layer][None], 0.5 * ig_b[layer][None],
        lru_lambda[layer][None], hgrn_lb[layer:layer + 2], hgrn_gnorm[layer][None],
        jnp.zeros((1, D), F32)], axis=0).astype(F32)
    halved = (C_GATE, C_Q, C_OG, C_MA, C_MB)
    scale_in = jnp.concatenate([jnp.full((1, D), 0.5 if g in halved else 1.0, F32) for g in range(8)], axis=1)

    gate_blocks = [w.reshape(N_HEADS * HEAD, HEAD) for w in (rg_w[layer], ig_w[layer])]
    w_in_b, w_ri, w_a, w_b, w_o, w_u, w_d = _pack_weights([
        ([w_in[layer]], scale_in),
        (gate_blocks, 0.5),
        ([w_branch_a[layer]], None),
        ([w_branch_b[layer]], None),
        ([w_out[layer]], None),
        ([w_up[layer]], None),
        ([w_down[layer]], None),
    ], name="pack_weights")
    w_ri = w_ri.reshape(N_HEADS, HEAD // 2, 2 * HEAD)
    weights = (pvec, w_in_b, w_ri, w_a, w_b, w_o)

    zeros_c = jnp.zeros((1, SUBLANES, D), F32)
    zeros_h = jnp.zeros((1, 1, D), F32)
    zeros_s = jnp.zeros((1, N_HEADS, HEAD, HEAD), F32)
    _, c_m, h_m, s_m = _mixer(meta_tokens[None].astype(F32), zeros_c, zeros_h, zeros_s, *weights,
                              seqs=1, tokens=N_META, reset_first=True, sub=N_META, name="mixer_meta")

    x1_p, c_p, h_p, s_p = _mixer(
        x_prompt,
        c_m, h_m, s_m,
        *weights, seqs=1, tokens=256, reset_first=False, sub=64, name="mixer_prompt")

    cext_s = jnp.pad(state_conv[layer], ((0, 0), (SUBLANES - 3, 0), (0, 0)))
    x1_s, c_s, h_s, s_s = _mixer(
        x_sample, cext_s, state_rglru[layer][:, None, :], state_hgrn[layer],
        *weights, seqs=32, tokens=dec_seq, reset_first=False, sub=8, name="mixer_sample")

    gains_mlp = norm_gains[layer, 2:4]
    y_p, y_s = _mlp(x1_p.reshape(bp * seq, D), x1_s.reshape(bs * dec_seq, D), gains_mlp, w_u, w_d,
                    tile=1024, name="mlp")

    return (y_p.reshape(bp, seq, D), y_s.reshape(bs, dec_seq, D),
            c_p[:, SUBLANES - 3:, :][None], h_p[:, 0, :][None], s_p[None],
            c_s[:, SUBLANES - 3:, :][None], h_s[:, 0, :][None], s_s[None])
```

```python
import functools

import jax
import jax.numpy as jnp
from jax import lax
from jax.experimental import pallas as pl
from jax.experimental.pallas import tpu as pltpu

D = 1024
N_HEADS = 8
HEAD = 128
N_META = 16
LRU_C = 8.0
EPS = 1e-6
D_FF = 4096
SUBLANES = 8
F32 = jnp.float32
BF16 = jnp.bfloat16

P_GAIN, P_CONV_W, P_CONV_B, P_RG_B, P_IG_B, P_LAMBDA, P_LB, P_GNORM, P_ROWS = 0, 4, 8, 9, 10, 11, 12, 14, 16

C_U, C_GATE, C_Q, C_F, C_I, C_OG, C_MA, C_MB = range(8)

VMEM_LIMIT_BYTES = 60 * 1024 * 1024
PROMPT_TILE_TOKENS = 256
HGRN_SUB_CHUNK = 64
DECODE_GROUP_SEQS = 32
MLP_TILE_ROWS = 1024
PACK_STEPS = 8
PACK_VMEM_LIMIT_BYTES = 40 * 1024 * 1024
SAFE_DECAY_RANGE = 64.0


def _rms(x, g):
    ms = jnp.mean(x * x, axis=-1, keepdims=True)
    return x * lax.rsqrt(ms + EPS) * g


def _sigmoid_of_twice(h):
    return 0.5 * jnp.tanh(h) + 0.5


def _silu_of_twice(h):
    return h + h * jnp.tanh(h)


_GELU_C1 = 2.0 * 0.7978845608028654
_GELU_C2 = 8.0 * 0.7978845608028654 * 0.044715


def _gelu_tanh_of_twice(h):
    return h + h * jnp.tanh(h * (_GELU_C1 + _GELU_C2 * (h * h)))


def _mm(a, b):
    return jnp.dot(a, b, preferred_element_type=F32)


def _mm_nt(a, b):
    return lax.dot_general(a, b, (((1,), (1,)), ((), ())), preferred_element_type=F32)


def _mm_tn(a, b):
    return lax.dot_general(a, b, (((0,), (0,)), ((), ())), preferred_element_type=F32)


def _mask_matmul(mask_bf, x):
    hi = x.astype(BF16)
    lo = (x - hi.astype(F32)).astype(BF16)
    return _mm(mask_bf, hi) + _mm(mask_bf, lo)


def _pack_kernel(*refs, groups):
    out_refs = refs[len(refs) - len(groups):]
    pos = 0
    for (n_parts, scale_kind, scalar), o_ref in zip(groups, out_refs):
        parts = [refs[pos + p][...] for p in range(n_parts)]
        pos += n_parts
        w = parts[0] if n_parts == 1 else jnp.concatenate(parts, axis=1)
        if scale_kind == "row":
            w = w * refs[pos][...]
            pos += 1
        elif scale_kind == "scalar":
            w = w * scalar
        o_ref[...] = pltpu.bitcast(w.astype(BF16), jnp.int32)


def _pack_weights(groups, *, name):
    operands, in_specs, out_specs, out_shapes, kinds = [], [], [], [], []
    for mats, scale in groups:
        k = mats[0].shape[0]
        tile = k // PACK_STEPS
        assert all(m.shape[0] == k for m in mats) and k % PACK_STEPS == 0 and tile % (2 * SUBLANES) == 0
        n = sum(m.shape[1] for m in mats)
        for m in mats:
            operands.append(m)
            in_specs.append(pl.BlockSpec((tile, m.shape[1]), lambda i: (i, 0)))
        if scale is None:
            kinds.append((len(mats), "none", None))
        elif isinstance(scale, float):
            kinds.append((len(mats), "scalar", scale))
        else:
            kinds.append((len(mats), "row", None))
            operands.append(scale)
            in_specs.append(pl.BlockSpec((1, n), lambda i: (0, 0)))
        out_specs.append(pl.BlockSpec((tile // 2, n), lambda i: (i, 0)))
        out_shapes.append(jax.ShapeDtypeStruct((k // 2, n), jnp.int32))
    return pl.pallas_call(
        functools.partial(_pack_kernel, groups=tuple(kinds)),
        grid=(PACK_STEPS,),
        in_specs=in_specs,
        out_specs=out_specs,
        out_shape=out_shapes,
        compiler_params=pltpu.CompilerParams(
            dimension_semantics=("arbitrary",), vmem_limit_bytes=PACK_VMEM_LIMIT_BYTES),
        name=name,
    )(*operands)


def _unpack_rows(words):
    return pltpu.bitcast(words, BF16)


def _exact_block(q_b, k_b, c_b, v_b):
    t_idx = lax.broadcasted_iota(jnp.int32, (SUBLANES, 1), 0)
    acc = jnp.zeros((SUBLANES, HEAD), F32)
    for s in range(SUBLANES):
        decay = jnp.exp(jnp.minimum(c_b - c_b[s:s + 1, :], 0.0))
        score = jnp.sum(q_b * k_b[s:s + 1, :] * decay, axis=-1, keepdims=True)
        acc = acc + jnp.where(t_idx >= s, score, 0.0) * v_b[s:s + 1, :]
    return acc


def _column_of(row):
    return jnp.transpose(jnp.broadcast_to(row, (HEAD, HEAD)))


def _mixer_kernel(x_ref, cext_ref, h0_ref, s0_ref, pvec_ref, win_ref, wri_ref, wa_ref, wb_ref, wout_ref,
                  x1_ref, cout_ref, hout_ref, sout_ref, *scratch,
                  seqs, tokens, n_tiles, reset_first, sub):
    j = pl.program_id(1)
    rows = seqs * tokens
    decode = seqs > 1
    if decode:
        ubuf, ga_s, sgb_s, sog_s, qf_h, kk_h, cum_h, v_h, tot_h, oin_h, oi_h = scratch
    else:
        ubuf, ga_s, sgb_s, sog_s, hbuf, qf_s, kk_s, cum_s, v_s, o_s, sold_s = scratch

    pv = pvec_ref[...]

    def prow(r):
        return pv[r:r + 1, :]

    def wcol(g):
        return _unpack_rows(win_ref[:, g * D:(g + 1) * D])

    def head_cols(hd):
        return slice(hd * HEAD, (hd + 1) * HEAD)

    def front():
        x = x_ref[...].reshape(rows, D)
        xn = _rms(x, prow(P_GAIN + 0)).astype(BF16)
        row_id = lax.broadcasted_iota(jnp.int32, (rows, 1), 0)

        u = _mm(xn, wcol(C_U))
        ubuf[:, SUBLANES:SUBLANES + tokens, :] = u.reshape(seqs, tokens, D)
        if decode:
            merge_gates(xn)
        uc = prow(P_CONV_B) + prow(P_CONV_W + 3) * u
        for k in range(1, 4):
            shifted = ubuf[:, SUBLANES - k:SUBLANES - k + tokens, :].reshape(rows, D)
            uc = uc + prow(P_CONV_W + 3 - k) * shifted
        tail = ubuf[:, tokens:tokens + SUBLANES, :]
        ubuf[:, 0:SUBLANES, :] = tail
        cout_ref[...] = tail

        ucb = uc.astype(BF16)
        r_parts, i_parts = [], []
        for b in range(N_HEADS):
            ri = _mm(ucb[:, head_cols(b)], _unpack_rows(wri_ref[b]))
            r_parts.append(ri[:, :HEAD])
            i_parts.append(ri[:, HEAD:])
        r_pre = jnp.concatenate(r_parts, axis=1)
        i_pre = jnp.concatenate(i_parts, axis=1)
        if not decode:
            hbuf[...] = r_pre
            o_s[...] = i_pre
        hgrn_operands(xn)
        if not decode:
            r_pre = hbuf[...]
            i_pre = o_s[...]
        r_gate = _sigmoid_of_twice(r_pre + prow(P_RG_B))
        i_gate = _sigmoid_of_twice(i_pre + prow(P_IG_B))
        lam = prow(P_LAMBDA)
        softplus_neg_lam = jnp.maximum(-lam, 0.0) + jnp.log1p(jnp.exp(-jnp.abs(lam)))
        log_a = (-LRU_C) * r_gate * softplus_neg_lam
        a_cum = jnp.exp(log_a)
        th = jnp.tanh(log_a)
        sq = -2.0 * th / (1.0 - th)
        mult = jnp.where(sq > 0.0, sq * lax.rsqrt(sq), 0.0)
        if reset_first:
            mult = jnp.where(jnp.logical_and(row_id == 0, j == 0), 1.0, mult)
        b_cum = mult * i_gate * uc

        a_cum = a_cum.reshape(rows // SUBLANES, SUBLANES, D)
        b_cum = b_cum.reshape(rows // SUBLANES, SUBLANES, D)
        sublane = lax.broadcasted_iota(jnp.int32, (1, SUBLANES, 1), 1)
        for s in (1, 2, 4):
            keep = sublane >= s
            a_prev = jnp.where(keep, pltpu.roll(a_cum, s, 1), 1.0)
            b_prev = jnp.where(keep, pltpu.roll(b_cum, s, 1), 0.0)
            b_cum = a_cum * b_prev + b_cum
            a_cum = a_cum * a_prev
        a_cum = a_cum.reshape(rows, D)
        b_cum = b_cum.reshape(rows, D)

        if decode:
            h_in = jnp.broadcast_to(h0_ref[...], (seqs, SUBLANES, D)).reshape(rows, D)
            h_all = a_cum * h_in + b_cum
            hout_ref[...] = h_all.reshape(seqs, SUBLANES, D)[:, SUBLANES - 1:SUBLANES, :]
        else:
            h = hout_ref[0]
            for g in range(rows // SUBLANES):
                sl = slice(g * SUBLANES, (g + 1) * SUBLANES)
                hg = a_cum[sl] * h + b_cum[sl]
                hbuf[sl, :] = hg
                h = hg[SUBLANES - 1:SUBLANES, :]
            hout_ref[0] = h
            h_all = hbuf[...]

        ya = (h_all * _gelu_tanh_of_twice(_mm(xn, wcol(C_GATE)))).astype(BF16)
        ga_s[...] = _sigmoid_of_twice(_mm(xn, wcol(C_MA))) * _mm(ya, _unpack_rows(wa_ref[...]))
        return xn

    def merge_gates(xn):
        sgb_s[...] = _sigmoid_of_twice(_mm(xn, wcol(C_MB)))
        sog_s[...] = _silu_of_twice(_mm(xn, wcol(C_OG)))

    def hgrn_operands(xn):
        qf = _silu_of_twice(_mm(xn, wcol(C_Q)))
        lb_raw = pv[P_LB:P_LB + 2, :]
        lb_exp = jnp.exp(lb_raw - jnp.max(lb_raw, axis=0, keepdims=True))
        lb = lb_exp[0:1, :] / jnp.sum(lb_exp, axis=0, keepdims=True)
        fg = lb + (1.0 - lb) / (1.0 + jnp.exp(-_mm(xn, wcol(C_F))))
        logf = jnp.log(fg)
        kk = 1.0 - fg
        v = _mm(xn, wcol(C_I))

        ri2 = lax.broadcasted_iota(jnp.int32, (rows, rows), 0)
        ci2 = lax.broadcasted_iota(jnp.int32, (rows, rows), 1)
        if decode:
            same_seq = (ri2 // tokens) == (ci2 // tokens)
            total = _mask_matmul(same_seq.astype(BF16), logf)
            cum = _mask_matmul(jnp.logical_and(ci2 <= ri2, same_seq).astype(BF16), logf)
            for hd in range(N_HEADS):
                hs = head_cols(hd)
                v_h[hd] = v[:, hs]
                tot_h[hd] = total[:, hs]
                qf_h[hd] = qf[:, hs]
                kk_h[hd] = kk[:, hs]
                cum_h[hd] = cum[:, hs]
        else:
            qf_s[...] = qf
            kk_s[...] = kk
            v_s[...] = v
            cum_s[...] = _mask_matmul((ci2 <= ri2).astype(BF16), logf)

    def head_norm(o_h):
        ms = jnp.mean(o_h * o_h, axis=-1, keepdims=True)
        return o_h * lax.rsqrt(ms + EPS)

    def finish(o_heads):
        o_n = jnp.concatenate([head_norm(o_h) for o_h in o_heads], axis=1) * prow(P_GNORM)
        yb = (o_n * sog_s[...]).astype(BF16)
        mixed = ga_s[...] + sgb_s[...] * _mm(yb, _unpack_rows(wb_ref[...]))
        z = _mm(mixed.astype(BF16), _unpack_rows(wout_ref[...]))
        x1 = x_ref[...].reshape(rows, D) + _rms(z, prow(P_GAIN + 1))
        x1_ref[...] = x1.reshape(seqs, tokens, D)

    def back_prompt(xnb):
        qf = qf_s[...]
        kk = kk_s[...]
        cum = cum_s[...]
        vb = v_s[...].astype(BF16)
        n_sub = rows // sub
        last = cum[rows - 1:rows, :]
        qib = (qf * jnp.exp(cum)).astype(BF16)
        ksb = (kk * jnp.exp(last - cum)).astype(BF16)
        decay_row = jnp.exp(last)
        def score_blocks(hd):
            hs = head_cols(hd)
            parts = []
            for i in range(n_sub):
                r0 = i * sub
                r1 = r0 + sub
                width = min(rows, -(-r1 // HEAD) * HEAD)
                ref_row = cum[r0 - 1:r0, hs] if i > 0 else jnp.zeros((1, HEAD), F32)
                qd = (qf[r0:r1, hs] * jnp.exp(cum[r0:r1, hs] - ref_row)).astype(BF16)
                arg = ref_row - cum[0:width, hs]
                if width > r1:
                    arg = jnp.where(lax.broadcasted_iota(jnp.int32, (width, 1), 0) < r1, arg, 0.0)
                kdi = (kk[0:width, hs] * jnp.exp(arg)).astype(BF16)
                att = _mm_nt(qd, kdi)
                rr = lax.broadcasted_iota(jnp.int32, (sub, width), 0) + r0
                cc = lax.broadcasted_iota(jnp.int32, (sub, width), 1)
                att = jnp.where(cc <= rr, att, 0.0).astype(BF16)
                if width < rows:
                    att = jnp.concatenate([att, jnp.zeros((sub, rows - width), BF16)], axis=1)
                parts.append(att)
            return jnp.concatenate(parts, axis=0) if n_sub > 1 else parts[0]

        def state_and_gate_quarter(hd, s_old):
            hs = head_cols(hd)
            upd = _mm_tn(ksb[:, hs], vb[:, hs])
            sout_ref[0, hd] = _column_of(decay_row[:, hs]) * s_old + upd
            sold_s[hd] = s_old
            quarter = D // 4
            grp, part = (C_MB, hd) if hd < 4 else (C_OG, hd - 4)
            cols = slice(part * quarter, (part + 1) * quarter)
            proj = _mm(xnb, _unpack_rows(win_ref[:, grp * D + part * quarter:grp * D + (part + 1) * quarter]))
            if hd < 4:
                sgb_s[:, cols] = _sigmoid_of_twice(proj)
            else:
                sog_s[:, cols] = _silu_of_twice(proj)

        att_next = score_blocks(0)
        o_heads = []
        for hd in range(N_HEADS):
            hs = head_cols(hd)
            att_full = att_next
            s_old = sout_ref[0, hd]
            state_and_gate_quarter(hd, s_old)
            if hd + 1 < N_HEADS:
                att_next = score_blocks(hd + 1)
            if rows % HEAD == 0:
                o_h = _mm(jnp.concatenate([att_full, qib[:, hs]], axis=1),
                          jnp.concatenate([vb[:, hs], s_old.astype(BF16)], axis=0))
            else:
                o_h = _mm(att_full, vb[:, hs]) + _mm(qib[:, hs], s_old.astype(BF16))
            o_heads.append(o_h)

        finish(o_heads)

        worst = jnp.zeros((1, D), F32)
        for i in range(n_sub):
            start = cum[i * sub - 1:i * sub, :] if i > 0 else jnp.zeros((1, D), F32)
            worst = jnp.maximum(worst, start - cum[(i + 1) * sub - 1:(i + 1) * sub, :])
        out_of_range = jnp.max(worst) > SAFE_DECAY_RANGE

        @pl.when(out_of_range)
        def _exact_scores():
            row_i = lax.broadcasted_iota(jnp.int32, (rows, 1), 0)
            for hd in range(N_HEADS):
                hs = head_cols(hd)
                qib_h = (qf_s[:, hs] * jnp.exp(cum_s[:, hs])).astype(BF16)
                o_s[:, hs] = _mm(qib_h, sold_s[hd].astype(BF16))

                def block_body(b, carry, hs=hs):
                    r0 = pl.multiple_of(b * SUBLANES, SUBLANES)
                    rs = pl.ds(r0, SUBLANES)
                    q_b, k_b, c_b, v_b = qf_s[rs, hs], kk_s[rs, hs], cum_s[rs, hs], v_s[rs, hs]
                    prev_start = pl.multiple_of(jnp.maximum(r0 - SUBLANES, 0), SUBLANES)
                    before = cum_s[pl.ds(prev_start, SUBLANES), hs][SUBLANES - 1:SUBLANES, :]
                    ref_row = jnp.where(b > 0, before, 0.0)
                    qd = (q_b * jnp.exp(c_b - ref_row)).astype(BF16)
                    arg = jnp.minimum(ref_row - cum_s[:, hs], 0.0)
                    kd = jnp.where(row_i < r0, kk_s[:, hs] * jnp.exp(arg), 0.0).astype(BF16)
                    earlier = _mm(_mm_nt(qd, kd).astype(BF16), v_s[:, hs].astype(BF16))
                    o_s[rs, hs] = o_s[rs, hs] + earlier + _exact_block(q_b, k_b, c_b, v_b)
                    return carry

                lax.fori_loop(0, rows // SUBLANES, block_body, 0)

            finish([o_s[:, head_cols(hd)] for hd in range(N_HEADS)])

    def decode_state_step():
        def seq_body(g, carry):
            rsl = pl.ds(pl.multiple_of(g * SUBLANES, SUBLANES), SUBLANES)
            s_old = s0_ref[g, 0]
            c_b = cum_h[j, rsl, :]
            total_b = tot_h[j, rsl, :]
            qi = qf_h[j, rsl, :] * jnp.exp(c_b)
            ks = kk_h[j, rsl, :] * jnp.exp(total_b - c_b)
            oin_h[j, rsl, :] = _mm(qi.astype(BF16), s_old.astype(BF16))
            upd = _mm_tn(ks.astype(BF16), v_h[j, rsl, :].astype(BF16))
            decay = _column_of(jnp.exp(total_b[0:1, :]))
            sout_ref[g, 0] = decay * s_old + upd
            return carry

        lax.fori_loop(0, seqs, seq_body, 0, unroll=SUBLANES)

    def back_decode():
        ri = lax.broadcasted_iota(jnp.int32, (rows, rows), 0)
        ci = lax.broadcasted_iota(jnp.int32, (rows, rows), 1)
        causal = jnp.logical_and(ci <= ri, (ri // tokens) == (ci // tokens))
        def masked_scores(hd):
            c_h = cum_h[hd]
            scores = _mm_nt((qf_h[hd] * jnp.exp(c_h)).astype(BF16), (kk_h[hd] * jnp.exp(-c_h)).astype(BF16))
            return jnp.where(causal, scores, 0.0).astype(BF16)

        o_heads = []
        att_next = masked_scores(0)
        for hd in range(N_HEADS):
            att = att_next
            if hd + 1 < N_HEADS:
                att_next = masked_scores(hd + 1)
            o_heads.append(oin_h[hd] + _mm(att, v_h[hd].astype(BF16)))
        finish(o_heads)

        worst = jnp.zeros((1, HEAD), F32)
        for hd in range(N_HEADS):
            worst = jnp.maximum(worst, jnp.max(-tot_h[hd], axis=0, keepdims=True))
        out_of_range = jnp.max(worst) > SAFE_DECAY_RANGE

        @pl.when(out_of_range)
        def _exact_scores():
            for hd in range(N_HEADS):
                def seq_body(g, carry, hd=hd):
                    rs = pl.ds(pl.multiple_of(g * SUBLANES, SUBLANES), SUBLANES)
                    oi_h[hd, rs, :] = _exact_block(qf_h[hd, rs, :], kk_h[hd, rs, :], cum_h[hd, rs, :], v_h[hd, rs, :])
                    return carry

                lax.fori_loop(0, seqs, seq_body, 0)

            finish([oin_h[hd] + oi_h[hd] for hd in range(N_HEADS)])

    if decode:
        @pl.when(j == 0)
        def _front_region():
            ubuf[:, 0:SUBLANES, :] = cext_ref[...]
            front()

        decode_state_step()

        @pl.when(j == N_HEADS - 1)
        def _back_region():
            back_decode()
    else:
        @pl.when(j == 0)
        def _init():
            ubuf[:, 0:SUBLANES, :] = cext_ref[...]
            hout_ref[...] = h0_ref[...]
            sout_ref[...] = s0_ref[...]

        back_prompt(front())


def _resident(shape):
    return pl.BlockSpec(shape, lambda i, j: (0,) * len(shape), pipeline_mode=pl.Buffered(1))


def _mixer(x, cext, h0, s0, pvec, w_in, w_ri, w_a, w_b, w_out, *, seqs, tokens, reset_first, sub, name):
    n_seq, length, _ = x.shape
    n_tiles = length // tokens
    decode = seqs > 1
    assert n_seq % seqs == 0 and length % tokens == 0 and tokens % SUBLANES == 0
    assert not decode or (n_tiles == 1 and tokens == SUBLANES)
    assert (seqs * tokens) % sub == 0
    rows = seqs * tokens
    kern = functools.partial(_mixer_kernel, seqs=seqs, tokens=tokens, n_tiles=n_tiles,
                             reset_first=reset_first, sub=sub)
    tile_f32 = pltpu.VMEM((rows, D), F32)
    by_head_f32 = pltpu.VMEM((N_HEADS, rows, HEAD), F32)
    common = [pltpu.VMEM((seqs, SUBLANES + tokens, D), F32),
              tile_f32, tile_f32, tile_f32]
    if decode:
        grid = (n_seq // seqs, N_HEADS)
        x_map = lambda i, j: (i, 0, 0)
        s_block = (seqs, 1, HEAD, HEAD)
        s_map = lambda i, j: (i, j, 0, 0)
        scratch = common + [by_head_f32] * 7
    else:
        grid = (n_seq, n_tiles)
        x_map = lambda i, j: (i, j, 0)
        s_block = (1, N_HEADS, HEAD, HEAD)
        s_map = lambda i, j: (i, 0, 0, 0)
        scratch = common + [tile_f32] * 6 + [pltpu.VMEM((N_HEADS, HEAD, HEAD), F32)]
    seq_map = lambda i, j: (i, 0, 0)
    if cext.shape[0] == n_seq:
        in_seq_map, in_s_map = seq_map, s_map
    else:
        assert cext.shape[0] == h0.shape[0] == s0.shape[0] == seqs == 1
        in_seq_map = lambda i, j: (0, 0, 0)
        in_s_map = lambda i, j: (0, 0, 0, 0)
    return pl.pallas_call(
        kern,
        grid=grid,
        in_specs=[
            pl.BlockSpec((seqs, tokens, D), x_map),
            pl.BlockSpec((seqs, SUBLANES, D), in_seq_map),
            pl.BlockSpec((seqs, 1, D), in_seq_map),
            pl.BlockSpec(s_block, in_s_map),
            _resident((P_ROWS, D)),
            _resident(w_in.shape),
            _resident(w_ri.shape),
            _resident(w_a.shape),
            _resident(w_b.shape),
            _resident(w_out.shape),
        ],
        out_specs=[
            pl.BlockSpec((seqs, tokens, D), x_map),
            pl.BlockSpec((seqs, SUBLANES, D), seq_map),
            pl.BlockSpec((seqs, 1, D), seq_map),
            pl.BlockSpec(s_block, s_map),
        ],
        out_shape=[
            jax.ShapeDtypeStruct((n_seq, length, D), F32),
            jax.ShapeDtypeStruct((n_seq, SUBLANES, D), F32),
            jax.ShapeDtypeStruct((n_seq, 1, D), F32),
            jax.ShapeDtypeStruct((n_seq, N_HEADS, HEAD, HEAD), F32),
        ],
        scratch_shapes=scratch,
        compiler_params=pltpu.CompilerParams(
            dimension_semantics=("arbitrary", "arbitrary"),
            vmem_limit_bytes=VMEM_LIMIT_BYTES),
        name=name,
    )(x, cext, h0, s0, pvec, w_in, w_ri, w_a, w_b, w_out)


def _mlp_kernel(xa_ref, xb_ref, g_ref, wup_ref, wdn_ref, oa_ref, ob_ref, *, steps_a):
    def tile_mlp(x_ref, o_ref):
        x = x_ref[...]
        hn = _rms(x, g_ref[0:1, :]).astype(BF16)
        acc = jnp.zeros(x.shape, F32)
        for c in range(D_FF // D):
            t = _mm(hn, _unpack_rows(wup_ref[:, c * D:(c + 1) * D]))
            t = jnp.square(jnp.maximum(t, 0.0)).astype(BF16)
            acc = acc + _mm(t, _unpack_rows(wdn_ref[c * (D // 2):(c + 1) * (D // 2), :]))
        o_ref[...] = x + _rms(acc, g_ref[1:2, :])

    i = pl.program_id(0)

    @pl.when(i < steps_a)
    def _first():
        tile_mlp(xa_ref, oa_ref)

    @pl.when(i >= steps_a)
    def _second():
        tile_mlp(xb_ref, ob_ref)


def _mlp(xa, xb, gains, w_up, w_down, *, tile, name):
    na, nb = xa.shape[0], xb.shape[0]
    assert na % tile == 0 and nb % tile == 0
    steps_a, steps_b = na // tile, nb // tile
    a_map = lambda i: (jnp.minimum(i, steps_a - 1), 0)
    b_map = lambda i: (jnp.maximum(i - steps_a, 0), 0)
    return pl.pallas_call(
        functools.partial(_mlp_kernel, steps_a=steps_a),
        grid=(steps_a + steps_b,),
        in_specs=[
            pl.BlockSpec((tile, D), a_map),
            pl.BlockSpec((tile, D), b_map),
            pl.BlockSpec((2, D), lambda i: (0, 0), pipeline_mode=pl.Buffered(1)),
            pl.BlockSpec(w_up.shape, lambda i: (0, 0), pipeline_mode=pl.Buffered(1)),
            pl.BlockSpec(w_down.shape, lambda i: (0, 0), pipeline_mode=pl.Buffered(1)),
        ],
        out_specs=[pl.BlockSpec((tile, D), a_map), pl.BlockSpec((tile, D), b_map)],
        out_shape=[jax.ShapeDtypeStruct((na, D), F32), jax.ShapeDtypeStruct((nb, D), F32)],
        compiler_params=pltpu.CompilerParams(
            dimension_semantics=("arbitrary",),
            vmem_limit_bytes=VMEM_LIMIT_BYTES),
        name=name,
    )(xa, xb, gains, w_up, w_down)


def kernel(x_prompt, x_sample, state_conv, state_rglru, state_hgrn, meta_tokens, norm_gains, w_in,
           conv_w, conv_b, rg_w, rg_b, ig_w, ig_b, lru_lambda, hgrn_lb, hgrn_gnorm,
           w_branch_a, w_branch_b, w_out, w_up, w_down):
    bp, seq, _ = x_prompt.shape
    bs, dec_seq, _ = x_sample.shape
    layer = 0
    pvec = jnp.concatenate([
        norm_gains[layer], conv_w[layer], conv_b[layer][None], 0.5 * rg_b[layer][None], 0.5 * ig_b[layer][None],
        lru_lambda[layer][None], hgrn_lb[layer:layer + 2], hgrn_gnorm[layer][None],
        jnp.zeros((1, D), F32)], axis=0).astype(F32)
    halved = (C_GATE, C_Q, C_OG, C_MA, C_MB)
    scale_in = jnp.concatenate([jnp.full((1, D), 0.5 if g in halved else 1.0, F32) for g in range(8)], axis=1)

    gate_blocks = [w.reshape(N_HEADS * HEAD, HEAD) for w in (rg_w[layer], ig_w[layer])]
    w_in_b, w_ri, w_a, w_b, w_o, w_u, w_d = _pack_weights([
        ([w_in[layer]], scale_in),
        (gate_blocks, 0.5),
        ([w_branch_a[layer]], None),
        ([w_branch_b[layer]], None),
        ([w_out[layer]], None),
        ([w_up[layer]], None),
        ([w_down[layer]], None),
    ], name="pack_weights")
    w_ri = w_ri.reshape(N_HEADS, HEAD // 2, 2 * HEAD)
    weights = (pvec, w_in_b, w_ri, w_a, w_b, w_o)

    zeros_c = jnp.zeros((1, SUBLANES, D), F32)
    zeros_h = jnp.zeros((1, 1, D), F32)
    zeros_s = jnp.zeros((1, N_HEADS, HEAD, HEAD), F32)
    _, c_m, h_m, s_m = _mixer(meta_tokens[None].astype(F32), zeros_c, zeros_h, zeros_s, *weights,
                              seqs=1, tokens=N_META, reset_first=True, sub=N_META, name="mixer_meta")

    x1_p, c_p, h_p, s_p = _mixer(
        x_prompt,
        c_m, h_m, s_m,
        *weights, seqs=1, tokens=PROMPT_TILE_TOKENS, reset_first=False, sub=HGRN_SUB_CHUNK, name="mixer_prompt")

    cext_s = jnp.pad(state_conv[layer], ((0, 0), (SUBLANES - 3, 0), (0, 0)))
    x1_s, c_s, h_s, s_s = _mixer(
        x_sample, cext_s, state_rglru[layer][:, None, :], state_hgrn[layer],
        *weights, seqs=DECODE_GROUP_SEQS, tokens=dec_seq, reset_first=False, sub=dec_seq, name="mixer_sample")

    gains_mlp = norm_gains[layer, 2:4]
    y_p, y_s = _mlp(x1_p.reshape(bp * seq, D), x1_s.reshape(bs * dec_seq, D), gains_mlp, w_u, w_d,
                    tile=MLP_TILE_ROWS, name="mlp")

    return (y_p.reshape(bp, seq, D), y_s.reshape(bs, dec_seq, D),
            c_p[:, SUBLANES - 3:, :][None], h_p[:, 0, :][None], s_p[None],
            c_s[:, SUBLANES - 3:, :][None], h_s[:, 0, :][None], s_s[None])
```

```python
import functools

import jax
import jax.numpy as jnp
from jax import lax
from jax.experimental import pallas as pl
from jax.experimental.pallas import tpu as pltpu

D = 1024
N_HEADS = 8
HEAD = 128
N_META = 16
LRU_C = 8.0
EPS = 1e-6
D_FF = 4096
SUBLANES = 8
F32 = jnp.float32
BF16 = jnp.bfloat16

P_GAIN, P_CONV_W, P_CONV_B, P_RG_B, P_IG_B, P_LAMBDA, P_LB, P_GNORM, P_ROWS = 0, 4, 8, 9, 10, 11, 12, 14, 16

C_U, C_GATE, C_Q, C_F, C_I, C_OG, C_MA, C_MB = range(8)

VMEM_LIMIT_BYTES = 60 * 1024 * 1024
PROMPT_TILE_TOKENS = 256
HGRN_SUB_CHUNK = 64
DECODE_GROUP_SEQS = 32
MLP_TILE_ROWS = 1024
PACK_STEPS = 8
PACK_VMEM_LIMIT_BYTES = 40 * 1024 * 1024
SAFE_DECAY_RANGE = 64.0


def _rms(x, g):
    ms = jnp.mean(x * x, axis=-1, keepdims=True)
    return x * lax.rsqrt(ms + EPS) * g


def _sigmoid_of_twice(h):
    return 0.5 * jnp.tanh(h) + 0.5


def _silu_of_twice(h):
    return h + h * jnp.tanh(h)


_GELU_C1 = 2.0 * 0.7978845608028654
_GELU_C2 = 8.0 * 0.7978845608028654 * 0.044715


def _gelu_tanh_of_twice(h):
    return h + h * jnp.tanh(h * (_GELU_C1 + _GELU_C2 * (h * h)))


def _mm(a, b):
    return jnp.dot(a, b, preferred_element_type=F32)


def _mm_nt(a, b):
    return lax.dot_general(a, b, (((1,), (1,)), ((), ())), preferred_element_type=F32)


def _mm_tn(a, b):
    return lax.dot_general(a, b, (((0,), (0,)), ((), ())), preferred_element_type=F32)


def _mask_matmul(mask_bf, x):
    hi = x.astype(BF16)
    lo = (x - hi.astype(F32)).astype(BF16)
    return _mm(mask_bf, hi) + _mm(mask_bf, lo)


def _pack_kernel(*refs, groups):
    out_refs = refs[len(refs) - len(groups):]
    pos = 0
    for (n_parts, scale_kind, scalar), o_ref in zip(groups, out_refs):
        parts = [refs[pos + p][...] for p in range(n_parts)]
        pos += n_parts
        w = parts[0] if n_parts == 1 else jnp.concatenate(parts, axis=1)
        if scale_kind == "row":
            w = w * refs[pos][...]
            pos += 1
        elif scale_kind == "scalar":
            w = w * scalar
        o_ref[...] = pltpu.bitcast(w.astype(BF16), jnp.int32)


def _pack_weights(groups, *, name):
    operands, in_specs, out_specs, out_shapes, kinds = [], [], [], [], []
    for mats, scale in groups:
        k = mats[0].shape[0]
        tile = k // PACK_STEPS
        assert all(m.shape[0] == k for m in mats) and k % PACK_STEPS == 0 and tile % (2 * SUBLANES) == 0
        n = sum(m.shape[1] for m in mats)
        for m in mats:
            operands.append(m)
            in_specs.append(pl.BlockSpec((tile, m.shape[1]), lambda i: (i, 0)))
        if scale is None:
            kinds.append((len(mats), "none", None))
        elif isinstance(scale, float):
            kinds.append((len(mats), "scalar", scale))
        else:
            kinds.append((len(mats), "row", None))
            operands.append(scale)
            in_specs.append(pl.BlockSpec((1, n), lambda i: (0, 0)))
        out_specs.append(pl.BlockSpec((tile // 2, n), lambda i: (i, 0)))
        out_shapes.append(jax.ShapeDtypeStruct((k // 2, n), jnp.int32))
    return pl.pallas_call(
        functools.partial(_pack_kernel, groups=tuple(kinds)),
        grid=(PACK_STEPS,),
        in_specs=in_specs,
        out_specs=out_specs,
        out_shape=out_shapes,
        compiler_params=pltpu.CompilerParams(
            dimension_semantics=("arbitrary",), vmem_limit_bytes=PACK_VMEM_LIMIT_BYTES),
        name=name,
    )(*operands)


def _unpack_rows(words):
    return pltpu.bitcast(words, BF16)


def _exact_block(q_b, k_b, c_b, v_b):
    t_idx = lax.broadcasted_iota(jnp.int32, (SUBLANES, 1), 0)
    acc = jnp.zeros((SUBLANES, HEAD), F32)
    for s in range(SUBLANES):
        decay = jnp.exp(jnp.minimum(c_b - c_b[s:s + 1, :], 0.0))
        score = jnp.sum(q_b * k_b[s:s + 1, :] * decay, axis=-1, keepdims=True)
        acc = acc + jnp.where(t_idx >= s, score, 0.0) * v_b[s:s + 1, :]
    return acc


def _column_of(row):
    return jnp.transpose(jnp.broadcast_to(row, (HEAD, HEAD)))


def _mixer_kernel(x_ref, cext_ref, h0_ref, s0_ref, pvec_ref, win_ref, wri_ref, wa_ref, wb_ref, wout_ref,
                  x1_ref, cout_ref, hout_ref, sout_ref, *scratch,
                  seqs, tokens, n_tiles, reset_first, sub, states_only=False):
    j = pl.program_id(1)
    rows = seqs * tokens
    decode = seqs > 1
    if decode:
        ubuf, ga_s, sgb_s, sog_s, qf_h, kk_h, cum_h, v_h, tot_h, oin_h, oi_h = scratch
    else:
        ubuf, ga_s, sgb_s, sog_s, hbuf, qf_s, kk_s, cum_s, v_s, o_s, sold_s = scratch

    pv = pvec_ref[...]

    def prow(r):
        return pv[r:r + 1, :]

    def wcol(g):
        return _unpack_rows(win_ref[:, g * D:(g + 1) * D])

    def head_cols(hd):
        return slice(hd * HEAD, (hd + 1) * HEAD)

    def front():
        x = x_ref[...].reshape(rows, D)
        xn = _rms(x, prow(P_GAIN + 0)).astype(BF16)
        row_id = lax.broadcasted_iota(jnp.int32, (rows, 1), 0)

        u = _mm(xn, wcol(C_U))
        ubuf[:, SUBLANES:SUBLANES + tokens, :] = u.reshape(seqs, tokens, D)
        if decode:
            merge_gates(xn)
        uc = prow(P_CONV_B) + prow(P_CONV_W + 3) * u
        for k in range(1, 4):
            shifted = ubuf[:, SUBLANES - k:SUBLANES - k + tokens, :].reshape(rows, D)
            uc = uc + prow(P_CONV_W + 3 - k) * shifted
        tail = ubuf[:, tokens:tokens + SUBLANES, :]
        ubuf[:, 0:SUBLANES, :] = tail
        cout_ref[...] = tail

        ucb = uc.astype(BF16)
        r_parts, i_parts = [], []
        for b in range(N_HEADS):
            ri = _mm(ucb[:, head_cols(b)], _unpack_rows(wri_ref[b]))
            r_parts.append(ri[:, :HEAD])
            i_parts.append(ri[:, HEAD:])
        r_pre = jnp.concatenate(r_parts, axis=1)
        i_pre = jnp.concatenate(i_parts, axis=1)
        if not decode:
            hbuf[...] = r_pre
            o_s[...] = i_pre
        hgrn_operands(xn)
        if not decode:
            r_pre = hbuf[...]
            i_pre = o_s[...]
        r_gate = _sigmoid_of_twice(r_pre + prow(P_RG_B))
        i_gate = _sigmoid_of_twice(i_pre + prow(P_IG_B))
        lam = prow(P_LAMBDA)
        softplus_neg_lam = jnp.maximum(-lam, 0.0) + jnp.log1p(jnp.exp(-jnp.abs(lam)))
        log_a = (-LRU_C) * r_gate * softplus_neg_lam
        a_cum = jnp.exp(log_a)
        th = jnp.tanh(log_a)
        sq = -2.0 * th / (1.0 - th)
        mult = jnp.where(sq > 0.0, sq * lax.rsqrt(sq), 0.0)
        if reset_first:
            mult = jnp.where(jnp.logical_and(row_id == 0, j == 0), 1.0, mult)
        b_cum = mult * i_gate * uc

        a_cum = a_cum.reshape(rows // SUBLANES, SUBLANES, D)
        b_cum = b_cum.reshape(rows // SUBLANES, SUBLANES, D)
        sublane = lax.broadcasted_iota(jnp.int32, (1, SUBLANES, 1), 1)
        for s in (1, 2, 4):
            keep = sublane >= s
            a_prev = jnp.where(keep, pltpu.roll(a_cum, s, 1), 1.0)
            b_prev = jnp.where(keep, pltpu.roll(b_cum, s, 1), 0.0)
            b_cum = a_cum * b_prev + b_cum
            a_cum = a_cum * a_prev
        a_cum = a_cum.reshape(rows, D)
        b_cum = b_cum.reshape(rows, D)

        if decode:
            h_in = jnp.broadcast_to(h0_ref[...], (seqs, SUBLANES, D)).reshape(rows, D)
            h_all = a_cum * h_in + b_cum
            hout_ref[...] = h_all.reshape(seqs, SUBLANES, D)[:, SUBLANES - 1:SUBLANES, :]
        else:
            h = hout_ref[0]
            for g in range(rows // SUBLANES):
                sl = slice(g * SUBLANES, (g + 1) * SUBLANES)
                hg = a_cum[sl] * h + b_cum[sl]
                hbuf[sl, :] = hg
                h = hg[SUBLANES - 1:SUBLANES, :]
            hout_ref[0] = h
            h_all = hbuf[...]

        if states_only:
            return xn
        ya = (h_all * _gelu_tanh_of_twice(_mm(xn, wcol(C_GATE)))).astype(BF16)
        ga_s[...] = _sigmoid_of_twice(_mm(xn, wcol(C_MA))) * _mm(ya, _unpack_rows(wa_ref[...]))
        return xn

    def merge_gates(xn):
        sgb_s[...] = _sigmoid_of_twice(_mm(xn, wcol(C_MB)))
        sog_s[...] = _silu_of_twice(_mm(xn, wcol(C_OG)))

    def hgrn_operands(xn):
        if not states_only:
            qf = _silu_of_twice(_mm(xn, wcol(C_Q)))
        lb_raw = pv[P_LB:P_LB + 2, :]
        lb_exp = jnp.exp(lb_raw - jnp.max(lb_raw, axis=0, keepdims=True))
        lb = lb_exp[0:1, :] / jnp.sum(lb_exp, axis=0, keepdims=True)
        fg = lb + (1.0 - lb) / (1.0 + jnp.exp(-_mm(xn, wcol(C_F))))
        logf = jnp.log(fg)
        kk = 1.0 - fg
        v = _mm(xn, wcol(C_I))

        ri2 = lax.broadcasted_iota(jnp.int32, (rows, rows), 0)
        ci2 = lax.broadcasted_iota(jnp.int32, (rows, rows), 1)
        if decode:
            same_seq = (ri2 // tokens) == (ci2 // tokens)
            total = _mask_matmul(same_seq.astype(BF16), logf)
            cum = _mask_matmul(jnp.logical_and(ci2 <= ri2, same_seq).astype(BF16), logf)
            for hd in range(N_HEADS):
                hs = head_cols(hd)
                v_h[hd] = v[:, hs]
                tot_h[hd] = total[:, hs]
                qf_h[hd] = qf[:, hs]
                kk_h[hd] = kk[:, hs]
                cum_h[hd] = cum[:, hs]
        else:
            if not states_only:
                qf_s[...] = qf
            kk_s[...] = kk
            v_s[...] = v
            cum_s[...] = _mask_matmul((ci2 <= ri2).astype(BF16), logf)

    def head_norm(o_h):
        ms = jnp.mean(o_h * o_h, axis=-1, keepdims=True)
        return o_h * lax.rsqrt(ms + EPS)

    def finish(o_heads):
        o_n = jnp.concatenate([head_norm(o_h) for o_h in o_heads], axis=1) * prow(P_GNORM)
        yb = (o_n * sog_s[...]).astype(BF16)
        mixed = ga_s[...] + sgb_s[...] * _mm(yb, _unpack_rows(wb_ref[...]))
        z = _mm(mixed.astype(BF16), _unpack_rows(wout_ref[...]))
        x1 = x_ref[...].reshape(rows, D) + _rms(z, prow(P_GAIN + 1))
        x1_ref[...] = x1.reshape(seqs, tokens, D)

    def back_prompt(xnb):
        qf = qf_s[...]
        kk = kk_s[...]
        cum = cum_s[...]
        vb = v_s[...].astype(BF16)
        n_sub = rows // sub
        last = cum[rows - 1:rows, :]
        qib = (qf * jnp.exp(cum)).astype(BF16)
        ksb = (kk * jnp.exp(last - cum)).astype(BF16)
        decay_row = jnp.exp(last)
        def score_blocks(hd):
            hs = head_cols(hd)
            parts = []
            for i in range(n_sub):
                r0 = i * sub
                r1 = r0 + sub
                width = min(rows, -(-r1 // HEAD) * HEAD)
                ref_row = cum[r0 - 1:r0, hs] if i > 0 else jnp.zeros((1, HEAD), F32)
                qd = (qf[r0:r1, hs] * jnp.exp(cum[r0:r1, hs] - ref_row)).astype(BF16)
                arg = ref_row - cum[0:width, hs]
                if width > r1:
                    arg = jnp.where(lax.broadcasted_iota(jnp.int32, (width, 1), 0) < r1, arg, 0.0)
                kdi = (kk[0:width, hs] * jnp.exp(arg)).astype(BF16)
                att = _mm_nt(qd, kdi)
                rr = lax.broadcasted_iota(jnp.int32, (sub, width), 0) + r0
                cc = lax.broadcasted_iota(jnp.int32, (sub, width), 1)
                att = jnp.where(cc <= rr, att, 0.0).astype(BF16)
                if width < rows:
                    att = jnp.concatenate([att, jnp.zeros((sub, rows - width), BF16)], axis=1)
                parts.append(att)
            return jnp.concatenate(parts, axis=0) if n_sub > 1 else parts[0]

        def state_and_gate_quarter(hd, s_old):
            hs = head_cols(hd)
            upd = _mm_tn(ksb[:, hs], vb[:, hs])
            sout_ref[0, hd] = _column_of(decay_row[:, hs]) * s_old + upd
            sold_s[hd] = s_old
            quarter = D // 4
            grp, part = (C_MB, hd) if hd < 4 else (C_OG, hd - 4)
            cols = slice(part * quarter, (part + 1) * quarter)
            proj = _mm(xnb, _unpack_rows(win_ref[:, grp * D + part * quarter:grp * D + (part + 1) * quarter]))
            if hd < 4:
                sgb_s[:, cols] = _sigmoid_of_twice(proj)
            else:
                sog_s[:, cols] = _silu_of_twice(proj)

        att_next = score_blocks(0)
        o_heads = []
        for hd in range(N_HEADS):
            hs = head_cols(hd)
            att_full = att_next
            s_old = sout_ref[0, hd]
            state_and_gate_quarter(hd, s_old)
            if hd + 1 < N_HEADS:
                att_next = score_blocks(hd + 1)
            if rows % HEAD == 0:
                o_h = _mm(jnp.concatenate([att_full, qib[:, hs]], axis=1),
                          jnp.concatenate([vb[:, hs], s_old.astype(BF16)], axis=0))
            else:
                o_h = _mm(att_full, vb[:, hs]) + _mm(qib[:, hs], s_old.astype(BF16))
            o_heads.append(o_h)

        finish(o_heads)

        worst = jnp.zeros((1, D), F32)
        for i in range(n_sub):
            start = cum[i * sub - 1:i * sub, :] if i > 0 else jnp.zeros((1, D), F32)
            worst = jnp.maximum(worst, start - cum[(i + 1) * sub - 1:(i + 1) * sub, :])
        out_of_range = jnp.max(worst) > SAFE_DECAY_RANGE

        @pl.when(out_of_range)
        def _exact_scores():
            row_i = lax.broadcasted_iota(jnp.int32, (rows, 1), 0)
            for hd in range(N_HEADS):
                hs = head_cols(hd)
                qib_h = (qf_s[:, hs] * jnp.exp(cum_s[:, hs])).astype(BF16)
                o_s[:, hs] = _mm(qib_h, sold_s[hd].astype(BF16))

                def block_body(b, carry, hs=hs):
                    r0 = pl.multiple_of(b * SUBLANES, SUBLANES)
                    rs = pl.ds(r0, SUBLANES)
                    q_b, k_b, c_b, v_b = qf_s[rs, hs], kk_s[rs, hs], cum_s[rs, hs], v_s[rs, hs]
                    prev_start = pl.multiple_of(jnp.maximum(r0 - SUBLANES, 0), SUBLANES)
                    before = cum_s[pl.ds(prev_start, SUBLANES), hs][SUBLANES - 1:SUBLANES, :]
                    ref_row = jnp.where(b > 0, before, 0.0)
                    qd = (q_b * jnp.exp(c_b - ref_row)).astype(BF16)
                    arg = jnp.minimum(ref_row - cum_s[:, hs], 0.0)
                    kd = jnp.where(row_i < r0, kk_s[:, hs] * jnp.exp(arg), 0.0).astype(BF16)
                    earlier = _mm(_mm_nt(qd, kd).astype(BF16), v_s[:, hs].astype(BF16))
                    o_s[rs, hs] = o_s[rs, hs] + earlier + _exact_block(q_b, k_b, c_b, v_b)
                    return carry

                lax.fori_loop(0, rows // SUBLANES, block_body, 0)

            finish([o_s[:, head_cols(hd)] for hd in range(N_HEADS)])

    def states_prompt():
        kk = kk_s[...]
        cum = cum_s[...]
        vb = v_s[...].astype(BF16)
        last = cum[rows - 1:rows, :]
        ksb = (kk * jnp.exp(last - cum)).astype(BF16)
        decay_row = jnp.exp(last)
        for hd in range(N_HEADS):
            hs = head_cols(hd)
            sout_ref[0, hd] = _column_of(decay_row[:, hs]) * sout_ref[0, hd] + _mm_tn(ksb[:, hs], vb[:, hs])
        x1_ref[...] = jnp.zeros((seqs, tokens, D), F32)

    def decode_state_step():
        def seq_body(g, carry):
            rsl = pl.ds(pl.multiple_of(g * SUBLANES, SUBLANES), SUBLANES)
            s_old = s0_ref[g, 0]
            c_b = cum_h[j, rsl, :]
            total_b = tot_h[j, rsl, :]
            qi = qf_h[j, rsl, :] * jnp.exp(c_b)
            ks = kk_h[j, rsl, :] * jnp.exp(total_b - c_b)
            oin_h[j, rsl, :] = _mm(qi.astype(BF16), s_old.astype(BF16))
            upd = _mm_tn(ks.astype(BF16), v_h[j, rsl, :].astype(BF16))
            decay = _column_of(jnp.exp(total_b[0:1, :]))
            sout_ref[g, 0] = decay * s_old + upd
            return carry

        lax.fori_loop(0, seqs, seq_body, 0, unroll=SUBLANES)

    def back_decode():
        ri = lax.broadcasted_iota(jnp.int32, (rows, rows), 0)
        ci = lax.broadcasted_iota(jnp.int32, (rows, rows), 1)
        causal = jnp.logical_and(ci <= ri, (ri // tokens) == (ci // tokens))
        def masked_scores(hd):
            c_h = cum_h[hd]
            scores = _mm_nt((qf_h[hd] * jnp.exp(c_h)).astype(BF16), (kk_h[hd] * jnp.exp(-c_h)).astype(BF16))
            return jnp.where(causal, scores, 0.0).astype(BF16)

        o_heads = []
        att_next = masked_scores(0)
        for hd in range(N_HEADS):
            att = att_next
            if hd + 1 < N_HEADS:
                att_next = masked_scores(hd + 1)
            o_heads.append(oin_h[hd] + _mm(att, v_h[hd].astype(BF16)))
        finish(o_heads)

        worst = jnp.zeros((1, HEAD), F32)
        for hd in range(N_HEADS):
            worst = jnp.maximum(worst, jnp.max(-tot_h[hd], axis=0, keepdims=True))
        out_of_range = jnp.max(worst) > SAFE_DECAY_RANGE

        @pl.when(out_of_range)
        def _exact_scores():
            for hd in range(N_HEADS):
                def seq_body(g, carry, hd=hd):
                    rs = pl.ds(pl.multiple_of(g * SUBLANES, SUBLANES), SUBLANES)
                    oi_h[hd, rs, :] = _exact_block(qf_h[hd, rs, :], kk_h[hd, rs, :], cum_h[hd, rs, :], v_h[hd, rs, :])
                    return carry

                lax.fori_loop(0, seqs, seq_body, 0)

            finish([oin_h[hd] + oi_h[hd] for hd in range(N_HEADS)])

    if decode:
        @pl.when(j == 0)
        def _front_region():
            ubuf[:, 0:SUBLANES, :] = cext_ref[...]
            front()

        decode_state_step()

        @pl.when(j == N_HEADS - 1)
        def _back_region():
            back_decode()
    else:
        @pl.when(j == 0)
        def _init():
            ubuf[:, 0:SUBLANES, :] = cext_ref[...]
            hout_ref[...] = h0_ref[...]
            sout_ref[...] = s0_ref[...]

        if states_only:
            front()
            states_prompt()
        else:
            back_prompt(front())


def _resident(shape):
    return pl.BlockSpec(shape, lambda i, j: (0,) * len(shape), pipeline_mode=pl.Buffered(1))


def _mixer(x, cext, h0, s0, pvec, w_in, w_ri, w_a, w_b, w_out, *, seqs, tokens, reset_first, sub, name,
           states_only=False):
    n_seq, length, _ = x.shape
    n_tiles = length // tokens
    decode = seqs > 1
    assert n_seq % seqs == 0 and length % tokens == 0 and tokens % SUBLANES == 0
    assert not decode or (n_tiles == 1 and tokens == SUBLANES)
    assert (seqs * tokens) % sub == 0
    rows = seqs * tokens
    kern = functools.partial(_mixer_kernel, seqs=seqs, tokens=tokens, n_tiles=n_tiles,
                             reset_first=reset_first, sub=sub, states_only=states_only)
    tile_f32 = pltpu.VMEM((rows, D), F32)
    by_head_f32 = pltpu.VMEM((N_HEADS, rows, HEAD), F32)
    common = [pltpu.VMEM((seqs, SUBLANES + tokens, D), F32),
              tile_f32, tile_f32, tile_f32]
    if decode:
        grid = (n_seq // seqs, N_HEADS)
        x_map = lambda i, j: (i, 0, 0)
        s_block = (seqs, 1, HEAD, HEAD)
        s_map = lambda i, j: (i, j, 0, 0)
        scratch = common + [by_head_f32] * 7
    else:
        grid = (n_seq, n_tiles)
        x_map = lambda i, j: (i, j, 0)
        s_block = (1, N_HEADS, HEAD, HEAD)
        s_map = lambda i, j: (i, 0, 0, 0)
        scratch = common + [tile_f32] * 6 + [pltpu.VMEM((N_HEADS, HEAD, HEAD), F32)]
    seq_map = lambda i, j: (i, 0, 0)
    if cext.shape[0] == n_seq:
        in_seq_map, in_s_map = seq_map, s_map
    else:
        assert cext.shape[0] == h0.shape[0] == s0.shape[0] == seqs == 1
        in_seq_map = lambda i, j: (0, 0, 0)
        in_s_map = lambda i, j: (0, 0, 0, 0)
    return pl.pallas_call(
        kern,
        grid=grid,
        in_specs=[
            pl.BlockSpec((seqs, tokens, D), x_map),
            pl.BlockSpec((seqs, SUBLANES, D), in_seq_map),
            pl.BlockSpec((seqs, 1, D), in_seq_map),
            pl.BlockSpec(s_block, in_s_map),
            _resident((P_ROWS, D)),
            _resident((w_in.shape[0], (C_I + 1) * D) if states_only else w_in.shape),
            _resident(w_ri.shape),
            _resident(w_a.shape),
            _resident(w_b.shape),
            _resident(w_out.shape),
        ],
        out_specs=[
            pl.BlockSpec((seqs, tokens, D), x_map),
            pl.BlockSpec((seqs, SUBLANES, D), seq_map),
            pl.BlockSpec((seqs, 1, D), seq_map),
            pl.BlockSpec(s_block, s_map),
        ],
        out_shape=[
            jax.ShapeDtypeStruct((n_seq, length, D), F32),
            jax.ShapeDtypeStruct((n_seq, SUBLANES, D), F32),
            jax.ShapeDtypeStruct((n_seq, 1, D), F32),
            jax.ShapeDtypeStruct((n_seq, N_HEADS, HEAD, HEAD), F32),
        ],
        scratch_shapes=scratch,
        compiler_params=pltpu.CompilerParams(
            dimension_semantics=("arbitrary", "arbitrary"),
            vmem_limit_bytes=VMEM_LIMIT_BYTES),
        name=name,
    )(x, cext, h0, s0, pvec, w_in, w_ri, w_a, w_b, w_out)


def _mlp_kernel(xa_ref, xb_ref, g_ref, wup_ref, wdn_ref, oa_ref, ob_ref, *, steps_a):
    def tile_mlp(x_ref, o_ref):
        x = x_ref[...]
        hn = _rms(x, g_ref[0:1, :]).astype(BF16)
        acc = jnp.zeros(x.shape, F32)
        for c in range(D_FF // D):
            t = _mm(hn, _unpack_rows(wup_ref[:, c * D:(c + 1) * D]))
            t = jnp.square(jnp.maximum(t, 0.0)).astype(BF16)
            acc = acc + _mm(t, _unpack_rows(wdn_ref[c * (D // 2):(c + 1) * (D // 2), :]))
        o_ref[...] = x + _rms(acc, g_ref[1:2, :])

    i = pl.program_id(0)

    @pl.when(i < steps_a)
    def _first():
        tile_mlp(xa_ref, oa_ref)

    @pl.when(i >= steps_a)
    def _second():
        tile_mlp(xb_ref, ob_ref)


def _mlp(xa, xb, gains, w_up, w_down, *, tile, name):
    na, nb = xa.shape[0], xb.shape[0]
    assert na % tile == 0 and nb % tile == 0
    steps_a, steps_b = na // tile, nb // tile
    a_map = lambda i: (jnp.minimum(i, steps_a - 1), 0)
    b_map = lambda i: (jnp.maximum(i - steps_a, 0), 0)
    return pl.pallas_call(
        functools.partial(_mlp_kernel, steps_a=steps_a),
        grid=(steps_a + steps_b,),
        in_specs=[
            pl.BlockSpec((tile, D), a_map),
            pl.BlockSpec((tile, D), b_map),
            pl.BlockSpec((2, D), lambda i: (0, 0), pipeline_mode=pl.Buffered(1)),
            pl.BlockSpec(w_up.shape, lambda i: (0, 0), pipeline_mode=pl.Buffered(1)),
            pl.BlockSpec(w_down.shape, lambda i: (0, 0), pipeline_mode=pl.Buffered(1)),
        ],
        out_specs=[pl.BlockSpec((tile, D), a_map), pl.BlockSpec((tile, D), b_map)],
        out_shape=[jax.ShapeDtypeStruct((na, D), F32), jax.ShapeDtypeStruct((nb, D), F32)],
        compiler_params=pltpu.CompilerParams(
            dimension_semantics=("arbitrary",),
            vmem_limit_bytes=VMEM_LIMIT_BYTES),
        name=name,
    )(xa, xb, gains, w_up, w_down)


def kernel(x_prompt, x_sample, state_conv, state_rglru, state_hgrn, meta_tokens, norm_gains, w_in,
           conv_w, conv_b, rg_w, rg_b, ig_w, ig_b, lru_lambda, hgrn_lb, hgrn_gnorm,
           w_branch_a, w_branch_b, w_out, w_up, w_down):
    bp, seq, _ = x_prompt.shape
    bs, dec_seq, _ = x_sample.shape
    layer = 0
    pvec = jnp.concatenate([
        norm_gains[layer], conv_w[layer], conv_b[layer][None], 0.5 * rg_b[layer][None], 0.5 * ig_b[layer][None],
        lru_lambda[layer][None], hgrn_lb[layer:layer + 2], hgrn_gnorm[layer][None],
        jnp.zeros((1, D), F32)], axis=0).astype(F32)
    halved = (C_GATE, C_Q, C_OG, C_MA, C_MB)
    scale_in = jnp.concatenate([jnp.full((1, D), 0.5 if g in halved else 1.0, F32) for g in range(8)], axis=1)

    gate_blocks = [w.reshape(N_HEADS * HEAD, HEAD) for w in (rg_w[layer], ig_w[layer])]
    w_in_b, w_ri, w_a, w_b, w_o, w_u, w_d = _pack_weights([
        ([w_in[layer]], scale_in),
        (gate_blocks, 0.5),
        ([w_branch_a[layer]], None),
        ([w_branch_b[layer]], None),
        ([w_out[layer]], None),
        ([w_up[layer]], None),
        ([w_down[layer]], None),
    ], name="pack_weights")
    w_ri = w_ri.reshape(N_HEADS, HEAD // 2, 2 * HEAD)
    weights = (pvec, w_in_b, w_ri, w_a, w_b, w_o)

    zeros_c = jnp.zeros((1, SUBLANES, D), F32)
    zeros_h = jnp.zeros((1, 1, D), F32)
    zeros_s = jnp.zeros((1, N_HEADS, HEAD, HEAD), F32)
    _, c_m, h_m, s_m = _mixer(meta_tokens[None].astype(F32), zeros_c, zeros_h, zeros_s,
                              pvec, w_in_b, w_ri, pvec, pvec, pvec,
                              seqs=1, tokens=N_META, reset_first=True, sub=N_META, name="mixer_meta",
                              states_only=True)

    x1_p, c_p, h_p, s_p = _mixer(
        x_prompt,
        c_m, h_m, s_m,
        *weights, seqs=1, tokens=PROMPT_TILE_TOKENS, reset_first=False, sub=HGRN_SUB_CHUNK, name="mixer_prompt")

    cext_s = jnp.pad(state_conv[layer], ((0, 0), (SUBLANES - 3, 0), (0, 0)))
    x1_s, c_s, h_s, s_s = _mixer(
        x_sample, cext_s, state_rglru[layer][:, None, :], state_hgrn[layer],
        *weights, seqs=DECODE_GROUP_SEQS, tokens=dec_seq, reset_first=False, sub=dec_seq, name="mixer_sample")

    gains_mlp = norm_gains[layer, 2:4]
    y_p, y_s = _mlp(x1_p.reshape(bp * seq, D), x1_s.reshape(bs * dec_seq, D), gains_mlp, w_u, w_d,
                    tile=MLP_TILE_ROWS, name="mlp")

    return (y_p.reshape(bp, seq, D), y_s.reshape(bs, dec_seq, D),
            c_p[:, SUBLANES - 3:, :][None], h_p[:, 0, :][None], s_p[None],
            c_s[:, SUBLANES - 3:, :][None], h_s[:, 0, :][None], s_s[None])
```
